```python
import math
import jax, jax.numpy as jnp
from jax import lax
import numpy as np

D_MODEL = 1024
BATCH = 8
SEQ = 4096
DEPTH = 2

GRID_W = 64
CTX_LEN = 256
D_FF = 2816
N_MOD = 9
EPS = 1e-6
NEG = -1e30

SSM_WIDTH = 384
SSM_GROUP = 16
SSM_GROUPS = SSM_WIDTH // SSM_GROUP
SSM_STATE = 64
DT_MIN = 1e-3
DT_MAX = 1e-1

HEAD_DIM = 64
GQA_HEADS = 8
GQA_KV_HEADS = 2
GQA_WIDTH = GQA_HEADS * HEAD_DIM
GQA_KV_WIDTH = GQA_KV_HEADS * HEAD_DIM
WINDOW = 128
Q_BLOCK = 128
ROPE_THETA = 10000.0

NA_HEADS = 8
NA_WIDTH = NA_HEADS * HEAD_DIM
NA_ROWS_MAX = 8
NA_COLS = 16

N_BRANCH = 3
IN_SPLITS = (SSM_WIDTH, GQA_WIDTH, GQA_KV_WIDTH, GQA_KV_WIDTH, NA_WIDTH, NA_WIDTH, NA_WIDTH, N_BRANCH * D_MODEL)
N_IN = SSM_WIDTH + GQA_WIDTH + 2 * GQA_KV_WIDTH + 3 * NA_WIDTH + N_BRANCH * D_MODEL

kernel_name = "hybrid_s5_swa_natten_prefix_dit_block"


def rmsnorm(x, g):
    xf = x.astype(jnp.float32)
    y = xf * lax.rsqrt(jnp.mean(xf * xf, axis=-1, keepdims=True) + EPS)
    return (y * g.astype(jnp.float32)).astype(x.dtype)


def modulate(h, shift, scale):
    return h * (1.0 + scale) + shift


def swiglu(h, wg, wu, wd):
    return (jax.nn.silu(h @ wg) * (h @ wu)) @ wd


def split_in(z):
    offs = np.cumsum(IN_SPLITS)[:-1]
    return jnp.split(z, [int(o) for o in offs], axis=-1)


def to_heads(t, n):
    return t.reshape(t.shape[:2] + (n, HEAD_DIM))


def joint_softmax(parts, sink=None):
    sizes = [p.shape[-1] for p in parts]
    cols = list(parts)
    if sink is not None:
        cols.append(jnp.broadcast_to(sink, parts[0].shape[:-1] + (1,)))
    probs = jax.nn.softmax(jnp.concatenate(cols, axis=-1), axis=-1)
    offs = np.cumsum(sizes)
    return [probs[..., int(o) - s:int(o)] for o, s in zip(offs, sizes)]


def axial_rope_tables(n_tokens):
    t = jnp.arange(n_tokens)
    pos = jnp.stack([t // GRID_W, t % GRID_W], axis=-1).astype(jnp.float32)
    half = HEAD_DIM // 2
    inv = ROPE_THETA ** (-jnp.arange(0, half, 2, dtype=jnp.float32) / half)
    ang = pos[:, :, None] * inv
    return jnp.cos(ang), jnp.sin(ang)


def apply_axial_rope(x, cos, sin):
    b_, l_, h_, _ = x.shape
    xs = x.reshape(b_, l_, h_, 2, 2, HEAD_DIM // 4)
    x1, x2 = xs[..., 0, :], xs[..., 1, :]
    cs, sn = cos[None, :, None], sin[None, :, None]
    out = jnp.stack([x1 * cs - x2 * sn, x1 * sn + x2 * cs], axis=-2)
    return out.reshape(x.shape).astype(x.dtype)


def s5_discretize(a_re, a_im, log_dt, b_re, b_im):
    f32 = jnp.float32
    lam = lax.complex(a_re.astype(f32), a_im.astype(f32))
    dt = jnp.exp(log_dt.astype(f32))[:, None]
    lam_bar = jnp.exp(lam * dt)
    b = lax.complex(b_re.astype(f32), b_im.astype(f32))
    b_bar = ((lam_bar - 1.0) / lam)[..., None] * b
    return lam_bar, b_bar


def s5_scan(u, lam_bar, b_bar, s0):
    bu = lax.complex(jnp.einsum('blgi,gpi->blgp', u, b_bar.real),
                     jnp.einsum('blgi,gpi->blgp', u, b_bar.imag))
    if s0 is not None:
        bu = bu.at[:, 0].add(lam_bar * s0)
    a = jnp.broadcast_to(lam_bar, (1, u.shape[1]) + lam_bar.shape)

    def combine(left, right):
        a_l, b_l = left
        a_r, b_r = right
        return a_l * a_r, a_r * b_l + b_r

    _, states = lax.associative_scan(combine, (a, bu), axis=1)
    return states


def s5_readout(states, c_re, c_im):
    return (jnp.einsum('blgp,gip->blgi', states.real, c_re)
            - jnp.einsum('blgp,gip->blgi', states.imag, c_im))


def s5_mixer(u, uc, p, ctx_out):
    f32 = jnp.float32
    b_, l_, _ = u.shape
    n_c = uc.shape[1]
    ul = u.astype(f32).reshape(b_, l_, SSM_GROUPS, SSM_GROUP)
    ucg = uc.astype(f32).reshape(b_, n_c, SSM_GROUPS, SSM_GROUP)
    d = p['ssm_d'].astype(f32).reshape(SSM_GROUPS, SSM_GROUP)
    y = d * ul
    yc = d * ucg if ctx_out else None
    for direction in range(2):
        lam_bar, b_bar = s5_discretize(p['ssm_a_re'][direction], p['ssm_a_im'][direction],
                                       p['ssm_log_dt'][direction], p['ssm_b_re'][direction], p['ssm_b_im'][direction])
        c_re = p['ssm_c_re'][direction].astype(f32)
        c_im = p['ssm_c_im'][direction].astype(f32)
        rev = (lambda t: jnp.flip(t, axis=1)) if direction == 1 else (lambda t: t)
        st_c = s5_scan(rev(ucg), lam_bar, b_bar, None)
        st_l = s5_scan(rev(ul), lam_bar, b_bar, st_c[:, -1])
        y = y + rev(s5_readout(st_l, c_re, c_im))
        if ctx_out:
            yc = yc + rev(s5_readout(st_c, c_re, c_im))
    w_glu = p['ssm_w_glu']

    def glu(t):
        t = jax.nn.gelu(t.reshape(t.shape[:2] + (SSM_WIDTH,)))
        return (t * jax.nn.sigmoid(t @ w_glu)).astype(u.dtype)

    return glu(y), (glu(yc) if ctx_out else None)


def window_gqa_latent(q, k, v, kc, vc, sink):
    b_, l_, h_, dh = q.shape
    grp = h_ // GQA_KV_HEADS
    nb = l_ // Q_BLOCK
    span = Q_BLOCK + 2 * WINDOW
    scale = dh ** -0.5
    pad = ((0, 0), (WINDOW, WINDOW), (0, 0), (0, 0))
    kp, vp = jnp.pad(k, pad), jnp.pad(v, pad)
    qb = q.reshape(b_, nb, Q_BLOCK, GQA_KV_HEADS, grp, dh).transpose(1, 0, 2, 3, 4, 5)
    sink_l = sink.astype(jnp.float32).reshape(GQA_KV_HEADS, grp)[None, :, :, None, None]

    def block(args):
        i, qi = args
        start = i * Q_BLOCK
        kb = lax.dynamic_slice_in_dim(kp, start, span, axis=1)
        vb = lax.dynamic_slice_in_dim(vp, start, span, axis=1)
        qpos = start + jnp.arange(Q_BLOCK)
        kpos = start - WINDOW + jnp.arange(span)
        valid = ((jnp.abs(qpos[:, None] - kpos[None, :]) <= WINDOW)
                 & (kpos >= 0)[None, :] & (kpos < l_)[None, :])
        s_loc = jnp.einsum('bqkgd,bskd->bkgqs', qi, kb, preferred_element_type=jnp.float32) * scale
        s_loc = jnp.where(valid, s_loc, NEG)
        s_ctx = jnp.einsum('bqkgd,bckd->bkgqc', qi, kc, preferred_element_type=jnp.float32) * scale
        p_loc, p_ctx = joint_softmax([s_loc, s_ctx], sink_l)
        return (jnp.einsum('bkgqs,bskd->bqkgd', p_loc.astype(v.dtype), vb)
                + jnp.einsum('bkgqc,bckd->bqkgd', p_ctx.astype(v.dtype), vc))

    out = lax.map(block, (jnp.arange(nb), qb))
    return out.transpose(1, 0, 2, 3, 4, 5).reshape(b_, l_, h_ * dh)


def neighborhood_attn_latent(q, k, v, kc, vc, rpb):
    b_, l_, h_, dh = q.shape
    rows = l_ // GRID_W
    kh = min(NA_ROWS_MAX, rows)
    kw = NA_COLS
    scale = dh ** -0.5
    qg = q.reshape(b_, rows, GRID_W, h_, dh).transpose(1, 0, 2, 3, 4)
    kg = k.reshape(b_, rows, GRID_W, h_, dh)
    vg = v.reshape(b_, rows, GRID_W, h_, dh)
    cols = np.arange(GRID_W)
    col_start = np.clip(cols - kw // 2, 0, GRID_W - kw)
    col_idx = col_start[:, None] + np.arange(kw)[None, :]
    col_bias_idx = col_idx - cols[:, None] + (kw - 1)
    rpb_c = rpb.astype(jnp.float32)[:, :, col_bias_idx]

    def row_block(args):
        r, qr = args
        rs = jnp.clip(r - kh // 2, 0, rows - kh)
        kband = lax.dynamic_slice_in_dim(kg, rs, kh, axis=1)
        vband = lax.dynamic_slice_in_dim(vg, rs, kh, axis=1)
        k_nb = kband[:, :, col_idx]
        v_nb = vband[:, :, col_idx]
        row_bias_idx = rs + jnp.arange(kh) - r + (NA_ROWS_MAX - 1)
        bias = rpb_c[:, row_bias_idx].transpose(0, 2, 1, 3)
        s_loc = jnp.einsum('bchd,bicjhd->bhcij', qr, k_nb, preferred_element_type=jnp.float32) * scale + bias[None]
        s_loc = s_loc.reshape(b_, h_, GRID_W, kh * kw)
        s_ctx = jnp.einsum('bchd,bkhd->bhck', qr, kc, preferred_element_type=jnp.float32) * scale
        p_loc, p_ctx = joint_softmax([s_loc, s_ctx])
        p_loc = p_loc.reshape(b_, h_, GRID_W, kh, kw)
        return (jnp.einsum('bhcij,bicjhd->bchd', p_loc.astype(v.dtype), v_nb)
                + jnp.einsum('bhck,bkhd->bchd', p_ctx.astype(v.dtype), vc))

    out = lax.map(row_block, (jnp.arange(rows), qg))
    return out.transpose(1, 0, 2, 3, 4).reshape(b_, l_, h_ * dh)


def context_self_attn(q, k, v, sink):
    b_, n_c, h_, dh = q.shape
    hkv = k.shape[2]
    grp = h_ // hkv
    qg = q.reshape(b_, n_c, hkv, grp, dh)
    s = jnp.einsum('bqkgd,bckd->bkgqc', qg, k, preferred_element_type=jnp.float32) * dh ** -0.5
    sk = None if sink is None else sink.astype(jnp.float32).reshape(hkv, grp)[None, :, :, None, None]
    (pr,) = joint_softmax([s], sk)
    o = jnp.einsum('bkgqc,bckd->bqkgd', pr.astype(v.dtype), v)
    return o.reshape(b_, n_c, h_ * dh)


def merge_branches(y_ssm, y_gqa, y_na, gates, p):
    g_s, g_a, g_n = jnp.split(gates, N_BRANCH, axis=-1)
    m = (jax.nn.sigmoid(g_s) * (y_ssm @ p['w_p_ssm'])
         + jax.nn.sigmoid(g_a) * (y_gqa @ p['w_p_gqa'])
         + jax.nn.sigmoid(g_n) * (y_na @ p['w_p_na']))
    return m @ p['w_out']


def ffn_half(h_stream, m, gain, w):
    h = modulate(rmsnorm(h_stream, gain), m[0], m[1])
    return h_stream + 0.5 * m[2] * swiglu(h, *w)


def trunk_layer(x, xc, p, cos, sin, ctx_out):
    ml = jnp.split(p['mod_lat'], N_MOD, axis=-1)
    mc = jnp.split(p['mod_ctx'], N_MOD, axis=-1)
    g = p['norm_g']
    x = ffn_half(x, ml[0:3], g[0], p['ffn1'])
    xc = ffn_half(xc, mc[0:3], g[0], p['ffn1'])
    h = modulate(rmsnorm(x, g[1]), ml[3], ml[4])
    hc = modulate(rmsnorm(xc, g[1]), mc[3], mc[4])
    u, gq, gk, gv, nq, nk, nv, gates = split_in(h @ p['w_in'])
    uc, gqc, gkc, gvc, nqc, nkc, nvc, gates_c = split_in(hc @ p['w_in'])
    y_ssm, y_ssm_c = s5_mixer(u, uc, p, ctx_out)
    kc_a, vc_a = to_heads(gkc, GQA_KV_HEADS), to_heads(gvc, GQA_KV_HEADS)
    y_gqa = window_gqa_latent(apply_axial_rope(to_heads(gq, GQA_HEADS), cos, sin),
                              apply_axial_rope(to_heads(gk, GQA_KV_HEADS), cos, sin),
                              to_heads(gv, GQA_KV_HEADS), kc_a, vc_a, p['gqa_sink'])
    kc_n, vc_n = to_heads(nkc, NA_HEADS), to_heads(nvc, NA_HEADS)
    y_na = neighborhood_attn_latent(to_heads(nq, NA_HEADS), to_heads(nk, NA_HEADS), to_heads(nv, NA_HEADS),
                                    kc_n, vc_n, p['na_rpb'])
    x = x + ml[5] * merge_branches(y_ssm, y_gqa, y_na, gates, p)
    if ctx_out:
        y_gqa_c = context_self_attn(to_heads(gqc, GQA_HEADS), kc_a, vc_a, p['gqa_sink'])
        y_na_c = context_self_attn(to_heads(nqc, NA_HEADS), kc_n, vc_n, None)
        xc = xc + mc[5] * merge_branches(y_ssm_c, y_gqa_c, y_na_c, gates_c, p)
    x = ffn_half(x, ml[6:9], g[2], p['ffn2'])
    if ctx_out:
        xc = ffn_half(xc, mc[6:9], g[2], p['ffn2'])
    return x, xc


def setup_inputs(seed: int = 0) -> dict:
    key = jax.random.key(seed)
    ks = iter(jax.random.split(key, 40))
    f32 = jnp.float32

    def nrm(shape, s):
        return jax.random.normal(next(ks), shape, f32) * s

    D, F, G, P, I = D_MODEL, D_FF, SSM_GROUPS, SSM_STATE, SSM_GROUP
    return {
        'x': nrm((BATCH, SEQ, D), 1.0),
        'c': nrm((BATCH, D), 1.0),
        'ctx': nrm((BATCH, CTX_LEN, D), 1.0),
        'c_ctx': nrm((D,), 1.0),
        'w_ada': nrm((DEPTH, D, N_MOD * D), 0.5 * D ** -0.5),
        'b_ada': nrm((DEPTH, N_MOD * D), 0.02),
        'norm_g': 1.0 + nrm((DEPTH, 3, D), 0.02),
        'ffn1_wg': nrm((DEPTH, D, F), D ** -0.5),
        'ffn1_wu': nrm((DEPTH, D, F), D ** -0.5),
        'ffn1_wd': nrm((DEPTH, F, D), F ** -0.5),
        'ffn2_wg': nrm((DEPTH, D, F), D ** -0.5),
        'ffn2_wu': nrm((DEPTH, D, F), D ** -0.5),
        'ffn2_wd': nrm((DEPTH, F, D), F ** -0.5),
        'w_in': nrm((DEPTH, D, N_IN), D ** -0.5),
        'ssm_a_re': -0.5 + nrm((DEPTH, 2, G, P), 0.01),
        'ssm_a_im': math.pi * jnp.arange(P, dtype=f32) + nrm((DEPTH, 2, G, P), 0.01),
        'ssm_log_dt': jax.random.uniform(next(ks), (DEPTH, 2, G), f32, math.log(DT_MIN), math.log(DT_MAX)),
        'ssm_b_re': nrm((DEPTH, 2, G, P, I), (2 * I) ** -0.5),
        'ssm_b_im': nrm((DEPTH, 2, G, P, I), (2 * I) ** -0.5),
        'ssm_c_re': nrm((DEPTH, 2, G, I, P), P ** -0.5),
        'ssm_c_im': nrm((DEPTH, 2, G, I, P), P ** -0.5),
        'ssm_d': nrm((DEPTH, SSM_WIDTH), 1.0),
        'ssm_w_glu': nrm((DEPTH, SSM_WIDTH, SSM_WIDTH), SSM_WIDTH ** -0.5),
        'gqa_sink': nrm((DEPTH, GQA_HEADS), 0.5),
        'na_rpb': nrm((DEPTH, NA_HEADS, 2 * NA_ROWS_MAX - 1, 2 * NA_COLS - 1), 0.1),
        'w_p_ssm': nrm((DEPTH, SSM_WIDTH, D), SSM_WIDTH ** -0.5),
        'w_p_gqa': nrm((DEPTH, GQA_WIDTH, D), GQA_WIDTH ** -0.5),
        'w_p_na': nrm((DEPTH, NA_WIDTH, D), NA_WIDTH ** -0.5),
        'w_out': nrm((DEPTH, D, D), D ** -0.5),
        'final_g': 1.0 + nrm((D,), 0.02),
    }


def reference(x, c, ctx, c_ctx, w_ada, b_ada, norm_g, ffn1_wg, ffn1_wu, ffn1_wd, ffn2_wg, ffn2_wu, ffn2_wd,
              w_in, ssm_a_re, ssm_a_im, ssm_log_dt, ssm_b_re, ssm_b_im, ssm_c_re, ssm_c_im, ssm_d, ssm_w_glu,
              gqa_sink, na_rpb, w_p_ssm, w_p_gqa, w_p_na, w_out, final_g):
    cos, sin = axial_rope_tables(x.shape[1])
    xc = ctx
    s_lat = jax.nn.silu(c)
    s_ctx = jax.nn.silu(c_ctx)
    for l in range(DEPTH):
        p = {
            'mod_lat': (s_lat @ w_ada[l] + b_ada[l])[:, None, :],
            'mod_ctx': (s_ctx @ w_ada[l] + b_ada[l])[None, None, :],
            'norm_g': norm_g[l],
            'ffn1': (ffn1_wg[l], ffn1_wu[l], ffn1_wd[l]),
            'ffn2': (ffn2_wg[l], ffn2_wu[l], ffn2_wd[l]),
            'w_in': w_in[l],
            'ssm_a_re': ssm_a_re[l], 'ssm_a_im': ssm_a_im[l], 'ssm_log_dt': ssm_log_dt[l],
            'ssm_b_re': ssm_b_re[l], 'ssm_b_im': ssm_b_im[l],
            'ssm_c_re': ssm_c_re[l], 'ssm_c_im': ssm_c_im[l],
            'ssm_d': ssm_d[l], 'ssm_w_glu': ssm_w_glu[l],
            'gqa_sink': gqa_sink[l], 'na_rpb': na_rpb[l],
            'w_p_ssm': w_p_ssm[l], 'w_p_gqa': w_p_gqa[l], 'w_p_na': w_p_na[l], 'w_out': w_out[l],
        }
        x, xc = trunk_layer(x, xc, p, cos, sin, l < DEPTH - 1)
    return rmsnorm(x, final_g)
```

```python
import functools
import math

import jax
import jax.numpy as jnp
import numpy as np
from jax import lax
from jax.experimental import pallas as pl
from jax.experimental.pallas import tpu as pltpu

F32 = jnp.float32
BF16 = jnp.bfloat16

EPS = 1e-6
NEG = -1e30
N_MOD = 9
GRID_W = 64
HEAD_DIM = 64
GQA_HEADS = 8
GQA_KV_HEADS = 2
NA_HEADS = 8
NA_ROWS = 8
NA_COLS = 16
Q_BLOCK = 128
ROPE_THETA = 10000.0
SSM_GROUP = 16
SSM_GROUPS = 24
SSM_STATE = 64
SSM_WIDTH = SSM_GROUP * SSM_GROUPS
GQA_WIDTH = GQA_HEADS * HEAD_DIM
GQA_KV_WIDTH = GQA_KV_HEADS * HEAD_DIM
NA_WIDTH = NA_HEADS * HEAD_DIM
N_BRANCH = 3

LANES = 128
MOD_ROWS = 16
TOKEN_TILE = 256
FFN_CHUNK = 256
SSM_CHUNK = 64
VMEM_LIMIT = 56 * 1024 * 1024
GQA_HEAD_PERM = (0, 4, 1, 5, 2, 6, 3, 7)


def _params(n_axes):
    return pltpu.CompilerParams(dimension_semantics=("arbitrary",) * n_axes, vmem_limit_bytes=VMEM_LIMIT)


def _resident(shape):
    return pl.BlockSpec(shape, lambda *_: (0,) * len(shape), pipeline_mode=pl.Buffered(1))


def _dot(a, b):
    return jnp.dot(a, b, preferred_element_type=F32)


def _dot_t(a, b):
    return lax.dot_general(a, b, (((1,), (1,)), ((), ())), preferred_element_type=F32)


def _rmsnorm(x, g):
    return x * lax.rsqrt(jnp.mean(x * x, axis=-1, keepdims=True) + EPS) * g


def _mod_norm(x, g, shift, scale):
    return _rmsnorm(x, g) * (1.0 + scale) + shift


def _ada_kernel(c_ref, w_ref, b_ref, o_ref):
    s = c_ref[...]
    s = s * jax.nn.sigmoid(s)
    o_ref[0] = _dot(s.astype(BF16), w_ref[0].astype(BF16)) + b_ref[0]


def _ada(cin, w_ada, b_ada):
    depth, d, n = w_ada.shape
    tn = n // 4
    return pl.pallas_call(
        _ada_kernel,
        grid=(depth, n // tn),
        in_specs=[
            pl.BlockSpec((MOD_ROWS, d), lambda l, j: (0, 0)),
            pl.BlockSpec((1, d, tn), lambda l, j: (l, 0, j)),
            pl.BlockSpec((1, 1, tn), lambda l, j: (l, 0, j)),
        ],
        out_specs=pl.BlockSpec((1, MOD_ROWS, tn), lambda l, j: (l, 0, j)),
        out_shape=jax.ShapeDtypeStruct((depth, MOD_ROWS, n), F32),
        compiler_params=_params(2),
        name="ada_mod",
    )(cin, w_ada, b_ada.reshape(depth, 1, n))


def _tok_spec(width, tm):
    return pl.BlockSpec((1, tm, width), lambda b, j: (b, j, 0))


def _mod_spec(d, n_lat_tiles, ctx_row):
    return pl.BlockSpec((1, N_MOD, d), lambda b, j: (jnp.where(j >= n_lat_tiles, ctx_row, b), 0, 0))


def _ffn_kernel(x_ref, mod_ref, g_ref, wg_ref, wu_ref, wd_ref, *rest, k0, gi, final):
    if final:
        fg_ref, o_ref, t_ref = rest
    else:
        o_ref, t_ref = rest
    x = x_ref[0]
    h = _mod_norm(x, g_ref[gi:gi + 1, :], mod_ref[0, k0:k0 + 1, :], mod_ref[0, k0 + 1:k0 + 2, :]).astype(BF16)
    f = wg_ref.shape[1]
    for c0 in range(0, f, FFN_CHUNK):
        a = _dot(h, wg_ref[:, c0:c0 + FFN_CHUNK])
        u = _dot(h, wu_ref[:, c0:c0 + FFN_CHUNK])
        t_ref[:, c0:c0 + FFN_CHUNK] = (a * jax.nn.sigmoid(a) * u).astype(BF16)
    y = _dot(t_ref[...], wd_ref[...])
    out = x + (0.5 * mod_ref[0, k0 + 2:k0 + 3, :]) * y
    if final:
        out = _rmsnorm(out, fg_ref[...])
    o_ref[0] = out


def _ffn(xs, mod, g, wg, wu, wd, *, k0, gi, n_tok, n_lat_tiles, ctx_row, final_g=None):
    b, _, d = xs.shape
    f = wg.shape[1]
    tm = TOKEN_TILE
    final = final_g is not None
    in_specs = [_tok_spec(d, tm), _mod_spec(d, n_lat_tiles, ctx_row), _resident(g.shape),
                _resident((d, f)), _resident((d, f)), _resident((f, d))]
    args = [xs, mod, g, wg, wu, wd]
    if final:
        in_specs.append(_resident((1, d)))
        args.append(final_g.reshape(1, d))
    return pl.pallas_call(
        functools.partial(_ffn_kernel, k0=k0, gi=gi, final=final),
        grid=(b, n_tok // tm),
        in_specs=in_specs,
        out_specs=_tok_spec(d, tm),
        out_shape=jax.ShapeDtypeStruct((b, n_tok, d), F32),
        scratch_shapes=[pltpu.VMEM((tm, f), BF16)],
        compiler_params=_params(2),
        name="ffn_half_final" if final else "ffn_half",
    )(*args)


def _inproj_kernel(x_ref, mod_ref, g_ref, w_ref, cos_ref, sa_ref, sb_ref,
                   u_ref, q_ref, k_ref, v_ref, nq_ref, nk_ref, nv_ref, gt_ref):
    x = x_ref[0]
    h = _mod_norm(x, g_ref[1:2, :], mod_ref[0, 3:4, :], mod_ref[0, 4:5, :]).astype(BF16)
    cos, sa, sb = cos_ref[...], sa_ref[...], sb_ref[...]
    scale = HEAD_DIM ** -0.5

    def mm(c0, n):
        return _dot(h, w_ref[:, c0:c0 + n])

    def rope(z):
        return z * cos + pltpu.roll(z, 16, 1) * sa + pltpu.roll(z, LANES - 16, 1) * sb

    c0 = 0
    u_ref[0] = mm(c0, SSM_WIDTH)
    c0 += SSM_WIDTH
    for j in range(GQA_WIDTH // LANES):
        q_ref[0, :, j * LANES:(j + 1) * LANES] = (rope(mm(c0, LANES)) * scale).astype(BF16)
        c0 += LANES
    k_ref[0] = rope(mm(c0, GQA_KV_WIDTH)).astype(BF16)
    c0 += GQA_KV_WIDTH
    v_ref[0] = mm(c0, GQA_KV_WIDTH).astype(BF16)
    c0 += GQA_KV_WIDTH
    nq_ref[0] = (mm(c0, NA_WIDTH) * scale).astype(BF16)
    c0 += NA_WIDTH
    nk_ref[0] = mm(c0, NA_WIDTH).astype(BF16)
    c0 += NA_WIDTH
    nv_ref[0] = mm(c0, NA_WIDTH).astype(BF16)
    c0 += NA_WIDTH
    n_gate = gt_ref.shape[2]
    for g0 in range(0, n_gate, 512):
        gt_ref[0, :, g0:g0 + 512] = mm(c0 + g0, 512)


def _inproj(xs, mod, g, w_in, cos_t, sa_t, sb_t, *, n_lat_tiles, ctx_row):
    b, s, d = xs.shape
    tm = TOKEN_TILE
    n_in = w_in.shape[1]
    n_gate = n_in - (SSM_WIDTH + GQA_WIDTH + 2 * GQA_KV_WIDTH + 3 * NA_WIDTH)
    widths = (SSM_WIDTH, GQA_WIDTH, GQA_KV_WIDTH, GQA_KV_WIDTH, NA_WIDTH, NA_WIDTH, NA_WIDTH, n_gate)
    dtypes = (F32, BF16, BF16, BF16, BF16, BF16, BF16, F32)
    rope_spec = pl.BlockSpec((tm, LANES), lambda bb, j: (j, 0))
    return pl.pallas_call(
        _inproj_kernel,
        grid=(b, s // tm),
        in_specs=[_tok_spec(d, tm), _mod_spec(d, n_lat_tiles, ctx_row), _resident(g.shape),
                  _resident((d, n_in)), rope_spec, rope_spec, rope_spec],
        out_specs=[_tok_spec(w, tm) for w in widths],
        out_shape=[jax.ShapeDtypeStruct((b, s, w), dt) for w, dt in zip(widths, dtypes)],
        compiler_params=_params(2),
        name="mixer_in_proj",
    )(xs, mod, g, w_in, cos_t, sa_t, sb_t)


def _gqa_kernel(sink_ref, q_ref, kp_ref, kc_ref, kn_ref, kx_ref, vp_ref, vc_ref, vn_ref, vx_ref, o_ref, *, n_lat):
    i = pl.program_id(1)
    far = 4 * Q_BLOCK
    row = lax.broadcasted_iota(jnp.int32, (Q_BLOCK, Q_BLOCK), 0)
    col = lax.broadcasted_iota(jnp.int32, (Q_BLOCK, Q_BLOCK), 1)
    is_lat = i < n_lat
    ok_p = col >= row + jnp.where(jnp.logical_and(is_lat, i > 0), 0, far)
    ok_c = col >= jnp.where(is_lat, 0, far)
    ok_n = col <= row - jnp.where(jnp.logical_and(is_lat, i < n_lat - 1), 0, far)
    lane = lax.broadcasted_iota(jnp.int32, (Q_BLOCK, LANES), 1)
    low = lane < HEAD_DIM
    kp, kc, kn, kx = kp_ref[0], kc_ref[0], kn_ref[0], kx_ref[0]
    vp, vc, vn, vx = vp_ref[0], vc_ref[0], vn_ref[0], vx_ref[0]
    for j in range(GQA_WIDTH // LANES):
        qp = q_ref[0, :, j * LANES:(j + 1) * LANES]
        outs = []
        for half in range(2):
            qm = jnp.where(low if half == 0 else jnp.logical_not(low), qp, jnp.zeros_like(qp))
            sp = jnp.where(ok_p, _dot_t(qm, kp), NEG)
            sc = jnp.where(ok_c, _dot_t(qm, kc), NEG)
            sn = jnp.where(ok_n, _dot_t(qm, kn), NEG)
            sx = _dot_t(qm, kx)
            sink = sink_ref[GQA_HEAD_PERM[2 * j + half]]
            m = jnp.maximum(jnp.maximum(jnp.max(sp, axis=1, keepdims=True), jnp.max(sc, axis=1, keepdims=True)),
                            jnp.maximum(jnp.max(sn, axis=1, keepdims=True), jnp.max(sx, axis=1, keepdims=True)))
            m = jnp.maximum(m, sink)
            ep, ec, en, ex = jnp.exp(sp - m), jnp.exp(sc - m), jnp.exp(sn - m), jnp.exp(sx - m)
            den = (jnp.sum(ep, axis=1, keepdims=True) + jnp.sum(ec, axis=1, keepdims=True)
                   + jnp.sum(en, axis=1, keepdims=True) + jnp.sum(ex, axis=1, keepdims=True) + jnp.exp(sink - m))
            o = (_dot(ep.astype(BF16), vp) + _dot(ec.astype(BF16), vc)
                 + _dot(en.astype(BF16), vn) + _dot(ex.astype(BF16), vx))
            outs.append(o / den)
        o_ref[0, :, j * LANES:(j + 1) * LANES] = jnp.where(low, outs[0], outs[1]).astype(BF16)


def _gqa(sink, q, k, v, *, n_lat, n_ctx_blocks, ctx_out, c_len):
    b = q.shape[0]
    nq = n_lat + (n_ctx_blocks if ctx_out else 0)
    ctx_idx = n_lat * Q_BLOCK // c_len
    kv_prev = pl.BlockSpec((1, Q_BLOCK, GQA_KV_WIDTH), lambda bb, i: (bb, jnp.maximum(i - 1, 0), 0))
    kv_cur = pl.BlockSpec((1, Q_BLOCK, GQA_KV_WIDTH), lambda bb, i: (bb, i, 0))
    kv_next = pl.BlockSpec((1, Q_BLOCK, GQA_KV_WIDTH), lambda bb, i: (bb, jnp.minimum(i + 1, n_lat - 1), 0))
    kv_ctx = pl.BlockSpec((1, c_len, GQA_KV_WIDTH), lambda bb, i: (bb, ctx_idx, 0))
    return pl.pallas_call(
        functools.partial(_gqa_kernel, n_lat=n_lat),
        grid=(b, nq),
        in_specs=[pl.BlockSpec(memory_space=pltpu.SMEM),
                  pl.BlockSpec((1, Q_BLOCK, GQA_WIDTH), lambda bb, i: (bb, i, 0)),
                  kv_prev, kv_cur, kv_next, kv_ctx, kv_prev, kv_cur, kv_next, kv_ctx],
        out_specs=pl.BlockSpec((1, Q_BLOCK, GQA_WIDTH), lambda bb, i: (bb, i, 0)),
        out_shape=jax.ShapeDtypeStruct((b, nq * Q_BLOCK, GQA_WIDTH), BF16),
        compiler_params=_params(2),
        name="window_gqa",
    )(sink, q, k, k, k, k, v, v, v, v)


def _na_kernel(q_ref, *refs):
    k_refs, v_refs = refs[:NA_ROWS], refs[NA_ROWS:2 * NA_ROWS]
    kx_ref, vx_ref, bias_ref, o_ref = refs[2 * NA_ROWS:]
    lane = lax.broadcasted_iota(jnp.int32, (GRID_W, LANES), 1)
    low = lane < HEAD_DIM
    for j in range(NA_WIDTH // LANES):
        sl = slice(j * LANES, (j + 1) * LANES)
        qp = q_ref[0, :, sl]
        kb = jnp.concatenate([r[0, :, sl] for r in k_refs], axis=0)
        vb = jnp.concatenate([r[0, :, sl] for r in v_refs], axis=0)
        kx, vx = kx_ref[0, :, sl], vx_ref[0, :, sl]
        outs = []
        for half in range(2):
            qm = jnp.where(low if half == 0 else jnp.logical_not(low), qp, jnp.zeros_like(qp))
            s_loc = _dot_t(qm, kb) + bias_ref[0, 2 * j + half]
            s_ctx = _dot_t(qm, kx)
            m = jnp.maximum(jnp.max(s_loc, axis=1, keepdims=True), jnp.max(s_ctx, axis=1, keepdims=True))
            e_loc, e_ctx = jnp.exp(s_loc - m), jnp.exp(s_ctx - m)
            den = jnp.sum(e_loc, axis=1, keepdims=True) + jnp.sum(e_ctx, axis=1, keepdims=True)
            o = _dot(e_loc.astype(BF16), vb) + _dot(e_ctx.astype(BF16), vx)
            outs.append(o / den)
        o_ref[0, :, sl] = jnp.where(low, outs[0], outs[1]).astype(BF16)


def _na(q, k, v, bias, *, rows, n_ctx_rows, ctx_out, c_len):
    b = q.shape[0]
    nr = rows + (n_ctx_rows if ctx_out else 0)
    ctx_idx = rows * GRID_W // c_len

    def band_start(r):
        return jnp.clip(r - NA_ROWS // 2, 0, rows - NA_ROWS)

    def band_spec(i):
        return pl.BlockSpec((1, GRID_W, NA_WIDTH), lambda bb, r: (bb, band_start(r) + i, 0))

    def variant(r):
        return jnp.where(r < rows, r - band_start(r), NA_ROWS)

    ctx_spec = pl.BlockSpec((1, c_len, NA_WIDTH), lambda bb, r: (bb, ctx_idx, 0))
    return pl.pallas_call(
        _na_kernel,
        grid=(b, nr),
        in_specs=[pl.BlockSpec((1, GRID_W, NA_WIDTH), lambda bb, r: (bb, r, 0))]
                 + [band_spec(i) for i in range(NA_ROWS)] * 2
                 + [ctx_spec, ctx_spec,
                    pl.BlockSpec((1, NA_HEADS, GRID_W, NA_ROWS * GRID_W), lambda bb, r: (variant(r), 0, 0, 0))],
        out_specs=pl.BlockSpec((1, GRID_W, NA_WIDTH), lambda bb, r: (bb, r, 0)),
        out_shape=jax.ShapeDtypeStruct((b, nr * GRID_W, NA_WIDTH), BF16),
        compiler_params=_params(2),
        name="neighborhood_attn",
    )(q, *([k] * NA_ROWS), *([v] * NA_ROWS), k, v, bias)


def _na_bias_table(rpb):
    cols = np.arange(GRID_W)
    col_start = np.clip(cols - NA_COLS // 2, 0, GRID_W - NA_COLS)
    kc = np.arange(GRID_W)
    in_win = (kc[None, :] >= col_start[:, None]) & (kc[None, :] < col_start[:, None] + NA_COLS)
    col_idx = np.clip(kc[None, :] - cols[:, None] + (NA_COLS - 1), 0, 2 * NA_COLS - 2)
    v = np.arange(NA_ROWS)
    row_idx = (NA_ROWS - 1) - v[:, None] + np.arange(NA_ROWS)[None, :]
    tab = rpb.astype(F32)[:, row_idx][:, :, :, col_idx]
    tab = jnp.where(in_win[None, None, None], tab, NEG)
    tab = tab.transpose(1, 0, 3, 2, 4).reshape(NA_ROWS, NA_HEADS, GRID_W, NA_ROWS * GRID_W)
    return jnp.concatenate([tab, jnp.full((1,) + tab.shape[1:], NEG, F32)], axis=0)


def _cpow(are, aim, dt, e):
    mag = jnp.exp(are * dt * e)
    ang = aim * dt * e
    return mag * jnp.cos(ang), mag * jnp.sin(ang)


def _zoh_coef(are, aim, dt):
    lbr, lbi = _cpow(are, aim, dt, 1.0)
    nr, ni = lbr - 1.0, lbi
    den = are * are + aim * aim
    return (nr * are + ni * aim) / den, (ni * are - nr * aim) / den


def _cmul(ar, ai, br, bi):
    return ar * br - ai * bi, ar * bi + ai * br


def _s5_prep_kernel(arow_ref, acol_ref, ldt_ref, btr_ref, bti_ref, ctr_ref, cti_ref, crr_ref, cri_ref,
                    b2r_ref, b2i_ref, ew_ref, er_ref, ek_ref,
                    k_ref, wre_ref, wim_ref, ra_ref, rb_ref, lt_ref):
    dt = jnp.exp(ldt_ref[0, 0])
    are_r, aim_r = arow_ref[0, 0, 0:1, :], arow_ref[0, 0, 1:2, :]
    are_c, aim_c = acol_ref[0, 0, :, 0:1], acol_ref[0, 0, :, 1:2]
    cfr_r, cfi_r = _zoh_coef(are_r, aim_r, dt)
    cfr_c, cfi_c = _zoh_coef(are_c, aim_c, dt)

    xr, xi = _cpow(are_r, aim_r, dt, ew_ref[0])
    bbr, bbi = _cmul(cfr_r, cfi_r, btr_ref[0, 0], bti_ref[0, 0])
    wr, wi = _cmul(xr, xi, bbr, bbi)
    wre_ref[0, 0] = wr
    wim_ref[0, 0] = wi

    xr, xi = _cpow(are_c, aim_c, dt, er_ref[0])
    rr, ri = _cmul(ctr_ref[0, 0], cti_ref[0, 0], xr, xi)
    ra_ref[0, 0] = rr
    rb_ref[0, 0] = -ri

    xr, xi = _cpow(are_r, aim_r, dt, ek_ref[...])
    bbr, bbi = _cmul(cfr_c, cfi_c, b2r_ref[0, 0], b2i_ref[0, 0])
    cbr, cbi = _cmul(crr_ref[0, 0], cri_ref[0, 0], bbr, bbi)
    hi = lax.Precision.HIGHEST
    k_ref[0, 0] = (jnp.dot(xr, cbr, precision=hi, preferred_element_type=F32)
                   - jnp.dot(xi, cbi, precision=hi, preferred_element_type=F32))

    lr, li = _cpow(are_r, aim_r, dt, float(SSM_CHUNK))
    lt_ref[0, 0, 0:1, :] = lr
    lt_ref[0, 0, 1:2, :] = li


def _s5_prep(a_re, a_im, log_dt, b_re, b_im, c_re, c_im):
    nd, g, p = a_re.shape
    i = SSM_GROUP
    t = SSM_CHUNK
    ti = t * i
    arow = jnp.stack([a_re, a_im], axis=2)
    acol = jnp.stack([a_re, a_im], axis=3)
    ldt = log_dt.reshape(nd, g, 1, 1)
    bt = lambda z: jnp.tile(z.transpose(0, 1, 3, 2), (1, 1, t, 1))
    ct = lambda z: jnp.tile(z.transpose(0, 1, 3, 2), (1, 1, 1, t))
    cr = lambda z: jnp.repeat(z.transpose(0, 1, 3, 2), i, axis=3)
    b2 = lambda z: jnp.tile(z, (1, 1, 1, i))
    tok = np.arange(ti) // i
    ew = np.stack([t - 1 - tok, tok]).astype(np.float32).reshape(nd, ti, 1)
    er = np.stack([tok + 1, t - tok]).astype(np.float32).reshape(nd, 1, ti)
    ek = np.arange(t, dtype=np.float32).reshape(t, 1)

    def spec(*tail):
        return pl.BlockSpec((1, 1) + tail, lambda d, gg: (d, gg) + (0,) * len(tail))

    return pl.pallas_call(
        _s5_prep_kernel,
        grid=(nd, g),
        in_specs=[spec(2, p), spec(p, 2), spec(1, 1), spec(ti, p), spec(ti, p), spec(p, ti), spec(p, ti),
                  spec(p, i * i), spec(p, i * i), spec(p, i * i), spec(p, i * i),
                  pl.BlockSpec((1, ti, 1), lambda d, gg: (d, 0, 0)),
                  pl.BlockSpec((1, 1, ti), lambda d, gg: (d, 0, 0)),
                  pl.BlockSpec((t, 1), lambda d, gg: (0, 0))],
        out_specs=[spec(t, i * i), spec(ti, p), spec(ti, p), spec(p, ti), spec(p, ti), spec(2, p)],
        out_shape=[jax.ShapeDtypeStruct((nd, g) + s, F32)
                   for s in ((t, i * i), (ti, p), (ti, p), (p, ti), (p, ti), (2, p))],
        compiler_params=_params(2),
        name="s5_prep",
    )(arow, acol, ldt, bt(b_re), bt(b_im), ct(c_re), ct(c_im), cr(c_re), cr(c_im), b2(b_re), b2(b_im),
      jnp.asarray(ew), jnp.asarray(er), jnp.asarray(ek))


def _s5_toeplitz(taps):
    _, g, t, _ = taps.shape
    i = SSM_GROUP
    kf = taps[0].reshape(g, t, i, i).transpose(0, 2, 3, 1)
    kb = taps[1].reshape(g, t, i, i).transpose(0, 2, 3, 1)
    w = jnp.concatenate([jnp.flip(kb[..., 1:], axis=-1), kf[..., :1] + kb[..., :1], kf[..., 1:],
                         jnp.zeros_like(kf[..., :1])], axis=-1)
    tz = jnp.tile(w, (1, 1, 1, t))[..., :t * (2 * t - 1)].reshape(g, i, i, t, 2 * t - 1)[..., t - 1:]
    return tz.transpose(0, 3, 2, 4, 1).reshape(g, t * i, t * i)


def _s5_kernel(u_ref, m_ref, w_ref, r_ref, cst_ref, y_ref, e_ref, esw_ref, sin_ref, *, nb, n_lat_chunks, n_chunks):
    u = u_ref[0]
    e = _dot(u, w_ref[0])
    e_ref[...] = e
    for d in range(2):
        esw_ref[:, d * LANES:(d + 1) * LANES] = pltpu.roll(e[:, d * LANES:(d + 1) * LANES], SSM_STATE, 1)
    cst = cst_ref[0]
    lat = list(range(n_lat_chunks))
    ctx = list(range(n_lat_chunks, n_chunks))
    orders = (ctx + lat, ctx[::-1] + lat[::-1])
    for d in range(2):
        lo = d * LANES
        a, bm, bs = (jnp.broadcast_to(cst[3 * d + r:3 * d + r + 1, :], (nb, LANES)) for r in range(3))
        s = jnp.zeros((nb, LANES), F32)
        ssw = jnp.zeros((nb, LANES), F32)
        for c in orders[d]:
            rows = slice(c * nb, (c + 1) * nb)
            sin_ref[rows, lo:lo + LANES] = s
            s, ssw = (s * a + ssw * bm + e_ref[rows, lo:lo + LANES],
                      ssw * a + s * bs + esw_ref[rows, lo:lo + LANES])
    y_ref[0] = _dot(u, m_ref[0]) + _dot(sin_ref[...].astype(BF16), r_ref[0])


def _s5_scan(ug, m, w, r, cst, *, nb, n_lat_chunks):
    g, rows, ti = ug.shape
    n_chunks = rows // nb
    ns = w.shape[2]

    def spec(a, bb):
        return pl.BlockSpec((1, a, bb), lambda gg: (gg, 0, 0))

    return pl.pallas_call(
        functools.partial(_s5_kernel, nb=nb, n_lat_chunks=n_lat_chunks, n_chunks=n_chunks),
        grid=(g,),
        in_specs=[spec(rows, ti), spec(ti, ti), spec(ti, ns), spec(ns, ti), spec(8, LANES)],
        out_specs=spec(rows, ti),
        out_shape=jax.ShapeDtypeStruct((g, rows, ti), F32),
        scratch_shapes=[pltpu.VMEM((rows, ns), F32)] * 3,
        compiler_params=_params(1),
        name="s5_scan",
    )(ug, m, w, r, cst)


def _s5(u, a_re, a_im, log_dt, b_re, b_im, c_re, c_im, *, n_lat_chunks):
    b, s, _ = u.shape
    t, g, i = SSM_CHUNK, SSM_GROUPS, SSM_GROUP
    nc = s // t
    taps, wre, wim, ra, rb, lt = _s5_prep(a_re, a_im, log_dt, b_re, b_im, c_re, c_im)
    m = _s5_toeplitz(taps).astype(BF16)
    w = jnp.concatenate([wre[0], wim[0], wre[1], wim[1]], axis=-1).astype(BF16)
    r = jnp.concatenate([ra[0], rb[0], ra[1], rb[1]], axis=1).astype(BF16)
    lr, li = lt[:, :, 0, :], lt[:, :, 1, :]
    rows = []
    for d in range(2):
        rows += [jnp.concatenate([lr[d], lr[d]], -1), jnp.concatenate([-li[d], li[d]], -1),
                 jnp.concatenate([li[d], -li[d]], -1)]
    rows += [jnp.zeros_like(rows[0])] * 2
    cst = jnp.stack(rows, axis=1)
    ug = u.astype(BF16).reshape(b, nc, t, g, i).transpose(3, 1, 0, 2, 4).reshape(g, nc * b, t * i)
    yg = _s5_scan(ug, m, w, r, cst, nb=b, n_lat_chunks=n_lat_chunks)
    return yg.reshape(g, nc, b, t, i).transpose(2, 1, 3, 0, 4).reshape(b, s, g * i)


def _merge_kernel(x_ref, mod_ref, ys_ref, u_ref, ya_ref, yn_ref, gt_ref, d_ref,
                  wglu_ref, wps_ref, wpa_ref, wpn_ref, wo_ref, o_ref):
    d = x_ref.shape[2]
    y = ys_ref[0] + d_ref[...] * u_ref[0]
    t = 0.5 * y * (1.0 + jnp.tanh(math.sqrt(2.0 / math.pi) * (y + 0.044715 * (y * y * y))))
    glu = (t * jax.nn.sigmoid(_dot(t.astype(BF16), wglu_ref[...]))).astype(BF16)
    m = (jax.nn.sigmoid(gt_ref[0, :, 0:d]) * _dot(glu, wps_ref[...])
         + jax.nn.sigmoid(gt_ref[0, :, d:2 * d]) * _dot(ya_ref[0], wpa_ref[...])
         + jax.nn.sigmoid(gt_ref[0, :, 2 * d:3 * d]) * _dot(yn_ref[0], wpn_ref[...]))
    o_ref[0] = x_ref[0] + mod_ref[0, 5:6, :] * _dot(m.astype(BF16), wo_ref[...])


def _merge(xs, mod, ys, u, ya, yn, gates, ssm_d, wglu, wps, wpa, wpn, wo, *, n_tok, n_lat_tiles, ctx_row):
    b, _, d = xs.shape
    tm = TOKEN_TILE
    return pl.pallas_call(
        _merge_kernel,
        grid=(b, n_tok // tm),
        in_specs=[_tok_spec(d, tm), _mod_spec(d, n_lat_tiles, ctx_row), _tok_spec(SSM_WIDTH, tm),
                  _tok_spec(SSM_WIDTH, tm), _tok_spec(GQA_WIDTH, tm), _tok_spec(NA_WIDTH, tm),
                  _tok_spec(N_BRANCH * d, tm), _resident((1, SSM_WIDTH)), _resident(wglu.shape),
                  _resident(wps.shape), _resident(wpa.shape), _resident(wpn.shape), _resident(wo.shape)],
        out_specs=_tok_spec(d, tm),
        out_shape=jax.ShapeDtypeStruct((b, n_tok, d), F32),
        compiler_params=_params(2),
        name="mixer_merge",
    )(xs, mod, ys, u, ya, yn, gates, ssm_d.reshape(1, SSM_WIDTH), wglu, wps, wpa, wpn, wo)


def _rope_tables(l, c_len):
    t = jnp.arange(l)
    pos = jnp.stack([t // GRID_W, t % GRID_W], axis=-1).astype(F32)
    half = HEAD_DIM // 2
    inv = ROPE_THETA ** (-jnp.arange(0, half, 2, dtype=F32) / half)
    ang = pos[:, :, None] * inv
    cos, sin = jnp.cos(ang), jnp.sin(ang)
    zero = jnp.zeros_like(sin[:, 0])
    cos_h = jnp.concatenate([cos[:, 0], cos[:, 0], cos[:, 1], cos[:, 1]], axis=-1)
    sa_h = jnp.concatenate([zero, sin[:, 0], zero, sin[:, 1]], axis=-1)
    sb_h = jnp.concatenate([-sin[:, 0], zero, -sin[:, 1], zero], axis=-1)

    def full(tab, ctx_fill):
        tab = jnp.concatenate([tab, tab], axis=-1)
        return jnp.concatenate([tab, jnp.full((c_len, LANES), ctx_fill, F32)], axis=0)

    return full(cos_h, 1.0), full(sa_h, 0.0), full(sb_h, 0.0)


def kernel(x, c, ctx, c_ctx, w_ada, b_ada, norm_g, ffn1_wg, ffn1_wu, ffn1_wd, ffn2_wg, ffn2_wu, ffn2_wd, w_in,
           ssm_a_re, ssm_a_im, ssm_log_dt, ssm_b_re, ssm_b_im, ssm_c_re, ssm_c_im, ssm_d, ssm_w_glu, gqa_sink,
           na_rpb, w_p_ssm, w_p_gqa, w_p_na, w_out, final_g):
    b, l, d = x.shape
    c_len = ctx.shape[1]
    s = l + c_len
    depth = w_ada.shape[0]
    tm = TOKEN_TILE
    assert b + 1 <= MOD_ROWS and l % tm == 0 and c_len % tm == 0 and l % (GRID_W * NA_ROWS) == 0
    assert c_len % Q_BLOCK == 0 and l % c_len == 0 and b % 8 == 0
    n_lat_tiles = l // tm

    xs = jnp.concatenate([x, ctx], axis=1)
    cin = jnp.zeros((MOD_ROWS, d), F32).at[:b].set(c).at[b].set(c_ctx)
    mods = _ada(cin, w_ada, b_ada).reshape(depth, MOD_ROWS, N_MOD, d)
    cos_t, sa_t, sb_t = _rope_tables(l, c_len)

    perm = np.asarray(GQA_HEAD_PERM)
    q0 = SSM_WIDTH
    w_q = w_in[:, :, q0:q0 + GQA_WIDTH].reshape(depth, d, GQA_HEADS, HEAD_DIM)[:, :, perm]
    w_in_p = jnp.concatenate([w_in[:, :, :q0], w_q.reshape(depth, d, GQA_WIDTH), w_in[:, :, q0 + GQA_WIDTH:]],
                             axis=-1).astype(BF16)
    w_pa = w_p_gqa.reshape(depth, GQA_HEADS, HEAD_DIM, d)[:, perm].reshape(depth, GQA_WIDTH, d).astype(BF16)
    bf = lambda z: z.astype(BF16)
    f1 = (bf(ffn1_wg), bf(ffn1_wu), bf(ffn1_wd))
    f2 = (bf(ffn2_wg), bf(ffn2_wu), bf(ffn2_wd))
    w_glu, w_ps, w_pn, w_o = bf(ssm_w_glu), bf(w_p_ssm), bf(w_p_na), bf(w_out)

    tile_kw = dict(n_lat_tiles=n_lat_tiles, ctx_row=b)
    for li in range(depth):
        ctx_out = li < depth - 1
        n_tok = s if ctx_out else l
        mod, g = mods[li], norm_g[li]
        xs = _ffn(xs, mod, g, f1[0][li], f1[1][li], f1[2][li], k0=0, gi=0, n_tok=s, **tile_kw)
        u, q, k, v, nq, nk, nv, gates = _inproj(xs, mod, g, w_in_p[li], cos_t, sa_t, sb_t, **tile_kw)
        ys = _s5(u, ssm_a_re[li], ssm_a_im[li], ssm_log_dt[li], ssm_b_re[li], ssm_b_im[li],
                 ssm_c_re[li], ssm_c_im[li], n_lat_chunks=l // SSM_CHUNK)
        ya = _gqa(gqa_sink[li], q, k, v, n_lat=l // Q_BLOCK, n_ctx_blocks=c_len // Q_BLOCK, ctx_out=ctx_out,
                  c_len=c_len)
        yn = _na(nq, nk, nv, _na_bias_table(na_rpb[li]), rows=l // GRID_W, n_ctx_rows=c_len // GRID_W,
                 ctx_out=ctx_out, c_len=c_len)
        xs = _merge(xs, mod, ys, u, ya, yn, gates, ssm_d[li], w_glu[li], w_ps[li], w_pa[li], w_pn[li], w_o[li],
                    n_tok=n_tok, **tile_kw)
        xs = _ffn(xs, mod, g, f2[0][li], f2[1][li], f2[2][li], k0=6, gi=2, n_tok=n_tok,
                  final_g=None if ctx_out else final_g, **tile_kw)
    return xs
```

```python
import functools
import math

import jax
import jax.numpy as jnp
import numpy as np
from jax import lax
from jax.experimental import pallas as pl
from jax.experimental.pallas import tpu as pltpu

F32 = jnp.float32
BF16 = jnp.bfloat16

EPS = 1e-6
NEG = -1e30
N_MOD = 9
GRID_W = 64
HEAD_DIM = 64
GQA_HEADS = 8
GQA_KV_HEADS = 2
NA_HEADS = 8
NA_ROWS = 8
NA_COLS = 16
Q_BLOCK = 128
ROPE_THETA = 10000.0
SSM_GROUP = 16
SSM_GROUPS = 24
SSM_STATE = 64
SSM_WIDTH = SSM_GROUP * SSM_GROUPS
GQA_WIDTH = GQA_HEADS * HEAD_DIM
GQA_KV_WIDTH = GQA_KV_HEADS * HEAD_DIM
NA_WIDTH = NA_HEADS * HEAD_DIM
N_BRANCH = 3

LANES = 128
SUBLANES = 8
MOD_ROWS = 16
TOKEN_TILE = 256
FFN_CHUNK = 256
SSM_TILE = 64
SSM_BLOCK_GROUPS = LANES // SSM_GROUP
SSM_BLOCKS = SSM_GROUPS // SSM_BLOCK_GROUPS
SSM_BLOCK_STATE = SSM_BLOCK_GROUPS * SSM_STATE
VMEM_LIMIT = 56 * 1024 * 1024
GQA_HEAD_PERM = (0, 4, 1, 5, 2, 6, 3, 7)


def _params(n_axes):
    return pltpu.CompilerParams(dimension_semantics=("arbitrary",) * n_axes, vmem_limit_bytes=VMEM_LIMIT)


def _resident(shape):
    return pl.BlockSpec(shape, lambda *_: (0,) * len(shape), pipeline_mode=pl.Buffered(1))


def _dot(a, b):
    return jnp.dot(a, b, preferred_element_type=F32)


def _dot_t(a, b):
    return lax.dot_general(a, b, (((1,), (1,)), ((), ())), preferred_element_type=F32)


def _rmsnorm(x, g):
    return x * lax.rsqrt(jnp.mean(x * x, axis=-1, keepdims=True) + EPS) * g


def _mod_norm(x, g, shift, scale):
    return _rmsnorm(x, g) * (1.0 + scale) + shift


def _ada_kernel(c_ref, w_ref, b_ref, o_ref):
    s = c_ref[...]
    s = s * jax.nn.sigmoid(s)
    o_ref[0] = _dot(s.astype(BF16), w_ref[0].astype(BF16)) + b_ref[0]


def _ada(cin, w_ada, b_ada):
    depth, d, n = w_ada.shape
    tn = n // 4
    return pl.pallas_call(
        _ada_kernel,
        grid=(depth, n // tn),
        in_specs=[
            pl.BlockSpec((MOD_ROWS, d), lambda l, j: (0, 0)),
            pl.BlockSpec((1, d, tn), lambda l, j: (l, 0, j)),
            pl.BlockSpec((1, 1, tn), lambda l, j: (l, 0, j)),
        ],
        out_specs=pl.BlockSpec((1, MOD_ROWS, tn), lambda l, j: (l, 0, j)),
        out_shape=jax.ShapeDtypeStruct((depth, MOD_ROWS, n), F32),
        compiler_params=_params(2),
        name="ada_mod",
    )(cin, w_ada, b_ada.reshape(depth, 1, n))


def _tok_spec(width, tm):
    return pl.BlockSpec((1, tm, width), lambda b, j: (b, j, 0))


def _time_major_spec(width, tm):
    return pl.BlockSpec((tm, width), lambda b, j: (j, b))


def _mod_spec(d, n_lat_tiles, ctx_row):
    return pl.BlockSpec((1, N_MOD, d), lambda b, j: (jnp.where(j >= n_lat_tiles, ctx_row, b), 0, 0))


def _ffn_kernel(x_ref, mod_ref, g_ref, wg_ref, wu_ref, wd_ref, *rest, k0, gi, final):
    if final:
        fg_ref, o_ref, t_ref = rest
    else:
        o_ref, t_ref = rest
    x = x_ref[0]
    h = _mod_norm(x, g_ref[gi:gi + 1, :], mod_ref[0, k0:k0 + 1, :], mod_ref[0, k0 + 1:k0 + 2, :]).astype(BF16)
    f = wg_ref.shape[1]
    for c0 in range(0, f, FFN_CHUNK):
        a = _dot(h, wg_ref[:, c0:c0 + FFN_CHUNK])
        u = _dot(h, wu_ref[:, c0:c0 + FFN_CHUNK])
        t_ref[:, c0:c0 + FFN_CHUNK] = (a * jax.nn.sigmoid(a) * u).astype(BF16)
    y = _dot(t_ref[...], wd_ref[...])
    out = x + (0.5 * mod_ref[0, k0 + 2:k0 + 3, :]) * y
    if final:
        out = _rmsnorm(out, fg_ref[...])
    o_ref[0] = out


def _ffn(xs, mod, g, wg, wu, wd, *, k0, gi, n_tok, n_lat_tiles, ctx_row, final_g=None):
    b, _, d = xs.shape
    f = wg.shape[1]
    tm = TOKEN_TILE
    final = final_g is not None
    in_specs = [_tok_spec(d, tm), _mod_spec(d, n_lat_tiles, ctx_row), _resident(g.shape),
                _resident((d, f)), _resident((d, f)), _resident((f, d))]
    args = [xs, mod, g, wg, wu, wd]
    if final:
        in_specs.append(_resident((1, d)))
        args.append(final_g.reshape(1, d))
    return pl.pallas_call(
        functools.partial(_ffn_kernel, k0=k0, gi=gi, final=final),
        grid=(b, n_tok // tm),
        in_specs=in_specs,
        out_specs=_tok_spec(d, tm),
        out_shape=jax.ShapeDtypeStruct((b, n_tok, d), F32),
        scratch_shapes=[pltpu.VMEM((tm, f), BF16)],
        compiler_params=_params(2),
        name="ffn_half_final" if final else "ffn_half",
    )(*args)


def _inproj_kernel(x_ref, mod_ref, g_ref, w_ref, cos_ref, sa_ref, sb_ref,
                   u_ref, q_ref, k_ref, v_ref, nq_ref, nk_ref, nv_ref, gt_ref):
    x = x_ref[0]
    h = _mod_norm(x, g_ref[1:2, :], mod_ref[0, 3:4, :], mod_ref[0, 4:5, :]).astype(BF16)
    cos, sa, sb = cos_ref[...], sa_ref[...], sb_ref[...]
    scale = HEAD_DIM ** -0.5

    def mm(c0, n):
        return _dot(h, w_ref[:, c0:c0 + n])

    def rope(z):
        return z * cos + pltpu.roll(z, 16, 1) * sa + pltpu.roll(z, LANES - 16, 1) * sb

    c0 = 0
    u_ref[...] = mm(c0, SSM_WIDTH)
    c0 += SSM_WIDTH
    for j in range(GQA_WIDTH // LANES):
        q_ref[0, :, j * LANES:(j + 1) * LANES] = (rope(mm(c0, LANES)) * scale).astype(BF16)
        c0 += LANES
    k_ref[0] = rope(mm(c0, GQA_KV_WIDTH)).astype(BF16)
    c0 += GQA_KV_WIDTH
    v_ref[0] = mm(c0, GQA_KV_WIDTH).astype(BF16)
    c0 += GQA_KV_WIDTH
    nq_ref[0] = (mm(c0, NA_WIDTH) * scale).astype(BF16)
    c0 += NA_WIDTH
    nk_ref[0] = mm(c0, NA_WIDTH).astype(BF16)
    c0 += NA_WIDTH
    nv_ref[0] = mm(c0, NA_WIDTH).astype(BF16)
    c0 += NA_WIDTH
    n_gate = gt_ref.shape[2]
    for g0 in range(0, n_gate, 512):
        gt_ref[0, :, g0:g0 + 512] = mm(c0 + g0, 512)


def _inproj(xs, mod, g, w_in, cos_t, sa_t, sb_t, *, n_lat_tiles, ctx_row):
    b, s, d = xs.shape
    tm = TOKEN_TILE
    n_in = w_in.shape[1]
    n_gate = n_in - (SSM_WIDTH + GQA_WIDTH + 2 * GQA_KV_WIDTH + 3 * NA_WIDTH)
    widths = (GQA_WIDTH, GQA_KV_WIDTH, GQA_KV_WIDTH, NA_WIDTH, NA_WIDTH, NA_WIDTH, n_gate)
    dtypes = (BF16, BF16, BF16, BF16, BF16, BF16, F32)
    rope_spec = pl.BlockSpec((tm, LANES), lambda bb, j: (j, 0))
    return pl.pallas_call(
        _inproj_kernel,
        grid=(b, s // tm),
        in_specs=[_tok_spec(d, tm), _mod_spec(d, n_lat_tiles, ctx_row), _resident(g.shape),
                  _resident((d, n_in)), rope_spec, rope_spec, rope_spec],
        out_specs=[_time_major_spec(SSM_WIDTH, tm)] + [_tok_spec(w, tm) for w in widths],
        out_shape=[jax.ShapeDtypeStruct((s, b * SSM_WIDTH), F32)]
                  + [jax.ShapeDtypeStruct((b, s, w), dt) for w, dt in zip(widths, dtypes)],
        compiler_params=_params(2),
        name="mixer_in_proj",
    )(xs, mod, g, w_in, cos_t, sa_t, sb_t)


def _softmax_rows(s_ref, p_ref, extra=None):
    s = s_ref[...]
    m = jnp.max(s, axis=1, keepdims=True)
    if extra is not None:
        m = jnp.maximum(m, extra)
    e = jnp.exp(s - m)
    den = jnp.sum(e, axis=1, keepdims=True)
    if extra is not None:
        den = den + jnp.exp(extra - m)
    p_ref[...] = e.astype(BF16)
    return 1.0 / den


def _head_half(qp, low, half):
    return jnp.where(low if half == 0 else jnp.logical_not(low), qp, jnp.zeros_like(qp))


def _gqa_kernel(sink_ref, q_ref, kp_ref, kc_ref, kn_ref, kx_ref, vp_ref, vc_ref, vn_ref, vx_ref, o_ref,
                s_ref, p_ref, *, n_lat):
    i = pl.program_id(1)
    n_keys = s_ref.shape[1]
    far = 8 * n_keys
    row = lax.broadcasted_iota(jnp.int32, (Q_BLOCK, n_keys), 0)
    col = lax.broadcasted_iota(jnp.int32, (Q_BLOCK, n_keys), 1)
    is_lat = i < n_lat
    off_p = jnp.where(jnp.logical_and(is_lat, i > 0), 0, far)
    off_c = jnp.where(is_lat, 0, far)
    off_n = jnp.where(jnp.logical_and(is_lat, i < n_lat - 1), 0, far)
    ok = ((col >= row + off_p) & (col < Q_BLOCK)
          | (col >= Q_BLOCK + off_c) & (col < 2 * Q_BLOCK)
          | (col >= 2 * Q_BLOCK) & (col <= 2 * Q_BLOCK + row - off_n)
          | (col >= 3 * Q_BLOCK))
    low = lax.broadcasted_iota(jnp.int32, (Q_BLOCK, LANES), 1) < HEAD_DIM
    keys = jnp.concatenate([kp_ref[0], kc_ref[0], kn_ref[0], kx_ref[0]], axis=0)
    vals = jnp.concatenate([vp_ref[0], vc_ref[0], vn_ref[0], vx_ref[0]], axis=0)
    n_pairs = GQA_WIDTH // LANES
    for j in range(n_pairs):
        qp = q_ref[0, :, j * LANES:(j + 1) * LANES]
        for half in range(2):
            r0 = (2 * j + half) * Q_BLOCK
            s_ref[r0:r0 + Q_BLOCK, :] = jnp.where(ok, _dot_t(_head_half(qp, low, half), keys), NEG)
    sinks = jnp.concatenate([jnp.full((Q_BLOCK, 1), sink_ref[h], F32) for h in GQA_HEAD_PERM], axis=0)
    inv = _softmax_rows(s_ref, p_ref, sinks)
    for j in range(n_pairs):
        outs = []
        for half in range(2):
            r0 = (2 * j + half) * Q_BLOCK
            outs.append(_dot(p_ref[r0:r0 + Q_BLOCK, :], vals) * inv[r0:r0 + Q_BLOCK])
        o_ref[0, :, j * LANES:(j + 1) * LANES] = jnp.where(low, outs[0], outs[1]).astype(BF16)


def _gqa(sink, q, k, v, *, n_lat, n_ctx_blocks, ctx_out, c_len):
    b = q.shape[0]
    nq = n_lat + (n_ctx_blocks if ctx_out else 0)
    ctx_idx = n_lat * Q_BLOCK // c_len
    n_keys = 3 * Q_BLOCK + c_len
    kv_prev = pl.BlockSpec((1, Q_BLOCK, GQA_KV_WIDTH), lambda bb, i: (bb, jnp.maximum(i - 1, 0), 0))
    kv_cur = pl.BlockSpec((1, Q_BLOCK, GQA_KV_WIDTH), lambda bb, i: (bb, i, 0))
    kv_next = pl.BlockSpec((1, Q_BLOCK, GQA_KV_WIDTH), lambda bb, i: (bb, jnp.minimum(i + 1, n_lat - 1), 0))
    kv_ctx = pl.BlockSpec((1, c_len, GQA_KV_WIDTH), lambda bb, i: (bb, ctx_idx, 0))
    return pl.pallas_call(
        functools.partial(_gqa_kernel, n_lat=n_lat),
        grid=(b, nq),
        in_specs=[pl.BlockSpec(memory_space=pltpu.SMEM),
                  pl.BlockSpec((1, Q_BLOCK, GQA_WIDTH), lambda bb, i: (bb, i, 0)),
                  kv_prev, kv_cur, kv_next, kv_ctx, kv_prev, kv_cur, kv_next, kv_ctx],
        out_specs=pl.BlockSpec((1, Q_BLOCK, GQA_WIDTH), lambda bb, i: (bb, i, 0)),
        out_shape=jax.ShapeDtypeStruct((b, nq * Q_BLOCK, GQA_WIDTH), BF16),
        scratch_shapes=[pltpu.VMEM((GQA_HEADS * Q_BLOCK, n_keys), F32),
                        pltpu.VMEM((GQA_HEADS * Q_BLOCK, n_keys), BF16)],
        compiler_params=_params(2),
        name="window_gqa",
    )(sink, q, k, k, k, k, v, v, v, v)


def _na_kernel(q_ref, kb_ref, vb_ref, kx_ref, vx_ref, bias_ref, o_ref, s_ref, p_ref):
    n_loc = kb_ref.shape[1]
    low = lax.broadcasted_iota(jnp.int32, (GRID_W, LANES), 1) < HEAD_DIM
    n_pairs = NA_WIDTH // LANES
    for j in range(n_pairs):
        sl = slice(j * LANES, (j + 1) * LANES)
        qp = q_ref[0, :, sl]
        keys = jnp.concatenate([kb_ref[0, :, sl], kx_ref[0, :, sl]], axis=0)
        for half in range(2):
            r0 = (2 * j + half) * GRID_W
            s = _dot_t(_head_half(qp, low, half), keys)
            s_ref[r0:r0 + GRID_W, 0:n_loc] = s[:, 0:n_loc] + bias_ref[0, 2 * j + half]
            s_ref[r0:r0 + GRID_W, n_loc:] = s[:, n_loc:]
    inv = _softmax_rows(s_ref, p_ref)
    for j in range(n_pairs):
        sl = slice(j * LANES, (j + 1) * LANES)
        vals = jnp.concatenate([vb_ref[0, :, sl], vx_ref[0, :, sl]], axis=0)
        outs = []
        for half in range(2):
            r0 = (2 * j + half) * GRID_W
            outs.append(_dot(p_ref[r0:r0 + GRID_W, :], vals) * inv[r0:r0 + GRID_W])
        o_ref[0, :, sl] = jnp.where(low, outs[0], outs[1]).astype(BF16)


def _na(q, k, v, bias, *, rows, n_ctx_rows, ctx_out, c_len):
    b = q.shape[0]
    nr = rows + (n_ctx_rows if ctx_out else 0)
    ctx_idx = rows * GRID_W // c_len
    n_loc = NA_ROWS * GRID_W
    n_keys = n_loc + c_len

    def band_start(r):
        return jnp.clip(r - NA_ROWS // 2, 0, rows - NA_ROWS)

    def variant(r):
        return jnp.where(r < rows, r - band_start(r), NA_ROWS)

    band_spec = pl.BlockSpec((pl.Element(1), pl.Element(n_loc), pl.Element(NA_WIDTH)),
                             lambda bb, r: (bb, band_start(r) * GRID_W, 0))
    ctx_spec = pl.BlockSpec((1, c_len, NA_WIDTH), lambda bb, r: (bb, ctx_idx, 0))
    return pl.pallas_call(
        _na_kernel,
        grid=(b, nr),
        in_specs=[pl.BlockSpec((1, GRID_W, NA_WIDTH), lambda bb, r: (bb, r, 0)),
                  band_spec, band_spec, ctx_spec, ctx_spec,
                  pl.BlockSpec((1, NA_HEADS, GRID_W, n_loc), lambda bb, r: (variant(r), 0, 0, 0))],
        out_specs=pl.BlockSpec((1, GRID_W, NA_WIDTH), lambda bb, r: (bb, r, 0)),
        out_shape=jax.ShapeDtypeStruct((b, nr * GRID_W, NA_WIDTH), BF16),
        scratch_shapes=[pltpu.VMEM((NA_HEADS * GRID_W, n_keys), F32),
                        pltpu.VMEM((NA_HEADS * GRID_W, n_keys), BF16)],
        compiler_params=_params(2),
        name="neighborhood_attn",
    )(q, k, v, k, v, bias)


def _na_bias_table(rpb):
    cols = np.arange(GRID_W)
    col_start = np.clip(cols - NA_COLS // 2, 0, GRID_W - NA_COLS)
    kc = np.arange(GRID_W)
    in_win = (kc[None, :] >= col_start[:, None]) & (kc[None, :] < col_start[:, None] + NA_COLS)
    col_idx = np.clip(kc[None, :] - cols[:, None] + (NA_COLS - 1), 0, 2 * NA_COLS - 2)
    v = np.arange(NA_ROWS)
    row_idx = (NA_ROWS - 1) - v[:, None] + np.arange(NA_ROWS)[None, :]
    tab = rpb.astype(F32)[:, row_idx][:, :, :, col_idx]
    tab = jnp.where(in_win[None, None, None], tab, NEG)
    tab = tab.transpose(1, 0, 3, 2, 4).reshape(NA_ROWS, NA_HEADS, GRID_W, NA_ROWS * GRID_W)
    return jnp.concatenate([tab, jnp.full((1,) + tab.shape[1:], NEG, F32)], axis=0)


def _s5_prep_kernel(are_ref, aim_ref, ldt_ref, bre_ref, bim_ref, lbr_ref, lbi_ref, bbr_ref, bbi_ref):
    are, aim = are_ref[...], aim_ref[...]
    dt = jnp.exp(ldt_ref[...])
    mag = jnp.exp(are * dt)
    lbr, lbi = mag * jnp.cos(aim * dt), mag * jnp.sin(aim * dt)
    lbr_ref[...] = lbr
    lbi_ref[...] = lbi
    nr, ni = lbr - 1.0, lbi
    den = are * are + aim * aim
    cr, ci = (nr * are + ni * aim) / den, (ni * are - nr * aim) / den
    br, bi = bre_ref[...], bim_ref[...]
    bbr_ref[...] = cr * br - ci * bi
    bbi_ref[...] = cr * bi + ci * br


def _s5_prep(a_re, a_im, log_dt, b_re, b_im):
    nd, g, p = a_re.shape
    i = b_re.shape[-1]
    n = nd * g * p
    col = lambda z: z.reshape(n, 1)
    ldt = jnp.broadcast_to(log_dt[:, :, None], (nd, g, p))
    whole = lambda w: pl.BlockSpec((n, w), lambda: (0, 0))
    lbr, lbi, bbr, bbi = pl.pallas_call(
        _s5_prep_kernel,
        in_specs=[whole(1)] * 3 + [whole(i)] * 2,
        out_specs=[whole(1)] * 2 + [whole(i)] * 2,
        out_shape=[jax.ShapeDtypeStruct((n, 1), F32)] * 2 + [jax.ShapeDtypeStruct((n, i), F32)] * 2,
        name="s5_prep",
    )(col(a_re), col(a_im), col(ldt), b_re.reshape(n, i), b_im.reshape(n, i))
    return lbr.reshape(nd, g, p), lbi.reshape(nd, g, p), bbr.reshape(nd, g, p, i), bbi.reshape(nd, g, p, i)


def _s5_block_operands(lbr, lbi, bbr, bbi, c_re, c_im):
    nd, g, p = lbr.shape
    i = bbr.shape[-1]
    nb, gb = SSM_BLOCKS, SSM_BLOCK_GROUPS
    eye = jnp.eye(gb, dtype=F32)

    def in_map(z):
        z = z.reshape(nd, nb, gb, p, i).transpose(0, 1, 2, 4, 3)
        return (z[:, :, :, :, None, :] * eye[None, None, :, None, :, None]).reshape(nd, nb, gb * i, gb * p)

    def out_map(z):
        z = z.reshape(nd, nb, gb, i, p).transpose(0, 1, 2, 4, 3)
        return (z[:, :, :, :, None, :] * eye[None, None, :, None, :, None]).reshape(nd, nb, gb * p, gb * i)

    lam = jnp.stack([lbr.reshape(nd, nb, gb * p), lbi.reshape(nd, nb, gb * p)], axis=2)
    w_in = jnp.concatenate([in_map(bbr), in_map(bbi)], axis=-1).astype(BF16)
    w_out = jnp.concatenate([out_map(c_re.astype(F32)), -out_map(c_im.astype(F32))], axis=2).astype(BF16)
    return lam, w_in, w_out


def _s5_kernel(u_ref, lam_ref, win_ref, wout_ref, y_ref, st_ref, carry_ref, *, nb, tt):
    d = pl.program_id(0)
    ns = SSM_BLOCK_STATE

    @pl.when(pl.program_id(1) == 0)
    def _():
        carry_ref[...] = jnp.zeros_like(carry_ref)

    for j in range(SSM_BLOCKS):
        ub = u_ref[:, j * LANES:(j + 1) * LANES].astype(BF16)
        st_ref[j] = _dot(ub, win_ref[0, j])

    lam = lam_ref[0]
    lr = [jnp.broadcast_to(lam[j, 0:1, :], (nb, ns)) for j in range(SSM_BLOCKS)]
    li = [jnp.broadcast_to(lam[j, 1:2, :], (nb, ns)) for j in range(SSM_BLOCKS)]

    def step(t, carry):
        r0 = pl.multiple_of(jnp.where(d == 0, t, tt - 1 - t) * nb, nb)
        new = []
        for j in range(SSM_BLOCKS):
            sr, si = carry[2 * j], carry[2 * j + 1]
            nr = lr[j] * sr - li[j] * si + st_ref[j, pl.ds(r0, nb), 0:ns]
            ni = lr[j] * si + li[j] * sr + st_ref[j, pl.ds(r0, nb), ns:2 * ns]
            st_ref[j, pl.ds(r0, nb), 0:ns] = nr
            st_ref[j, pl.ds(r0, nb), ns:2 * ns] = ni
            new += [nr, ni]
        return tuple(new)

    init = tuple(carry_ref[k] for k in range(2 * SSM_BLOCKS))
    last = lax.fori_loop(0, tt, step, init, unroll=2)
    for k in range(2 * SSM_BLOCKS):
        carry_ref[k] = last[k]
    for j in range(SSM_BLOCKS):
        y_ref[0, :, j * LANES:(j + 1) * LANES] = _dot(st_ref[j].astype(BF16), wout_ref[0, j])


def _s5_scan(u2, lam, w_in, w_out, *, nb, n_lat_tiles, n_ctx_tiles):
    rows_total, width = u2.shape
    tt = SSM_TILE
    rows = tt * nb
    n_tiles = n_lat_tiles + n_ctx_tiles
    ns2 = 2 * SSM_BLOCK_STATE

    def tile(d, k):
        fwd = jnp.where(k < n_ctx_tiles, n_lat_tiles + k, k - n_ctx_tiles)
        bwd = jnp.where(k < n_ctx_tiles, n_tiles - 1 - k, n_lat_tiles - 1 - (k - n_ctx_tiles))
        return jnp.where(d == 0, fwd, bwd)

    def par(*tail):
        return pl.BlockSpec((1,) + tail, lambda d, k: (d,) + (0,) * len(tail))

    return pl.pallas_call(
        functools.partial(_s5_kernel, nb=nb, tt=tt),
        grid=(2, n_tiles),
        in_specs=[pl.BlockSpec((rows, width), lambda d, k: (tile(d, k), 0)),
                  par(SSM_BLOCKS, 2, SSM_BLOCK_STATE), par(SSM_BLOCKS, LANES, ns2), par(SSM_BLOCKS, ns2, LANES)],
        out_specs=pl.BlockSpec((1, rows, width), lambda d, k: (d, tile(d, k), 0)),
        out_shape=jax.ShapeDtypeStruct((2, rows_total, width), F32),
        scratch_shapes=[pltpu.VMEM((SSM_BLOCKS, rows, ns2), F32),
                        pltpu.VMEM((2 * SSM_BLOCKS, nb, SSM_BLOCK_STATE), F32)],
        compiler_params=_params(2),
        name="s5_scan",
    )(u2, lam, w_in, w_out)


def _merge_kernel(x_ref, mod_ref, ysf_ref, ysb_ref, u_ref, ya_ref, yn_ref, gt_ref, d_ref,
                  wglu_ref, wps_ref, wpa_ref, wpn_ref, wo_ref, o_ref):
    d = x_ref.shape[2]
    y = ysf_ref[0] + ysb_ref[0] + d_ref[...] * u_ref[...]
    t = 0.5 * y * (1.0 + jnp.tanh(math.sqrt(2.0 / math.pi) * (y + 0.044715 * (y * y * y))))
    glu = (t * jax.nn.sigmoid(_dot(t.astype(BF16), wglu_ref[...]))).astype(BF16)
    m = (jax.nn.sigmoid(gt_ref[0, :, 0:d]) * _dot(glu, wps_ref[...])
         + jax.nn.sigmoid(gt_ref[0, :, d:2 * d]) * _dot(ya_ref[0], wpa_ref[...])
         + jax.nn.sigmoid(gt_ref[0, :, 2 * d:3 * d]) * _dot(yn_ref[0], wpn_ref[...]))
    o_ref[0] = x_ref[0] + mod_ref[0, 5:6, :] * _dot(m.astype(BF16), wo_ref[...])


def _merge(xs, mod, ys, u2, ya, yn, gates, ssm_d, wglu, wps, wpa, wpn, wo, *, n_tok, n_lat_tiles, ctx_row):
    b, _, d = xs.shape
    tm = TOKEN_TILE
    ys_dir = lambda dd: pl.BlockSpec((1, tm, SSM_WIDTH), lambda bb, j: (dd, j, bb))
    return pl.pallas_call(
        _merge_kernel,
        grid=(b, n_tok // tm),
        in_specs=[_tok_spec(d, tm), _mod_spec(d, n_lat_tiles, ctx_row), ys_dir(0), ys_dir(1),
                  _time_major_spec(SSM_WIDTH, tm), _tok_spec(GQA_WIDTH, tm), _tok_spec(NA_WIDTH, tm),
                  _tok_spec(N_BRANCH * d, tm), _resident((1, SSM_WIDTH)), _resident(wglu.shape),
                  _resident(wps.shape), _resident(wpa.shape), _resident(wpn.shape), _resident(wo.shape)],
        out_specs=_tok_spec(d, tm),
        out_shape=jax.ShapeDtypeStruct((b, n_tok, d), F32),
        compiler_params=_params(2),
        name="mixer_merge",
    )(xs, mod, ys, ys, u2, ya, yn, gates, ssm_d.reshape(1, SSM_WIDTH), wglu, wps, wpa, wpn, wo)


def _rope_tables(l, c_len):
    t = jnp.arange(l)
    pos = jnp.stack([t // GRID_W, t % GRID_W], axis=-1).astype(F32)
    half = HEAD_DIM // 2
    inv = ROPE_THETA ** (-jnp.arange(0, half, 2, dtype=F32) / half)
    ang = pos[:, :, None] * inv
    cos, sin = jnp.cos(ang), jnp.sin(ang)
    zero = jnp.zeros_like(sin[:, 0])
    cos_h = jnp.concatenate([cos[:, 0], cos[:, 0], cos[:, 1], cos[:, 1]], axis=-1)
    sa_h = jnp.concatenate([zero, sin[:, 0], zero, sin[:, 1]], axis=-1)
    sb_h = jnp.concatenate([-sin[:, 0], zero, -sin[:, 1], zero], axis=-1)

    def full(tab, ctx_fill):
        tab = jnp.concatenate([tab, tab], axis=-1)
        return jnp.concatenate([tab, jnp.full((c_len, LANES), ctx_fill, F32)], axis=0)

    return full(cos_h, 1.0), full(sa_h, 0.0), full(sb_h, 0.0)


def kernel(x, c, ctx, c_ctx, w_ada, b_ada, norm_g, ffn1_wg, ffn1_wu, ffn1_wd, ffn2_wg, ffn2_wu, ffn2_wd, w_in,
           ssm_a_re, ssm_a_im, ssm_log_dt, ssm_b_re, ssm_b_im, ssm_c_re, ssm_c_im, ssm_d, ssm_w_glu, gqa_sink,
           na_rpb, w_p_ssm, w_p_gqa, w_p_na, w_out, final_g):
    b, l, d = x.shape
    c_len = ctx.shape[1]
    s = l + c_len
    depth = w_ada.shape[0]
    tm = TOKEN_TILE
    assert b + 1 <= MOD_ROWS and l % tm == 0 and c_len % tm == 0 and l % (GRID_W * NA_ROWS) == 0
    assert c_len % Q_BLOCK == 0 and l % c_len == 0 and b % SUBLANES == 0
    assert l % SSM_TILE == 0 and c_len % SSM_TILE == 0
    n_lat_tiles = l // tm

    xs = jnp.concatenate([x, ctx], axis=1)
    cin = jnp.zeros((MOD_ROWS, d), F32).at[:b].set(c).at[b].set(c_ctx)
    mods = _ada(cin, w_ada, b_ada).reshape(depth, MOD_ROWS, N_MOD, d)
    cos_t, sa_t, sb_t = _rope_tables(l, c_len)

    perm = np.asarray(GQA_HEAD_PERM)
    q0 = SSM_WIDTH
    w_q = w_in[:, :, q0:q0 + GQA_WIDTH].reshape(depth, d, GQA_HEADS, HEAD_DIM)[:, :, perm]
    w_in_p = jnp.concatenate([w_in[:, :, :q0], w_q.reshape(depth, d, GQA_WIDTH), w_in[:, :, q0 + GQA_WIDTH:]],
                             axis=-1).astype(BF16)
    w_pa = w_p_gqa.reshape(depth, GQA_HEADS, HEAD_DIM, d)[:, perm].reshape(depth, GQA_WIDTH, d).astype(BF16)
    bf = lambda z: z.astype(BF16)
    f1 = (bf(ffn1_wg), bf(ffn1_wu), bf(ffn1_wd))
    f2 = (bf(ffn2_wg), bf(ffn2_wu), bf(ffn2_wd))
    w_glu, w_ps, w_pn, w_o = bf(ssm_w_glu), bf(w_p_ssm), bf(w_p_na), bf(w_out)

    tile_kw = dict(n_lat_tiles=n_lat_tiles, ctx_row=b)
    for li in range(depth):
        ctx_out = li < depth - 1
        n_tok = s if ctx_out else l
        mod, g = mods[li], norm_g[li]
        xs = _ffn(xs, mod, g, f1[0][li], f1[1][li], f1[2][li], k0=0, gi=0, n_tok=s, **tile_kw)
        u2, q, k, v, nq, nk, nv, gates = _inproj(xs, mod, g, w_in_p[li], cos_t, sa_t, sb_t, **tile_kw)
        lam, s_in, s_out = _s5_block_operands(
            *_s5_prep(ssm_a_re[li], ssm_a_im[li], ssm_log_dt[li], ssm_b_re[li], ssm_b_im[li]),
            ssm_c_re[li], ssm_c_im[li])
        ys = _s5_scan(u2.reshape(s * b, SSM_WIDTH), lam, s_in, s_out, nb=b,
                      n_lat_tiles=l // SSM_TILE, n_ctx_tiles=c_len // SSM_TILE)
        ys = ys.reshape(2, s, b * SSM_WIDTH)
        ya = _gqa(gqa_sink[li], q, k, v, n_lat=l // Q_BLOCK, n_ctx_blocks=c_len // Q_BLOCK, ctx_out=ctx_out,
                  c_len=c_len)
        yn = _na(nq, nk, nv, _na_bias_table(na_rpb[li]), rows=l // GRID_W, n_ctx_rows=c_len // GRID_W,
                 ctx_out=ctx_out, c_len=c_len)
        xs = _merge(xs, mod, ys, u2, ya, yn, gates, ssm_d[li], w_glu[li], w_ps[li], w_pa[li], w_pn[li], w_o[li],
                    n_tok=n_tok, **tile_kw)
        xs = _ffn(xs, mod, g, f2[0][li], f2[1][li], f2[2][li], k0=6, gi=2, n_tok=n_tok,
                  final_g=None if ctx_out else final_g, **tile_kw)
    return xs
```

```python
import functools
import math

import jax
import jax.numpy as jnp
import numpy as np
from jax import lax
from jax.experimental import pallas as pl
from jax.experimental.pallas import tpu as pltpu

F32 = jnp.float32
BF16 = jnp.bfloat16

EPS = 1e-6
NEG = -1e30
N_MOD = 9
GRID_W = 64
HEAD_DIM = 64
GQA_HEADS = 8
GQA_KV_HEADS = 2
NA_HEADS = 8
NA_ROWS = 8
NA_COLS = 16
Q_BLOCK = 128
ROPE_THETA = 10000.0
SSM_GROUP = 16
SSM_GROUPS = 24
SSM_STATE = 64
SSM_WIDTH = SSM_GROUP * SSM_GROUPS
GQA_WIDTH = GQA_HEADS * HEAD_DIM
GQA_KV_WIDTH = GQA_KV_HEADS * HEAD_DIM
NA_WIDTH = NA_HEADS * HEAD_DIM
N_BRANCH = 3

LANES = 128
SUBLANES = 8
MOD_ROWS = 16
TOKEN_TILE = 256
FFN_CHUNK = 256
NA_QROWS = 4
NA_UNION = NA_QROWS + NA_ROWS - 1
SSM_TILE = 64
SSM_BLOCK_GROUPS = LANES // SSM_GROUP
SSM_BLOCKS = SSM_GROUPS // SSM_BLOCK_GROUPS
SSM_BLOCK_STATE = SSM_BLOCK_GROUPS * SSM_STATE
VMEM_LIMIT = 56 * 1024 * 1024
GQA_HEAD_PERM = (0, 4, 1, 5, 2, 6, 3, 7)


def _params(n_axes):
    return pltpu.CompilerParams(dimension_semantics=("arbitrary",) * n_axes, vmem_limit_bytes=VMEM_LIMIT)


def _resident(shape):
    return pl.BlockSpec(shape, lambda *_: (0,) * len(shape), pipeline_mode=pl.Buffered(1))


def _dot(a, b):
    return jnp.dot(a, b, preferred_element_type=F32)


def _dot_t(a, b):
    return lax.dot_general(a, b, (((1,), (1,)), ((), ())), preferred_element_type=F32)


def _rmsnorm(x, g):
    return x * lax.rsqrt(jnp.mean(x * x, axis=-1, keepdims=True) + EPS) * g


def _mod_norm(x, g, shift, scale):
    return _rmsnorm(x, g) * (1.0 + scale) + shift


def _ada_kernel(c_ref, w_ref, b_ref, o_ref):
    s = c_ref[...]
    s = s * jax.nn.sigmoid(s)
    o_ref[0] = _dot(s.astype(BF16), w_ref[0].astype(BF16)) + b_ref[0]


def _ada(cin, w_ada, b_ada):
    depth, d, n = w_ada.shape
    tn = n // 4
    return pl.pallas_call(
        _ada_kernel,
        grid=(depth, n // tn),
        in_specs=[
            pl.BlockSpec((MOD_ROWS, d), lambda l, j: (0, 0)),
            pl.BlockSpec((1, d, tn), lambda l, j: (l, 0, j)),
            pl.BlockSpec((1, 1, tn), lambda l, j: (l, 0, j)),
        ],
        out_specs=pl.BlockSpec((1, MOD_ROWS, tn), lambda l, j: (l, 0, j)),
        out_shape=jax.ShapeDtypeStruct((depth, MOD_ROWS, n), F32),
        compiler_params=_params(2),
        name="ada_mod",
    )(cin, w_ada, b_ada.reshape(depth, 1, n))


def _tok_spec(width, tm):
    return pl.BlockSpec((1, tm, width), lambda j, b: (b, j, 0))


def _slab_spec(tm, nb, lead=()):
    n = len(lead)
    return pl.BlockSpec((1,) * n + (SSM_BLOCKS, tm * nb, LANES), lambda j, b: lead + (0, j, 0))


def _mod_spec(d, n_lat_tiles, ctx_row):
    return pl.BlockSpec((1, N_MOD, d), lambda j, b: (jnp.where(j >= n_lat_tiles, ctx_row, b), 0, 0))


def _ffn_kernel(*refs, k0, gi, final, n_lat_tiles, split_in):
    refs = list(refs)
    x_ref = refs.pop(0)
    ctx_ref = refs.pop(0) if split_in else None
    mod_ref, g_ref, wg_ref, wu_ref, wd_ref = refs[:5]
    fg_ref = refs[5] if final else None
    o_ref, t_ref = refs[-2:]
    x = x_ref[0]
    if split_in:
        x = jnp.where(pl.program_id(0) >= n_lat_tiles, ctx_ref[0], x)
    h = _mod_norm(x, g_ref[gi:gi + 1, :], mod_ref[0, k0:k0 + 1, :], mod_ref[0, k0 + 1:k0 + 2, :]).astype(BF16)
    f = wg_ref.shape[1]
    for c0 in range(0, f, FFN_CHUNK):
        a = _dot(h, wg_ref[:, c0:c0 + FFN_CHUNK])
        u = _dot(h, wu_ref[:, c0:c0 + FFN_CHUNK])
        t_ref[:, c0:c0 + FFN_CHUNK] = (a * jax.nn.sigmoid(a) * u).astype(BF16)
    y = _dot(t_ref[...], wd_ref[...])
    out = x + (0.5 * mod_ref[0, k0 + 2:k0 + 3, :]) * y
    if final:
        out = _rmsnorm(out, fg_ref[...])
    o_ref[0] = out


def _ffn(xs, mod, g, wg, wu, wd, *, k0, gi, n_tok, n_lat_tiles, ctx_row, final_g=None, ctx=None):
    b, _, d = xs.shape
    f = wg.shape[1]
    tm = TOKEN_TILE
    final = final_g is not None
    split_in = ctx is not None
    if split_in:
        in_specs = [pl.BlockSpec((1, tm, d), lambda j, bb: (bb, jnp.minimum(j, n_lat_tiles - 1), 0)),
                    pl.BlockSpec((1, tm, d), lambda j, bb: (bb, jnp.maximum(j - n_lat_tiles, 0), 0))]
        args = [xs, ctx]
    else:
        in_specs = [_tok_spec(d, tm)]
        args = [xs]
    in_specs += [_mod_spec(d, n_lat_tiles, ctx_row), _resident(g.shape),
                 _resident((d, f)), _resident((d, f)), _resident((f, d))]
    args += [mod, g, wg, wu, wd]
    if final:
        in_specs.append(_resident((1, d)))
        args.append(final_g.reshape(1, d))
    return pl.pallas_call(
        functools.partial(_ffn_kernel, k0=k0, gi=gi, final=final, n_lat_tiles=n_lat_tiles, split_in=split_in),
        grid=(n_tok // tm, b),
        in_specs=in_specs,
        out_specs=_tok_spec(d, tm),
        out_shape=jax.ShapeDtypeStruct((b, n_tok, d), F32),
        scratch_shapes=[pltpu.VMEM((tm, f), BF16)],
        compiler_params=_params(2),
        name="ffn_half_final" if final else "ffn_half",
    )(*args)


def _inproj_kernel(x_ref, mod_ref, g_ref, w_ref, cos_ref, sa_ref, sb_ref,
                   u_ref, q_ref, k_ref, v_ref, nq_ref, nk_ref, nv_ref, gt_ref):
    x = x_ref[0]
    h = _mod_norm(x, g_ref[1:2, :], mod_ref[0, 3:4, :], mod_ref[0, 4:5, :]).astype(BF16)
    cos, sa, sb = cos_ref[...], sa_ref[...], sb_ref[...]
    scale = HEAD_DIM ** -0.5

    def mm(c0, n):
        return _dot(h, w_ref[:, c0:c0 + n])

    def rope(z):
        return z * cos + pltpu.roll(z, 16, 1) * sa + pltpu.roll(z, LANES - 16, 1) * sb

    c0 = 0
    tm, nb = x.shape[0], u_ref.shape[1] // x.shape[0]
    for blk in range(SSM_BLOCKS):
        u_ref[blk, pl.ds(pl.program_id(1), tm, stride=nb), :] = mm(c0, LANES)
        c0 += LANES
    for j in range(GQA_WIDTH // LANES):
        q_ref[0, :, j * LANES:(j + 1) * LANES] = (rope(mm(c0, LANES)) * scale).astype(BF16)
        c0 += LANES
    k_ref[0] = rope(mm(c0, GQA_KV_WIDTH)).astype(BF16)
    c0 += GQA_KV_WIDTH
    v_ref[0] = mm(c0, GQA_KV_WIDTH).astype(BF16)
    c0 += GQA_KV_WIDTH
    nq_ref[0] = (mm(c0, NA_WIDTH) * scale).astype(BF16)
    c0 += NA_WIDTH
    nk_ref[0] = mm(c0, NA_WIDTH).astype(BF16)
    c0 += NA_WIDTH
    nv_ref[0] = mm(c0, NA_WIDTH).astype(BF16)
    c0 += NA_WIDTH
    n_gate = gt_ref.shape[2]
    for g0 in range(0, n_gate, 512):
        gt_ref[0, :, g0:g0 + 512] = mm(c0 + g0, 512)


def _inproj(xs, mod, g, w_in, cos_t, sa_t, sb_t, *, n_lat_tiles, ctx_row):
    b, s, d = xs.shape
    tm = TOKEN_TILE
    n_in = w_in.shape[1]
    n_gate = n_in - (SSM_WIDTH + GQA_WIDTH + 2 * GQA_KV_WIDTH + 3 * NA_WIDTH)
    widths = (GQA_WIDTH, GQA_KV_WIDTH, GQA_KV_WIDTH, NA_WIDTH, NA_WIDTH, NA_WIDTH, n_gate)
    dtypes = (BF16, BF16, BF16, BF16, BF16, BF16, F32)
    rope_spec = pl.BlockSpec((tm, LANES), lambda j, bb: (j, 0))
    return pl.pallas_call(
        _inproj_kernel,
        grid=(s // tm, b),
        in_specs=[_tok_spec(d, tm), _mod_spec(d, n_lat_tiles, ctx_row), _resident(g.shape),
                  _resident((d, n_in)), rope_spec, rope_spec, rope_spec],
        out_specs=[_slab_spec(tm, b)] + [_tok_spec(w, tm) for w in widths],
        out_shape=[jax.ShapeDtypeStruct((SSM_BLOCKS, s * b, LANES), F32)]
                  + [jax.ShapeDtypeStruct((b, s, w), dt) for w, dt in zip(widths, dtypes)],
        compiler_params=_params(2),
        name="mixer_in_proj",
    )(xs, mod, g, w_in, cos_t, sa_t, sb_t)


def _softmax_rows(s_ref, p_ref, extra=None):
    s = s_ref[...]
    m = jnp.max(s, axis=1, keepdims=True)
    if extra is not None:
        m = jnp.maximum(m, extra)
    e = jnp.exp(s - m)
    den = jnp.sum(e, axis=1, keepdims=True)
    if extra is not None:
        den = den + jnp.exp(extra - m)
    p_ref[...] = e.astype(BF16)
    return 1.0 / den


def _head_half(qp, low, half):
    return jnp.where(low if half == 0 else jnp.logical_not(low), qp, jnp.zeros_like(qp))


def _gqa_kernel(sink_ref, q_ref, kp_ref, kc_ref, kn_ref, kx_ref, vp_ref, vc_ref, vn_ref, vx_ref, o_ref,
                s_ref, p_ref, *, n_lat):
    i = pl.program_id(1)
    n_keys = s_ref.shape[1]
    far = 8 * n_keys
    row = lax.broadcasted_iota(jnp.int32, (Q_BLOCK, n_keys), 0)
    col = lax.broadcasted_iota(jnp.int32, (Q_BLOCK, n_keys), 1)
    is_lat = i < n_lat
    off_p = jnp.where(jnp.logical_and(is_lat, i > 0), 0, far)
    off_c = jnp.where(is_lat, 0, far)
    off_n = jnp.where(jnp.logical_and(is_lat, i < n_lat - 1), 0, far)
    ok = ((col >= row + off_p) & (col < Q_BLOCK)
          | (col >= Q_BLOCK + off_c) & (col < 2 * Q_BLOCK)
          | (col >= 2 * Q_BLOCK) & (col <= 2 * Q_BLOCK + row - off_n)
          | (col >= 3 * Q_BLOCK))
    low = lax.broadcasted_iota(jnp.int32, (Q_BLOCK, LANES), 1) < HEAD_DIM
    keys = jnp.concatenate([kp_ref[0], kc_ref[0], kn_ref[0], kx_ref[0]], axis=0)
    vals = jnp.concatenate([vp_ref[0], vc_ref[0], vn_ref[0], vx_ref[0]], axis=0)
    n_pairs = GQA_WIDTH // LANES
    for j in range(n_pairs):
        qp = q_ref[0, :, j * LANES:(j + 1) * LANES]
        for half in range(2):
            r0 = (2 * j + half) * Q_BLOCK
            s_ref[r0:r0 + Q_BLOCK, :] = jnp.where(ok, _dot_t(_head_half(qp, low, half), keys), NEG)
    sinks = jnp.concatenate([jnp.full((Q_BLOCK, 1), sink_ref[h], F32) for h in GQA_HEAD_PERM], axis=0)
    inv = _softmax_rows(s_ref, p_ref, sinks)
    for j in range(n_pairs):
        outs = []
        for half in range(2):
            r0 = (2 * j + half) * Q_BLOCK
            outs.append(_dot(p_ref[r0:r0 + Q_BLOCK, :], vals) * inv[r0:r0 + Q_BLOCK])
        o_ref[0, :, j * LANES:(j + 1) * LANES] = jnp.where(low, outs[0], outs[1]).astype(BF16)


def _gqa(sink, q, k, v, *, n_lat, n_ctx_blocks, ctx_out, c_len):
    b = q.shape[0]
    nq = n_lat + (n_ctx_blocks if ctx_out else 0)
    ctx_idx = n_lat * Q_BLOCK // c_len
    n_keys = 3 * Q_BLOCK + c_len
    kv_prev = pl.BlockSpec((1, Q_BLOCK, GQA_KV_WIDTH), lambda bb, i: (bb, jnp.maximum(i - 1, 0), 0))
    kv_cur = pl.BlockSpec((1, Q_BLOCK, GQA_KV_WIDTH), lambda bb, i: (bb, i, 0))
    kv_next = pl.BlockSpec((1, Q_BLOCK, GQA_KV_WIDTH), lambda bb, i: (bb, jnp.minimum(i + 1, n_lat - 1), 0))
    kv_ctx = pl.BlockSpec((1, c_len, GQA_KV_WIDTH), lambda bb, i: (bb, ctx_idx, 0))
    return pl.pallas_call(
        functools.partial(_gqa_kernel, n_lat=n_lat),
        grid=(b, nq),
        in_specs=[pl.BlockSpec(memory_space=pltpu.SMEM),
                  pl.BlockSpec((1, Q_BLOCK, GQA_WIDTH), lambda bb, i: (bb, i, 0)),
                  kv_prev, kv_cur, kv_next, kv_ctx, kv_prev, kv_cur, kv_next, kv_ctx],
        out_specs=pl.BlockSpec((1, Q_BLOCK, GQA_WIDTH), lambda bb, i: (bb, i, 0)),
        out_shape=jax.ShapeDtypeStruct((b, nq * Q_BLOCK, GQA_WIDTH), BF16),
        scratch_shapes=[pltpu.VMEM((GQA_HEADS * Q_BLOCK, n_keys), F32),
                        pltpu.VMEM((GQA_HEADS * Q_BLOCK, n_keys), BF16)],
        compiler_params=_params(2),
        name="window_gqa",
    )(sink, q, k, k, k, k, v, v, v, v)


def _na_band_start(r, rows):
    return jnp.clip(r - NA_ROWS // 2, 0, rows - NA_ROWS)


def _na_union_start(g, rows):
    return jnp.minimum(_na_band_start(g * NA_QROWS, rows), rows - NA_UNION)


def _na_kernel(q_ref, kb_ref, vb_ref, kx_ref, vx_ref, bias_ref, o_ref, s_ref, p_ref, *, rows):
    g = pl.program_id(1)
    n_loc = NA_ROWS * GRID_W
    low = lax.broadcasted_iota(jnp.int32, (GRID_W, LANES), 1) < HEAD_DIM
    n_pairs = NA_WIDTH // LANES
    u0 = _na_union_start(g, rows)
    for rho in range(NA_QROWS):
        r = g * NA_QROWS + rho
        is_ctx = r >= rows
        band = _na_band_start(r, rows)
        off = pl.multiple_of(jnp.where(is_ctx, 0, band - u0) * GRID_W, GRID_W)
        dr0 = jnp.where(is_ctx, 0, (NA_ROWS - 1) - (r - band))
        pen = jnp.where(is_ctx, NEG, 0.0).astype(F32)
        qrows = slice(rho * GRID_W, (rho + 1) * GRID_W)
        for j in range(n_pairs):
            sl = slice(j * LANES, (j + 1) * LANES)
            qp = q_ref[0, qrows, sl]
            keys = jnp.concatenate([kb_ref[0, pl.ds(off, n_loc), sl], kx_ref[0, :, sl]], axis=0)
            for half in range(2):
                h = 2 * j + half
                r0 = h * GRID_W
                s = _dot_t(_head_half(qp, low, half), keys)
                bias = jnp.concatenate([bias_ref[h, dr0 + i] for i in range(0, NA_ROWS, 2)], axis=1)
                s_ref[rho, r0:r0 + GRID_W, 0:n_loc] = s[:, 0:n_loc] + (bias + pen)
                s_ref[rho, r0:r0 + GRID_W, n_loc:] = s[:, n_loc:]
        inv = _softmax_rows(s_ref.at[rho], p_ref.at[rho])
        for j in range(n_pairs):
            sl = slice(j * LANES, (j + 1) * LANES)
            vals = jnp.concatenate([vb_ref[0, pl.ds(off, n_loc), sl], vx_ref[0, :, sl]], axis=0)
            outs = []
            for half in range(2):
                r0 = (2 * j + half) * GRID_W
                outs.append(_dot(p_ref[rho, r0:r0 + GRID_W, :], vals) * inv[r0:r0 + GRID_W])
            o_ref[0, qrows, sl] = jnp.where(low, outs[0], outs[1]).astype(BF16)


def _na(q, k, v, bias, *, rows, n_ctx_rows, ctx_out, c_len):
    b = q.shape[0]
    assert rows % NA_QROWS == 0 and n_ctx_rows % NA_QROWS == 0 and rows >= NA_UNION
    ng = (rows + (n_ctx_rows if ctx_out else 0)) // NA_QROWS
    qt = NA_QROWS * GRID_W
    ctx_idx = rows * GRID_W // c_len
    n_keys = NA_ROWS * GRID_W + c_len
    union_spec = pl.BlockSpec((pl.Element(1), pl.Element(NA_UNION * GRID_W), pl.Element(NA_WIDTH)),
                              lambda bb, g: (bb, _na_union_start(g, rows) * GRID_W, 0))
    ctx_spec = pl.BlockSpec((1, c_len, NA_WIDTH), lambda bb, g: (bb, ctx_idx, 0))
    return pl.pallas_call(
        functools.partial(_na_kernel, rows=rows),
        grid=(b, ng),
        in_specs=[pl.BlockSpec((1, qt, NA_WIDTH), lambda bb, g: (bb, g, 0)),
                  union_spec, union_spec, ctx_spec, ctx_spec, _resident(bias.shape)],
        out_specs=pl.BlockSpec((1, qt, NA_WIDTH), lambda bb, g: (bb, g, 0)),
        out_shape=jax.ShapeDtypeStruct((b, ng * qt, NA_WIDTH), BF16),
        scratch_shapes=[pltpu.VMEM((NA_QROWS, NA_HEADS * GRID_W, n_keys), F32),
                        pltpu.VMEM((NA_QROWS, NA_HEADS * GRID_W, n_keys), BF16)],
        compiler_params=_params(2),
        name="neighborhood_attn",
    )(q, k, v, k, v, bias)


def _na_bias_pairs(rpb):
    cols = np.arange(GRID_W)
    col_start = np.clip(cols - NA_COLS // 2, 0, GRID_W - NA_COLS)
    kc = np.arange(GRID_W)
    in_win = (kc[None, :] >= col_start[:, None]) & (kc[None, :] < col_start[:, None] + NA_COLS)
    col_idx = np.clip(kc[None, :] - cols[:, None] + (NA_COLS - 1), 0, 2 * NA_COLS - 2)
    tab = jnp.where(in_win[None, None], rpb.astype(F32)[:, :, col_idx], NEG)
    return jnp.concatenate([tab[:, :-1], tab[:, 1:]], axis=-1)


def _s5_prep_kernel(are_ref, aim_ref, ldt_ref, bre_ref, bim_ref, lbr_ref, lbi_ref, bbr_ref, bbi_ref):
    are, aim = are_ref[...], aim_ref[...]
    dt = jnp.exp(ldt_ref[...])
    mag = jnp.exp(are * dt)
    lbr, lbi = mag * jnp.cos(aim * dt), mag * jnp.sin(aim * dt)
    lbr_ref[...] = lbr
    lbi_ref[...] = lbi
    nr, ni = lbr - 1.0, lbi
    den = are * are + aim * aim
    cr, ci = (nr * are + ni * aim) / den, (ni * are - nr * aim) / den
    br, bi = bre_ref[...], bim_ref[...]
    bbr_ref[...] = cr * br - ci * bi
    bbi_ref[...] = cr * bi + ci * br


def _s5_prep(a_re, a_im, log_dt, b_re, b_im):
    nd, g, p = a_re.shape
    i = b_re.shape[-1]
    n = nd * g * p
    col = lambda z: z.reshape(n, 1)
    ldt = jnp.broadcast_to(log_dt[:, :, None], (nd, g, p))
    whole = lambda w: pl.BlockSpec((n, w), lambda: (0, 0))
    lbr, lbi, bbr, bbi = pl.pallas_call(
        _s5_prep_kernel,
        in_specs=[whole(1)] * 3 + [whole(i)] * 2,
        out_specs=[whole(1)] * 2 + [whole(i)] * 2,
        out_shape=[jax.ShapeDtypeStruct((n, 1), F32)] * 2 + [jax.ShapeDtypeStruct((n, i), F32)] * 2,
        name="s5_prep",
    )(col(a_re), col(a_im), col(ldt), b_re.reshape(n, i), b_im.reshape(n, i))
    return lbr.reshape(nd, g, p), lbi.reshape(nd, g, p), bbr.reshape(nd, g, p, i), bbi.reshape(nd, g, p, i)


def _s5_block_operands(lbr, lbi, bbr, bbi, c_re, c_im):
    nd, g, p = lbr.shape
    i = bbr.shape[-1]
    nb, gb = SSM_BLOCKS, SSM_BLOCK_GROUPS
    eye = jnp.eye(gb, dtype=F32)

    def in_map(z):
        z = z.reshape(nd, nb, gb, p, i).transpose(0, 1, 2, 4, 3)
        return (z[:, :, :, :, None, :] * eye[None, None, :, None, :, None]).reshape(nd, nb, gb * i, gb * p)

    def out_map(z):
        z = z.reshape(nd, nb, gb, i, p).transpose(0, 1, 2, 4, 3)
        return (z[:, :, :, :, None, :] * eye[None, None, :, None, :, None]).reshape(nd, nb, gb * p, gb * i)

    lam = jnp.stack([lbr.reshape(nd, nb, gb * p), lbi.reshape(nd, nb, gb * p)], axis=2)
    w_in = jnp.concatenate([in_map(bbr), in_map(bbi)], axis=-1).astype(BF16)
    w_out = jnp.concatenate([out_map(c_re.astype(F32)), -out_map(c_im.astype(F32))], axis=2).astype(BF16)
    return lam, w_in, w_out


def _s5_kernel(u_ref, lam_ref, win_ref, wout_ref, y_ref, st_ref, carry_ref, *, nb, tt):
    d = pl.program_id(0)
    ns = SSM_BLOCK_STATE

    @pl.when(pl.program_id(1) == 0)
    def _():
        carry_ref[...] = jnp.zeros_like(carry_ref)

    for j in range(SSM_BLOCKS):
        st_ref[j] = _dot(u_ref[j].astype(BF16), win_ref[0, j])

    lam = lam_ref[0]
    lr = [jnp.broadcast_to(lam[j, 0:1, :], (nb, ns)) for j in range(SSM_BLOCKS)]
    li = [jnp.broadcast_to(lam[j, 1:2, :], (nb, ns)) for j in range(SSM_BLOCKS)]

    def step(t, carry):
        r0 = pl.multiple_of(jnp.where(d == 0, t, tt - 1 - t) * nb, nb)
        new = []
        for j in range(SSM_BLOCKS):
            sr, si = carry[2 * j], carry[2 * j + 1]
            nr = lr[j] * sr - li[j] * si + st_ref[j, pl.ds(r0, nb), 0:ns]
            ni = lr[j] * si + li[j] * sr + st_ref[j, pl.ds(r0, nb), ns:2 * ns]
            st_ref[j, pl.ds(r0, nb), 0:ns] = nr
            st_ref[j, pl.ds(r0, nb), ns:2 * ns] = ni
            new += [nr, ni]
        return tuple(new)

    init = tuple(carry_ref[k] for k in range(2 * SSM_BLOCKS))
    last = lax.fori_loop(0, tt, step, init, unroll=2)
    for k in range(2 * SSM_BLOCKS):
        carry_ref[k] = last[k]
    for j in range(SSM_BLOCKS):
        y_ref[0, j] = _dot(st_ref[j].astype(BF16), wout_ref[0, j])


def _s5_scan(u2, lam, w_in, w_out, *, nb, n_lat_tiles, n_ctx_tiles):
    _, rows_total, _ = u2.shape
    tt = SSM_TILE
    rows = tt * nb
    n_tiles = n_lat_tiles + n_ctx_tiles
    ns2 = 2 * SSM_BLOCK_STATE

    def tile(d, k):
        fwd = jnp.where(k < n_ctx_tiles, n_lat_tiles + k, k - n_ctx_tiles)
        bwd = jnp.where(k < n_ctx_tiles, n_tiles - 1 - k, n_lat_tiles - 1 - (k - n_ctx_tiles))
        return jnp.where(d == 0, fwd, bwd)

    def par(*tail):
        return pl.BlockSpec((1,) + tail, lambda d, k: (d,) + (0,) * len(tail))

    return pl.pallas_call(
        functools.partial(_s5_kernel, nb=nb, tt=tt),
        grid=(2, n_tiles),
        in_specs=[pl.BlockSpec((SSM_BLOCKS, rows, LANES), lambda d, k: (0, tile(d, k), 0)),
                  par(SSM_BLOCKS, 2, SSM_BLOCK_STATE), par(SSM_BLOCKS, LANES, ns2), par(SSM_BLOCKS, ns2, LANES)],
        out_specs=pl.BlockSpec((1, SSM_BLOCKS, rows, LANES), lambda d, k: (d, 0, tile(d, k), 0)),
        out_shape=jax.ShapeDtypeStruct((2, SSM_BLOCKS, rows_total, LANES), F32),
        scratch_shapes=[pltpu.VMEM((SSM_BLOCKS, rows, ns2), F32),
                        pltpu.VMEM((2 * SSM_BLOCKS, nb, SSM_BLOCK_STATE), F32)],
        compiler_params=_params(2),
        name="s5_scan",
    )(u2, lam, w_in, w_out)


def _merge_kernel(x_ref, mod_ref, ysf_ref, ysb_ref, u_ref, ya_ref, yn_ref, gt_ref, d_ref,
                  wglu_ref, wps_ref, wpa_ref, wpn_ref, wo_ref, o_ref):
    d = x_ref.shape[2]
    tm = x_ref.shape[1]
    mine = pl.ds(pl.program_id(1), tm, stride=u_ref.shape[1] // tm)
    y = jnp.concatenate([ysf_ref[0, blk, mine, :] + ysb_ref[0, blk, mine, :] for blk in range(SSM_BLOCKS)], axis=1)
    y = y + d_ref[...] * jnp.concatenate([u_ref[blk, mine, :] for blk in range(SSM_BLOCKS)], axis=1)
    t = 0.5 * y * (1.0 + jnp.tanh(math.sqrt(2.0 / math.pi) * (y + 0.044715 * (y * y * y))))
    glu = (t * jax.nn.sigmoid(_dot(t.astype(BF16), wglu_ref[...]))).astype(BF16)
    m = (jax.nn.sigmoid(gt_ref[0, :, 0:d]) * _dot(glu, wps_ref[...])
         + jax.nn.sigmoid(gt_ref[0, :, d:2 * d]) * _dot(ya_ref[0], wpa_ref[...])
         + jax.nn.sigmoid(gt_ref[0, :, 2 * d:3 * d]) * _dot(yn_ref[0], wpn_ref[...]))
    o_ref[0] = x_ref[0] + mod_ref[0, 5:6, :] * _dot(m.astype(BF16), wo_ref[...])


def _merge(xs, mod, ys, u2, ya, yn, gates, ssm_d, wglu, wps, wpa, wpn, wo, *, n_tok, n_lat_tiles, ctx_row):
    b, _, d = xs.shape
    tm = TOKEN_TILE
    return pl.pallas_call(
        _merge_kernel,
        grid=(n_tok // tm, b),
        in_specs=[_tok_spec(d, tm), _mod_spec(d, n_lat_tiles, ctx_row), _slab_spec(tm, b, (0,)),
                  _slab_spec(tm, b, (1,)), _slab_spec(tm, b), _tok_spec(GQA_WIDTH, tm), _tok_spec(NA_WIDTH, tm),
                  _tok_spec(N_BRANCH * d, tm), _resident((1, SSM_WIDTH)), _resident(wglu.shape),
                  _resident(wps.shape), _resident(wpa.shape), _resident(wpn.shape), _resident(wo.shape)],
        out_specs=_tok_spec(d, tm),
        out_shape=jax.ShapeDtypeStruct((b, n_tok, d), F32),
        compiler_params=_params(2),
        name="mixer_merge",
    )(xs, mod, ys, ys, u2, ya, yn, gates, ssm_d.reshape(1, SSM_WIDTH), wglu, wps, wpa, wpn, wo)


def _rope_tables(l, c_len):
    t = jnp.arange(l)
    pos = jnp.stack([t // GRID_W, t % GRID_W], axis=-1).astype(F32)
    half = HEAD_DIM // 2
    inv = ROPE_THETA ** (-jnp.arange(0, half, 2, dtype=F32) / half)
    ang = pos[:, :, None] * inv
    cos, sin = jnp.cos(ang), jnp.sin(ang)
    zero = jnp.zeros_like(sin[:, 0])
    cos_h = jnp.concatenate([cos[:, 0], cos[:, 0], cos[:, 1], cos[:, 1]], axis=-1)
    sa_h = jnp.concatenate([zero, sin[:, 0], zero, sin[:, 1]], axis=-1)
    sb_h = jnp.concatenate([-sin[:, 0], zero, -sin[:, 1], zero], axis=-1)

    def full(tab, ctx_fill):
        tab = jnp.concatenate([tab, tab], axis=-1)
        return jnp.concatenate([tab, jnp.full((c_len, LANES), ctx_fill, F32)], axis=0)

    return full(cos_h, 1.0), full(sa_h, 0.0), full(sb_h, 0.0)


def kernel(x, c, ctx, c_ctx, w_ada, b_ada, norm_g, ffn1_wg, ffn1_wu, ffn1_wd, ffn2_wg, ffn2_wu, ffn2_wd, w_in,
           ssm_a_re, ssm_a_im, ssm_log_dt, ssm_b_re, ssm_b_im, ssm_c_re, ssm_c_im, ssm_d, ssm_w_glu, gqa_sink,
           na_rpb, w_p_ssm, w_p_gqa, w_p_na, w_out, final_g):
    b, l, d = x.shape
    c_len = ctx.shape[1]
    s = l + c_len
    depth = w_ada.shape[0]
    tm = TOKEN_TILE
    assert b + 1 <= MOD_ROWS and l % tm == 0 and c_len % tm == 0 and l % (GRID_W * NA_ROWS) == 0
    assert c_len % Q_BLOCK == 0 and l % c_len == 0 and b % SUBLANES == 0
    assert l % SSM_TILE == 0 and c_len % SSM_TILE == 0
    n_lat_tiles = l // tm

    cin = jnp.zeros((MOD_ROWS, d), F32).at[:b].set(c).at[b].set(c_ctx)
    mods = _ada(cin, w_ada, b_ada).reshape(depth, MOD_ROWS, N_MOD, d)
    cos_t, sa_t, sb_t = _rope_tables(l, c_len)

    perm = np.asarray(GQA_HEAD_PERM)
    q0 = SSM_WIDTH
    w_q = w_in[:, :, q0:q0 + GQA_WIDTH].reshape(depth, d, GQA_HEADS, HEAD_DIM)[:, :, perm]
    w_in_p = jnp.concatenate([w_in[:, :, :q0], w_q.reshape(depth, d, GQA_WIDTH), w_in[:, :, q0 + GQA_WIDTH:]],
                             axis=-1).astype(BF16)
    w_pa = w_p_gqa.reshape(depth, GQA_HEADS, HEAD_DIM, d)[:, perm].reshape(depth, GQA_WIDTH, d).astype(BF16)
    bf = lambda z: z.astype(BF16)
    f1 = (bf(ffn1_wg), bf(ffn1_wu), bf(ffn1_wd))
    f2 = (bf(ffn2_wg), bf(ffn2_wu), bf(ffn2_wd))
    w_glu, w_ps, w_pn, w_o = bf(ssm_w_glu), bf(w_p_ssm), bf(w_p_na), bf(w_out)

    tile_kw = dict(n_lat_tiles=n_lat_tiles, ctx_row=b)
    xs = x
    for li in range(depth):
        ctx_out = li < depth - 1
        n_tok = s if ctx_out else l
        mod, g = mods[li], norm_g[li]
        xs = _ffn(xs, mod, g, f1[0][li], f1[1][li], f1[2][li], k0=0, gi=0, n_tok=s,
                  ctx=ctx if li == 0 else None, **tile_kw)
        u2, q, k, v, nq, nk, nv, gates = _inproj(xs, mod, g, w_in_p[li], cos_t, sa_t, sb_t, **tile_kw)
        lam, s_in, s_out = _s5_block_operands(
            *_s5_prep(ssm_a_re[li], ssm_a_im[li], ssm_log_dt[li], ssm_b_re[li], ssm_b_im[li]),
            ssm_c_re[li], ssm_c_im[li])
        ys = _s5_scan(u2, lam, s_in, s_out, nb=b, n_lat_tiles=l // SSM_TILE, n_ctx_tiles=c_len // SSM_TILE)
        ya = _gqa(gqa_sink[li], q, k, v, n_lat=l // Q_BLOCK, n_ctx_blocks=c_len // Q_BLOCK, ctx_out=ctx_out,
                  c_len=c_len)
        yn = _na(nq, nk, nv, _na_bias_pairs(na_rpb[li]), rows=l // GRID_W, n_ctx_rows=c_len // GRID_W,
                 ctx_out=ctx_out, c_len=c_len)
        xs = _merge(xs, mod, ys, u2, ya, yn, gates, ssm_d[li], w_glu[li], w_ps[li], w_pa[li], w_pn[li], w_o[li],
                    n_tok=n_tok, **tile_kw)
        xs = _ffn(xs, mod, g, f2[0][li], f2[1][li], f2[2][li], k0=6, gi=2, n_tok=n_tok,
                  final_g=None if ctx_out else final_g, **tile_kw)
    return xs
```

```python
import functools
import math

import jax
import jax.numpy as jnp
import numpy as np
from jax import lax
from jax.experimental import pallas as pl
from jax.experimental.pallas import tpu as pltpu

F32 = jnp.float32
BF16 = jnp.bfloat16

EPS = 1e-6
NEG = -1e30
N_MOD = 9
GRID_W = 64
HEAD_DIM = 64
GQA_HEADS = 8
GQA_KV_HEADS = 2
NA_HEADS = 8
NA_ROWS = 8
NA_COLS = 16
Q_BLOCK = 128
ROPE_THETA = 10000.0
SSM_GROUP = 16
SSM_GROUPS = 24
SSM_STATE = 64
SSM_WIDTH = SSM_GROUP * SSM_GROUPS
GQA_WIDTH = GQA_HEADS * HEAD_DIM
GQA_KV_WIDTH = GQA_KV_HEADS * HEAD_DIM
NA_WIDTH = NA_HEADS * HEAD_DIM
N_BRANCH = 3
MIXER_COLS = SSM_WIDTH + GQA_WIDTH + 2 * GQA_KV_WIDTH + 3 * NA_WIDTH

LANES = 128
SUBLANES = 8
MOD_ROWS = 16
TOKEN_TILE = 256
FFN_CHUNK = 256
GQA_QBLOCKS = 2
NA_QROWS = 4
NA_UNION = NA_QROWS + NA_ROWS - 1
SSM_TILE = 64
SSM_BLOCK_GROUPS = LANES // SSM_GROUP
SSM_BLOCKS = SSM_GROUPS // SSM_BLOCK_GROUPS
SSM_BLOCK_STATE = SSM_BLOCK_GROUPS * SSM_STATE
VMEM_LIMIT = 56 * 1024 * 1024
GQA_HEAD_PERM = (0, 4, 1, 5, 2, 6, 3, 7)


def _params(n_axes):
    return pltpu.CompilerParams(dimension_semantics=("arbitrary",) * n_axes, vmem_limit_bytes=VMEM_LIMIT)


def _resident(shape):
    return pl.BlockSpec(shape, lambda *_: (0,) * len(shape), pipeline_mode=pl.Buffered(1))


def _dot(a, b):
    return jnp.dot(a, b, preferred_element_type=F32)


def _dot_t(a, b):
    return lax.dot_general(a, b, (((1,), (1,)), ((), ())), preferred_element_type=F32)


def _rmsnorm(x, g):
    return x * lax.rsqrt(jnp.mean(x * x, axis=-1, keepdims=True) + EPS) * g


def _mod_norm(x, g, shift, scale):
    return _rmsnorm(x, g) * (1.0 + scale) + shift


def _ada_kernel(c_ref, w_ref, b_ref, o_ref):
    s = c_ref[...]
    s = s * jax.nn.sigmoid(s)
    o_ref[0] = _dot(s.astype(BF16), w_ref[0].astype(BF16)) + b_ref[0]


def _ada(cin, w_ada, b_ada):
    depth, d, n = w_ada.shape
    tn = n // 4
    return pl.pallas_call(
        _ada_kernel,
        grid=(depth, n // tn),
        in_specs=[
            pl.BlockSpec((MOD_ROWS, d), lambda l, j: (0, 0)),
            pl.BlockSpec((1, d, tn), lambda l, j: (l, 0, j)),
            pl.BlockSpec((1, 1, tn), lambda l, j: (l, 0, j)),
        ],
        out_specs=pl.BlockSpec((1, MOD_ROWS, tn), lambda l, j: (l, 0, j)),
        out_shape=jax.ShapeDtypeStruct((depth, MOD_ROWS, n), F32),
        compiler_params=_params(2),
        name="ada_mod",
    )(cin, w_ada, b_ada.reshape(depth, 1, n))


def _tok_spec(width, tm):
    return pl.BlockSpec((1, tm, width), lambda j, b: (b, j, 0))


def _slab_spec(tm, nb, lead=()):
    n = len(lead)
    return pl.BlockSpec((1,) * n + (SSM_BLOCKS, tm * nb, LANES), lambda j, b: lead + (0, j, 0))


def _mod_spec(d, n_lat_tiles, ctx_row):
    return pl.BlockSpec((1, N_MOD, d), lambda j, b: (jnp.where(j >= n_lat_tiles, ctx_row, b), 0, 0))


def _ffn_kernel(*refs, k0, gi, final, n_lat_tiles, split_in):
    refs = list(refs)
    x_ref = refs.pop(0)
    ctx_ref = refs.pop(0) if split_in else None
    mod_ref, g_ref, wg_ref, wu_ref, wd_ref = refs[:5]
    fg_ref = refs[5] if final else None
    o_ref, t_ref = refs[-2:]
    x = x_ref[0]
    if split_in:
        x = jnp.where(pl.program_id(0) >= n_lat_tiles, ctx_ref[0], x)
    h = _mod_norm(x, g_ref[gi:gi + 1, :], mod_ref[0, k0:k0 + 1, :], mod_ref[0, k0 + 1:k0 + 2, :]).astype(BF16)
    f = wg_ref.shape[1]
    for c0 in range(0, f, FFN_CHUNK):
        a = _dot(h, wg_ref[:, c0:c0 + FFN_CHUNK])
        u = _dot(h, wu_ref[:, c0:c0 + FFN_CHUNK])
        t_ref[:, c0:c0 + FFN_CHUNK] = (a * jax.nn.sigmoid(a) * u).astype(BF16)
    y = _dot(t_ref[...], wd_ref[...])
    out = x + (0.5 * mod_ref[0, k0 + 2:k0 + 3, :]) * y
    if final:
        out = _rmsnorm(out, fg_ref[...])
    o_ref[0] = out


def _ffn(xs, mod, g, wg, wu, wd, *, k0, gi, n_tok, n_lat_tiles, ctx_row, final_g=None, ctx=None):
    b, _, d = xs.shape
    f = wg.shape[1]
    tm = TOKEN_TILE
    final = final_g is not None
    split_in = ctx is not None
    if split_in:
        in_specs = [pl.BlockSpec((1, tm, d), lambda j, bb: (bb, jnp.minimum(j, n_lat_tiles - 1), 0)),
                    pl.BlockSpec((1, tm, d), lambda j, bb: (bb, jnp.maximum(j - n_lat_tiles, 0), 0))]
        args = [xs, ctx]
    else:
        in_specs = [_tok_spec(d, tm)]
        args = [xs]
    in_specs += [_mod_spec(d, n_lat_tiles, ctx_row), _resident(g.shape),
                 _resident((d, f)), _resident((d, f)), _resident((f, d))]
    args += [mod, g, wg, wu, wd]
    if final:
        in_specs.append(_resident((1, d)))
        args.append(final_g.reshape(1, d))
    return pl.pallas_call(
        functools.partial(_ffn_kernel, k0=k0, gi=gi, final=final, n_lat_tiles=n_lat_tiles, split_in=split_in),
        grid=(n_tok // tm, b),
        in_specs=in_specs,
        out_specs=_tok_spec(d, tm),
        out_shape=jax.ShapeDtypeStruct((b, n_tok, d), F32),
        scratch_shapes=[pltpu.VMEM((tm, f), BF16)],
        compiler_params=_params(2),
        name="ffn_half_final" if final else "ffn_half",
    )(*args)


def _inproj_kernel(x_ref, mod_ref, g_ref, w_ref, cos_ref, sa_ref, sb_ref,
                   u_ref, q_ref, k_ref, v_ref, nq_ref, nk_ref, nv_ref):
    x = x_ref[0]
    h = _mod_norm(x, g_ref[1:2, :], mod_ref[0, 3:4, :], mod_ref[0, 4:5, :]).astype(BF16)
    cos, sa, sb = cos_ref[...], sa_ref[...], sb_ref[...]
    scale = HEAD_DIM ** -0.5

    def mm(c0, n):
        return _dot(h, w_ref[:, c0:c0 + n])

    def rope(z):
        return z * cos + pltpu.roll(z, 16, 1) * sa + pltpu.roll(z, LANES - 16, 1) * sb

    tm, nb = x.shape[0], u_ref.shape[1] // x.shape[0]
    c0 = SSM_WIDTH + GQA_WIDTH + 2 * GQA_KV_WIDTH
    z = mm(0, c0)
    col = lambda n: z[:, n * LANES:(n + 1) * LANES]
    n = 0
    for blk in range(SSM_BLOCKS):
        u_ref[blk, pl.ds(pl.program_id(1), tm, stride=nb), :] = col(n)
        n += 1
    for j in range(GQA_WIDTH // LANES):
        q_ref[0, :, j * LANES:(j + 1) * LANES] = (rope(col(n)) * scale).astype(BF16)
        n += 1
    k_ref[0] = rope(col(n)).astype(BF16)
    v_ref[0] = col(n + 1).astype(BF16)
    nq_ref[0] = (mm(c0, NA_WIDTH) * scale).astype(BF16)
    c0 += NA_WIDTH
    nk_ref[0] = mm(c0, NA_WIDTH).astype(BF16)
    c0 += NA_WIDTH
    nv_ref[0] = mm(c0, NA_WIDTH).astype(BF16)


def _inproj(xs, mod, g, w_in, cos_t, sa_t, sb_t, *, n_lat_tiles, ctx_row):
    b, s, d = xs.shape
    tm = TOKEN_TILE
    n_in = w_in.shape[1]
    assert n_in == MIXER_COLS
    widths = (GQA_WIDTH, GQA_KV_WIDTH, GQA_KV_WIDTH, NA_WIDTH, NA_WIDTH, NA_WIDTH)
    dtypes = (BF16,) * len(widths)
    rope_spec = pl.BlockSpec((tm, LANES), lambda j, bb: (j, 0))
    return pl.pallas_call(
        _inproj_kernel,
        grid=(s // tm, b),
        in_specs=[_tok_spec(d, tm), _mod_spec(d, n_lat_tiles, ctx_row), _resident(g.shape),
                  _resident((d, n_in)), rope_spec, rope_spec, rope_spec],
        out_specs=[_slab_spec(tm, b)] + [_tok_spec(w, tm) for w in widths],
        out_shape=[jax.ShapeDtypeStruct((SSM_BLOCKS, s * b, LANES), F32)]
                  + [jax.ShapeDtypeStruct((b, s, w), dt) for w, dt in zip(widths, dtypes)],
        compiler_params=_params(2),
        name="mixer_in_proj",
    )(xs, mod, g, w_in, cos_t, sa_t, sb_t)


def _softmax_rows(s_ref, p_ref, extra=None):
    s = s_ref[...]
    m = jnp.max(s, axis=1, keepdims=True)
    if extra is not None:
        m = jnp.maximum(m, extra)
    e = jnp.exp(s - m)
    den = jnp.sum(e, axis=1, keepdims=True)
    if extra is not None:
        den = den + jnp.exp(extra - m)
    p_ref[...] = e.astype(BF16)
    return 1.0 / den


def _head_half(qp, low, half):
    return jnp.where(low if half == 0 else jnp.logical_not(low), qp, jnp.zeros_like(qp))


def _gqa_kernel(sink_ref, q_ref, *refs, n_lat):
    nkb = GQA_QBLOCKS + 2
    k_refs, kx_ref = refs[:nkb], refs[nkb]
    v_refs, vx_ref = refs[nkb + 1:2 * nkb + 1], refs[2 * nkb + 1]
    o_ref, s_ref, p_ref = refs[2 * nkb + 2:]
    n_keys = s_ref.shape[2]
    far = 8 * n_keys
    row = lax.broadcasted_iota(jnp.int32, (Q_BLOCK, n_keys), 0)
    col = lax.broadcasted_iota(jnp.int32, (Q_BLOCK, n_keys), 1)
    low = lax.broadcasted_iota(jnp.int32, (Q_BLOCK, LANES), 1) < HEAD_DIM
    sinks = jnp.concatenate([jnp.full((Q_BLOCK, 1), sink_ref[h], F32) for h in GQA_HEAD_PERM], axis=0)
    n_pairs = GQA_WIDTH // LANES
    for a in range(GQA_QBLOCKS):
        i = pl.program_id(1) * GQA_QBLOCKS + a
        is_lat = i < n_lat
        off_p = jnp.where(jnp.logical_and(is_lat, i > 0), 0, far)
        off_c = jnp.where(is_lat, 0, far)
        off_n = jnp.where(jnp.logical_and(is_lat, i < n_lat - 1), 0, far)
        ok = ((col >= row + off_p) & (col < Q_BLOCK)
              | (col >= Q_BLOCK + off_c) & (col < 2 * Q_BLOCK)
              | (col >= 2 * Q_BLOCK) & (col <= 2 * Q_BLOCK + row - off_n)
              | (col >= 3 * Q_BLOCK))
        keys = jnp.concatenate([k_refs[a][0], k_refs[a + 1][0], k_refs[a + 2][0], kx_ref[0]], axis=0)
        vals = jnp.concatenate([v_refs[a][0], v_refs[a + 1][0], v_refs[a + 2][0], vx_ref[0]], axis=0)
        qrows = slice(a * Q_BLOCK, (a + 1) * Q_BLOCK)
        for j in range(n_pairs):
            qp = q_ref[0, qrows, j * LANES:(j + 1) * LANES]
            for half in range(2):
                r0 = (2 * j + half) * Q_BLOCK
                s_ref[a, r0:r0 + Q_BLOCK, :] = jnp.where(ok, _dot_t(_head_half(qp, low, half), keys), NEG)
        inv = _softmax_rows(s_ref.at[a], p_ref.at[a], sinks)
        for j in range(n_pairs):
            outs = []
            for half in range(2):
                r0 = (2 * j + half) * Q_BLOCK
                outs.append(_dot(p_ref[a, r0:r0 + Q_BLOCK, :], vals) * inv[r0:r0 + Q_BLOCK])
            o_ref[0, qrows, j * LANES:(j + 1) * LANES] = jnp.where(low, outs[0], outs[1]).astype(BF16)


def _gqa(sink, q, k, v, *, n_lat, n_ctx_blocks, ctx_out, c_len):
    b = q.shape[0]
    nq = n_lat + (n_ctx_blocks if ctx_out else 0)
    assert n_lat % GQA_QBLOCKS == 0 and n_ctx_blocks % GQA_QBLOCKS == 0
    ctx_idx = n_lat * Q_BLOCK // c_len
    n_keys = 3 * Q_BLOCK + c_len
    qt = GQA_QBLOCKS * Q_BLOCK

    def kv_spec(t):
        return pl.BlockSpec((1, Q_BLOCK, GQA_KV_WIDTH),
                            lambda bb, g: (bb, jnp.clip(g * GQA_QBLOCKS - 1 + t, 0, n_lat - 1), 0))

    kv = [kv_spec(t) for t in range(GQA_QBLOCKS + 2)]
    kv.append(pl.BlockSpec((1, c_len, GQA_KV_WIDTH), lambda bb, g: (bb, ctx_idx, 0)))
    return pl.pallas_call(
        functools.partial(_gqa_kernel, n_lat=n_lat),
        grid=(b, nq // GQA_QBLOCKS),
        in_specs=[pl.BlockSpec(memory_space=pltpu.SMEM),
                  pl.BlockSpec((1, qt, GQA_WIDTH), lambda bb, g: (bb, g, 0))] + kv + kv,
        out_specs=pl.BlockSpec((1, qt, GQA_WIDTH), lambda bb, g: (bb, g, 0)),
        out_shape=jax.ShapeDtypeStruct((b, nq * Q_BLOCK, GQA_WIDTH), BF16),
        scratch_shapes=[pltpu.VMEM((GQA_QBLOCKS, GQA_HEADS * Q_BLOCK, n_keys), F32),
                        pltpu.VMEM((GQA_QBLOCKS, GQA_HEADS * Q_BLOCK, n_keys), BF16)],
        compiler_params=_params(2),
        name="window_gqa",
    )(sink, q, *([k] * (GQA_QBLOCKS + 3)), *([v] * (GQA_QBLOCKS + 3)))


def _na_band_start(r, rows):
    return jnp.clip(r - NA_ROWS // 2, 0, rows - NA_ROWS)


def _na_union_start(g, rows):
    return jnp.minimum(_na_band_start(g * NA_QROWS, rows), rows - NA_UNION)


def _na_kernel(q_ref, kb_ref, vb_ref, kx_ref, vx_ref, bias_ref, o_ref, s_ref, p_ref, *, rows):
    g = pl.program_id(1)
    n_loc = NA_ROWS * GRID_W
    low = lax.broadcasted_iota(jnp.int32, (GRID_W, LANES), 1) < HEAD_DIM
    n_pairs = NA_WIDTH // LANES
    u0 = _na_union_start(g, rows)
    for rho in range(NA_QROWS):
        r = g * NA_QROWS + rho
        is_ctx = r >= rows
        band = _na_band_start(r, rows)
        off = pl.multiple_of(jnp.where(is_ctx, 0, band - u0) * GRID_W, GRID_W)
        dr0 = jnp.where(is_ctx, 0, (NA_ROWS - 1) - (r - band))
        pen = jnp.where(is_ctx, NEG, 0.0).astype(F32)
        qrows = slice(rho * GRID_W, (rho + 1) * GRID_W)
        for j in range(n_pairs):
            sl = slice(j * LANES, (j + 1) * LANES)
            qp = q_ref[0, qrows, sl]
            keys = jnp.concatenate([kb_ref[0, pl.ds(off, n_loc), sl], kx_ref[0, :, sl]], axis=0)
            for half in range(2):
                h = 2 * j + half
                r0 = h * GRID_W
                s = _dot_t(_head_half(qp, low, half), keys)
                bias = jnp.concatenate([bias_ref[h, dr0 + i] for i in range(0, NA_ROWS, 2)], axis=1)
                s_ref[rho, r0:r0 + GRID_W, 0:n_loc] = s[:, 0:n_loc] + (bias + pen)
                s_ref[rho, r0:r0 + GRID_W, n_loc:] = s[:, n_loc:]
        inv = _softmax_rows(s_ref.at[rho], p_ref.at[rho])
        for j in range(n_pairs):
            sl = slice(j * LANES, (j + 1) * LANES)
            vals = jnp.concatenate([vb_ref[0, pl.ds(off, n_loc), sl], vx_ref[0, :, sl]], axis=0)
            outs = []
            for half in range(2):
                r0 = (2 * j + half) * GRID_W
                outs.append(_dot(p_ref[rho, r0:r0 + GRID_W, :], vals) * inv[r0:r0 + GRID_W])
            o_ref[0, qrows, sl] = jnp.where(low, outs[0], outs[1]).astype(BF16)


def _na(q, k, v, bias, *, rows, n_ctx_rows, ctx_out, c_len):
    b = q.shape[0]
    assert rows % NA_QROWS == 0 and n_ctx_rows % NA_QROWS == 0 and rows >= NA_UNION
    ng = (rows + (n_ctx_rows if ctx_out else 0)) // NA_QROWS
    qt = NA_QROWS * GRID_W
    ctx_idx = rows * GRID_W // c_len
    n_keys = NA_ROWS * GRID_W + c_len
    union_spec = pl.BlockSpec((pl.Element(1), pl.Element(NA_UNION * GRID_W), pl.Element(NA_WIDTH)),
                              lambda bb, g: (bb, _na_union_start(g, rows) * GRID_W, 0))
    ctx_spec = pl.BlockSpec((1, c_len, NA_WIDTH), lambda bb, g: (bb, ctx_idx, 0))
    return pl.pallas_call(
        functools.partial(_na_kernel, rows=rows),
        grid=(b, ng),
        in_specs=[pl.BlockSpec((1, qt, NA_WIDTH), lambda bb, g: (bb, g, 0)),
                  union_spec, union_spec, ctx_spec, ctx_spec, _resident(bias.shape)],
        out_specs=pl.BlockSpec((1, qt, NA_WIDTH), lambda bb, g: (bb, g, 0)),
        out_shape=jax.ShapeDtypeStruct((b, ng * qt, NA_WIDTH), BF16),
        scratch_shapes=[pltpu.VMEM((NA_QROWS, NA_HEADS * GRID_W, n_keys), F32),
                        pltpu.VMEM((NA_QROWS, NA_HEADS * GRID_W, n_keys), BF16)],
        compiler_params=_params(2),
        name="neighborhood_attn",
    )(q, k, v, k, v, bias)


def _na_bias_pairs(rpb):
    cols = np.arange(GRID_W)
    col_start = np.clip(cols - NA_COLS // 2, 0, GRID_W - NA_COLS)
    kc = np.arange(GRID_W)
    in_win = (kc[None, :] >= col_start[:, None]) & (kc[None, :] < col_start[:, None] + NA_COLS)
    col_idx = np.clip(kc[None, :] - cols[:, None] + (NA_COLS - 1), 0, 2 * NA_COLS - 2)
    tab = jnp.where(in_win[None, None], rpb.astype(F32)[:, :, col_idx], NEG)
    return jnp.concatenate([tab[:, :-1], tab[:, 1:]], axis=-1)


def _s5_prep_kernel(are_ref, aim_ref, ldt_ref, bre_ref, bim_ref, lbr_ref, lbi_ref, bbr_ref, bbi_ref):
    are, aim = are_ref[...], aim_ref[...]
    dt = jnp.exp(ldt_ref[...])
    mag = jnp.exp(are * dt)
    lbr, lbi = mag * jnp.cos(aim * dt), mag * jnp.sin(aim * dt)
    lbr_ref[...] = lbr
    lbi_ref[...] = lbi
    nr, ni = lbr - 1.0, lbi
    den = are * are + aim * aim
    cr, ci = (nr * are + ni * aim) / den, (ni * are - nr * aim) / den
    br, bi = bre_ref[...], bim_ref[...]
    bbr_ref[...] = cr * br - ci * bi
    bbi_ref[...] = cr * bi + ci * br


def _s5_prep(a_re, a_im, log_dt, b_re, b_im):
    nd, g, p = a_re.shape
    i = b_re.shape[-1]
    n = nd * g * p
    col = lambda z: z.reshape(n, 1)
    ldt = jnp.broadcast_to(log_dt[:, :, None], (nd, g, p))
    whole = lambda w: pl.BlockSpec((n, w), lambda: (0, 0))
    lbr, lbi, bbr, bbi = pl.pallas_call(
        _s5_prep_kernel,
        in_specs=[whole(1)] * 3 + [whole(i)] * 2,
        out_specs=[whole(1)] * 2 + [whole(i)] * 2,
        out_shape=[jax.ShapeDtypeStruct((n, 1), F32)] * 2 + [jax.ShapeDtypeStruct((n, i), F32)] * 2,
        name="s5_prep",
    )(col(a_re), col(a_im), col(ldt), b_re.reshape(n, i), b_im.reshape(n, i))
    return lbr.reshape(nd, g, p), lbi.reshape(nd, g, p), bbr.reshape(nd, g, p, i), bbi.reshape(nd, g, p, i)


def _s5_block_operands(lbr, lbi, bbr, bbi, c_re, c_im):
    nd, g, p = lbr.shape
    i = bbr.shape[-1]
    nb, gb = SSM_BLOCKS, SSM_BLOCK_GROUPS
    eye = jnp.eye(gb, dtype=F32)

    def in_map(z):
        z = z.reshape(nd, nb, gb, p, i).transpose(0, 1, 2, 4, 3)
        return (z[:, :, :, :, None, :] * eye[None, None, :, None, :, None]).reshape(nd, nb, gb * i, gb * p)

    def out_map(z):
        z = z.reshape(nd, nb, gb, i, p).transpose(0, 1, 2, 4, 3)
        return (z[:, :, :, :, None, :] * eye[None, None, :, None, :, None]).reshape(nd, nb, gb * p, gb * i)

    lam = jnp.stack([lbr.reshape(nd, nb, gb * p), lbi.reshape(nd, nb, gb * p)], axis=2)
    w_in = jnp.concatenate([in_map(bbr), in_map(bbi)], axis=-1).astype(BF16)
    w_out = jnp.concatenate([out_map(c_re.astype(F32)), -out_map(c_im.astype(F32))], axis=2).astype(BF16)
    return lam, w_in, w_out


def _s5_kernel(u_ref, lam_ref, win_ref, wout_ref, y_ref, st_ref, carry_ref, *, nb, tt):
    d = pl.program_id(0)
    ns = SSM_BLOCK_STATE

    @pl.when(pl.program_id(1) == 0)
    def _():
        carry_ref[...] = jnp.zeros_like(carry_ref)

    for j in range(SSM_BLOCKS):
        st_ref[j] = _dot(u_ref[j].astype(BF16), win_ref[0, j])

    lam = lam_ref[0]
    lr = [jnp.broadcast_to(lam[j, 0:1, :], (nb, ns)) for j in range(SSM_BLOCKS)]
    li = [jnp.broadcast_to(lam[j, 1:2, :], (nb, ns)) for j in range(SSM_BLOCKS)]

    def step(t, carry):
        r0 = pl.multiple_of(jnp.where(d == 0, t, tt - 1 - t) * nb, nb)
        new = []
        for j in range(SSM_BLOCKS):
            sr, si = carry[2 * j], carry[2 * j + 1]
            nr = lr[j] * sr - li[j] * si + st_ref[j, pl.ds(r0, nb), 0:ns]
            ni = lr[j] * si + li[j] * sr + st_ref[j, pl.ds(r0, nb), ns:2 * ns]
            st_ref[j, pl.ds(r0, nb), 0:ns] = nr
            st_ref[j, pl.ds(r0, nb), ns:2 * ns] = ni
            new += [nr, ni]
        return tuple(new)

    init = tuple(carry_ref[k] for k in range(2 * SSM_BLOCKS))
    last = lax.fori_loop(0, tt, step, init, unroll=2)
    for k in range(2 * SSM_BLOCKS):
        carry_ref[k] = last[k]
    for j in range(SSM_BLOCKS):
        y_ref[0, j] = _dot(st_ref[j].astype(BF16), wout_ref[0, j])


def _s5_scan(u2, lam, w_in, w_out, *, nb, n_lat_tiles, n_ctx_tiles):
    _, rows_total, _ = u2.shape
    tt = SSM_TILE
    rows = tt * nb
    n_tiles = n_lat_tiles + n_ctx_tiles
    ns2 = 2 * SSM_BLOCK_STATE

    def tile(d, k):
        fwd = jnp.where(k < n_ctx_tiles, n_lat_tiles + k, k - n_ctx_tiles)
        bwd = jnp.where(k < n_ctx_tiles, n_tiles - 1 - k, n_lat_tiles - 1 - (k - n_ctx_tiles))
        return jnp.where(d == 0, fwd, bwd)

    def par(*tail):
        return pl.BlockSpec((1,) + tail, lambda d, k: (d,) + (0,) * len(tail))

    return pl.pallas_call(
        functools.partial(_s5_kernel, nb=nb, tt=tt),
        grid=(2, n_tiles),
        in_specs=[pl.BlockSpec((SSM_BLOCKS, rows, LANES), lambda d, k: (0, tile(d, k), 0)),
                  par(SSM_BLOCKS, 2, SSM_BLOCK_STATE), par(SSM_BLOCKS, LANES, ns2), par(SSM_BLOCKS, ns2, LANES)],
        out_specs=pl.BlockSpec((1, SSM_BLOCKS, rows, LANES), lambda d, k: (d, 0, tile(d, k), 0)),
        out_shape=jax.ShapeDtypeStruct((2, SSM_BLOCKS, rows_total, LANES), F32),
        scratch_shapes=[pltpu.VMEM((SSM_BLOCKS, rows, ns2), F32),
                        pltpu.VMEM((2 * SSM_BLOCKS, nb, SSM_BLOCK_STATE), F32)],
        compiler_params=_params(2),
        name="s5_scan",
    )(u2, lam, w_in, w_out)


def _merge_kernel(x_ref, mod_ref, g_ref, ysf_ref, ysb_ref, u_ref, ya_ref, yn_ref, d_ref,
                  wgt_ref, wglu_ref, wps_ref, wpa_ref, wpn_ref, wo_ref, o_ref):
    d = x_ref.shape[2]
    tm = x_ref.shape[1]
    x = x_ref[0]
    h = _mod_norm(x, g_ref[1:2, :], mod_ref[0, 3:4, :], mod_ref[0, 4:5, :]).astype(BF16)
    gate = lambda n: jax.nn.sigmoid(_dot(h, wgt_ref[:, n * d:(n + 1) * d]))
    mine = pl.ds(pl.program_id(1), tm, stride=u_ref.shape[1] // tm)
    y = jnp.concatenate([ysf_ref[0, blk, mine, :] + ysb_ref[0, blk, mine, :] for blk in range(SSM_BLOCKS)], axis=1)
    y = y + d_ref[...] * jnp.concatenate([u_ref[blk, mine, :] for blk in range(SSM_BLOCKS)], axis=1)
    t = 0.5 * y * (1.0 + jnp.tanh(math.sqrt(2.0 / math.pi) * (y + 0.044715 * (y * y * y))))
    glu = (t * jax.nn.sigmoid(_dot(t.astype(BF16), wglu_ref[...]))).astype(BF16)
    m = (gate(0) * _dot(glu, wps_ref[...]) + gate(1) * _dot(ya_ref[0], wpa_ref[...])
         + gate(2) * _dot(yn_ref[0], wpn_ref[...]))
    o_ref[0] = x + mod_ref[0, 5:6, :] * _dot(m.astype(BF16), wo_ref[...])


def _merge(xs, mod, g, ys, u2, ya, yn, ssm_d, wgt, wglu, wps, wpa, wpn, wo, *, n_tok, n_lat_tiles, ctx_row):
    b, _, d = xs.shape
    tm = TOKEN_TILE
    return pl.pallas_call(
        _merge_kernel,
        grid=(n_tok // tm, b),
        in_specs=[_tok_spec(d, tm), _mod_spec(d, n_lat_tiles, ctx_row), _resident(g.shape),
                  _slab_spec(tm, b, (0,)), _slab_spec(tm, b, (1,)), _slab_spec(tm, b),
                  _tok_spec(GQA_WIDTH, tm), _tok_spec(NA_WIDTH, tm), _resident((1, SSM_WIDTH)),
                  _resident(wgt.shape), _resident(wglu.shape),
                  _resident(wps.shape), _resident(wpa.shape), _resident(wpn.shape), _resident(wo.shape)],
        out_specs=_tok_spec(d, tm),
        out_shape=jax.ShapeDtypeStruct((b, n_tok, d), F32),
        compiler_params=_params(2),
        name="mixer_merge",
    )(xs, mod, g, ys, ys, u2, ya, yn, ssm_d.reshape(1, SSM_WIDTH), wgt, wglu, wps, wpa, wpn, wo)


def _rope_tables(l, c_len):
    t = jnp.arange(l)
    pos = jnp.stack([t // GRID_W, t % GRID_W], axis=-1).astype(F32)
    half = HEAD_DIM // 2
    inv = ROPE_THETA ** (-jnp.arange(0, half, 2, dtype=F32) / half)
    ang = pos[:, :, None] * inv
    cos, sin = jnp.cos(ang), jnp.sin(ang)
    zero = jnp.zeros_like(sin[:, 0])
    cos_h = jnp.concatenate([cos[:, 0], cos[:, 0], cos[:, 1], cos[:, 1]], axis=-1)
    sa_h = jnp.concatenate([zero, sin[:, 0], zero, sin[:, 1]], axis=-1)
    sb_h = jnp.concatenate([-sin[:, 0], zero, -sin[:, 1], zero], axis=-1)

    def full(tab, ctx_fill):
        tab = jnp.concatenate([tab, tab], axis=-1)
        return jnp.concatenate([tab, jnp.full((c_len, LANES), ctx_fill, F32)], axis=0)

    return full(cos_h, 1.0), full(sa_h, 0.0), full(sb_h, 0.0)


def kernel(x, c, ctx, c_ctx, w_ada, b_ada, norm_g, ffn1_wg, ffn1_wu, ffn1_wd, ffn2_wg, ffn2_wu, ffn2_wd, w_in,
           ssm_a_re, ssm_a_im, ssm_log_dt, ssm_b_re, ssm_b_im, ssm_c_re, ssm_c_im, ssm_d, ssm_w_glu, gqa_sink,
           na_rpb, w_p_ssm, w_p_gqa, w_p_na, w_out, final_g):
    b, l, d = x.shape
    c_len = ctx.shape[1]
    s = l + c_len
    depth = w_ada.shape[0]
    tm = TOKEN_TILE
    assert b + 1 <= MOD_ROWS and l % tm == 0 and c_len % tm == 0 and l % (GRID_W * NA_ROWS) == 0
    assert c_len % Q_BLOCK == 0 and l % c_len == 0 and b % SUBLANES == 0
    assert l % SSM_TILE == 0 and c_len % SSM_TILE == 0
    n_lat_tiles = l // tm

    cin = jnp.zeros((MOD_ROWS, d), F32).at[:b].set(c).at[b].set(c_ctx)
    mods = _ada(cin, w_ada, b_ada).reshape(depth, MOD_ROWS, N_MOD, d)
    cos_t, sa_t, sb_t = _rope_tables(l, c_len)

    perm = np.asarray(GQA_HEAD_PERM)
    q0 = SSM_WIDTH
    w_q = w_in[:, :, q0:q0 + GQA_WIDTH].reshape(depth, d, GQA_HEADS, HEAD_DIM)[:, :, perm]
    w_in_p = jnp.concatenate([w_in[:, :, :q0], w_q.reshape(depth, d, GQA_WIDTH),
                              w_in[:, :, q0 + GQA_WIDTH:MIXER_COLS]], axis=-1).astype(BF16)
    w_gt = w_in[:, :, MIXER_COLS:].astype(BF16)
    w_pa = w_p_gqa.reshape(depth, GQA_HEADS, HEAD_DIM, d)[:, perm].reshape(depth, GQA_WIDTH, d).astype(BF16)
    bf = lambda z: z.astype(BF16)
    f1 = (bf(ffn1_wg), bf(ffn1_wu), bf(ffn1_wd))
    f2 = (bf(ffn2_wg), bf(ffn2_wu), bf(ffn2_wd))
    w_glu, w_ps, w_pn, w_o = bf(ssm_w_glu), bf(w_p_ssm), bf(w_p_na), bf(w_out)

    tile_kw = dict(n_lat_tiles=n_lat_tiles, ctx_row=b)
    xs = x
    for li in range(depth):
        ctx_out = li < depth - 1
        n_tok = s if ctx_out else l
        mod, g = mods[li], norm_g[li]
        xs = _ffn(xs, mod, g, f1[0][li], f1[1][li], f1[2][li], k0=0, gi=0, n_tok=s,
                  ctx=ctx if li == 0 else None, **tile_kw)
        u2, q, k, v, nq, nk, nv = _inproj(xs, mod, g, w_in_p[li], cos_t, sa_t, sb_t, **tile_kw)
        lam, s_in, s_out = _s5_block_operands(
            *_s5_prep(ssm_a_re[li], ssm_a_im[li], ssm_log_dt[li], ssm_b_re[li], ssm_b_im[li]),
            ssm_c_re[li], ssm_c_im[li])
        ys = _s5_scan(u2, lam, s_in, s_out, nb=b, n_lat_tiles=l // SSM_TILE, n_ctx_tiles=c_len // SSM_TILE)
        ya = _gqa(gqa_sink[li], q, k, v, n_lat=l // Q_BLOCK, n_ctx_blocks=c_len // Q_BLOCK, ctx_out=ctx_out,
                  c_len=c_len)
        yn = _na(nq, nk, nv, _na_bias_pairs(na_rpb[li]), rows=l // GRID_W, n_ctx_rows=c_len // GRID_W,
                 ctx_out=ctx_out, c_len=c_len)
        xs = _merge(xs, mod, g, ys, u2, ya, yn, ssm_d[li], w_gt[li], w_glu[li], w_ps[li], w_pa[li], w_pn[li],
                    w_o[li], n_tok=n_tok, **tile_kw)
        xs = _ffn(xs, mod, g, f2[0][li], f2[1][li], f2[2][li], k0=6, gi=2, n_tok=n_tok,
                  final_g=None if ctx_out else final_g, **tile_kw)
    return xs
```

```python
import functools
import math

import jax
import jax.numpy as jnp
import numpy as np
from jax import lax
from jax.experimental import pallas as pl
from jax.experimental.pallas import tpu as pltpu

F32 = jnp.float32
BF16 = jnp.bfloat16

EPS = 1e-6
NEG = -1e30
N_MOD = 9
GRID_W = 64
HEAD_DIM = 64
GQA_HEADS = 8
GQA_KV_HEADS = 2
NA_HEADS = 8
NA_ROWS = 8
NA_COLS = 16
Q_BLOCK = 128
ROPE_THETA = 10000.0
SSM_GROUP = 16
SSM_GROUPS = 24
SSM_STATE = 64
SSM_WIDTH = SSM_GROUP * SSM_GROUPS
GQA_WIDTH = GQA_HEADS * HEAD_DIM
GQA_KV_WIDTH = GQA_KV_HEADS * HEAD_DIM
NA_WIDTH = NA_HEADS * HEAD_DIM
N_BRANCH = 3
MIXER_COLS = SSM_WIDTH + GQA_WIDTH + 2 * GQA_KV_WIDTH + 3 * NA_WIDTH

LANES = 128
SUBLANES = 8
MOD_ROWS = 16
TOKEN_TILE = 256
FFN_CHUNK = 256
GQA_QBLOCKS = 2
NA_QROWS = 4
NA_UNION = NA_QROWS + NA_ROWS - 1
SSM_TILE = 64
SSM_BLOCK_GROUPS = LANES // SSM_GROUP
SSM_BLOCKS = SSM_GROUPS // SSM_BLOCK_GROUPS
SSM_BLOCK_STATE = SSM_BLOCK_GROUPS * SSM_STATE
VMEM_LIMIT = 56 * 1024 * 1024
GQA_HEAD_PERM = (0, 4, 1, 5, 2, 6, 3, 7)


def _params(n_axes):
    return pltpu.CompilerParams(dimension_semantics=("arbitrary",) * n_axes, vmem_limit_bytes=VMEM_LIMIT)


def _resident(shape):
    return pl.BlockSpec(shape, lambda *_: (0,) * len(shape), pipeline_mode=pl.Buffered(1))


def _layer_resident(stack, li):
    nd = stack.ndim
    return pl.BlockSpec((None,) + stack.shape[1:], lambda *_: (li,) + (0,) * (nd - 1),
                        pipeline_mode=pl.Buffered(1))


def _dot(a, b):
    return jnp.dot(a, b, preferred_element_type=F32)


def _dot_t(a, b):
    return lax.dot_general(a, b, (((1,), (1,)), ((), ())), preferred_element_type=F32)


def _rmsnorm(x, g):
    return x * lax.rsqrt(jnp.mean(x * x, axis=-1, keepdims=True) + EPS) * g


def _mod_norm(x, g, shift, scale):
    return _rmsnorm(x, g) * (1.0 + scale) + shift


def _ada_kernel(c_ref, w_ref, b_ref, o_ref):
    s = c_ref[...]
    s = s * jax.nn.sigmoid(s)
    o_ref[0] = _dot(s.astype(BF16), w_ref[0].astype(BF16)) + b_ref[0]


def _ada(cin, w_ada, b_ada):
    depth, d, n = w_ada.shape
    tn = n // 4
    return pl.pallas_call(
        _ada_kernel,
        grid=(depth, n // tn),
        in_specs=[
            pl.BlockSpec((MOD_ROWS, d), lambda l, j: (0, 0)),
            pl.BlockSpec((1, d, tn), lambda l, j: (l, 0, j)),
            pl.BlockSpec((1, 1, tn), lambda l, j: (l, 0, j)),
        ],
        out_specs=pl.BlockSpec((1, MOD_ROWS, tn), lambda l, j: (l, 0, j)),
        out_shape=jax.ShapeDtypeStruct((depth, MOD_ROWS, n), F32),
        compiler_params=_params(2),
        name="ada_mod",
    )(cin, w_ada, b_ada.reshape(depth, 1, n))


def _tok_spec(width, tm):
    return pl.BlockSpec((1, tm, width), lambda j, b: (b, j, 0))


def _slab_spec(tm, nb, lead=()):
    n = len(lead)
    return pl.BlockSpec((1,) * n + (SSM_BLOCKS, tm * nb, LANES), lambda j, b: lead + (0, j, 0))


def _mod_spec(d, n_lat_tiles, ctx_row):
    return pl.BlockSpec((1, N_MOD, d), lambda j, b: (jnp.where(j >= n_lat_tiles, ctx_row, b), 0, 0))


def _ffn_kernel(*refs, k0, gi, final, n_lat_tiles, split_in):
    refs = list(refs)
    x_ref = refs.pop(0)
    ctx_ref = refs.pop(0) if split_in else None
    mod_ref, g_ref, wg_ref, wu_ref, wd_ref = refs[:5]
    fg_ref = refs[5] if final else None
    o_ref, t_ref = refs[-2:]
    x = x_ref[0]
    if split_in:
        x = jnp.where(pl.program_id(0) >= n_lat_tiles, ctx_ref[0], x)
    h = _mod_norm(x, g_ref[gi:gi + 1, :], mod_ref[0, k0:k0 + 1, :], mod_ref[0, k0 + 1:k0 + 2, :]).astype(BF16)
    f = wg_ref.shape[1]
    for c0 in range(0, f, FFN_CHUNK):
        a = _dot(h, wg_ref[:, c0:c0 + FFN_CHUNK])
        u = _dot(h, wu_ref[:, c0:c0 + FFN_CHUNK])
        t_ref[:, c0:c0 + FFN_CHUNK] = (a * jax.nn.sigmoid(a) * u).astype(BF16)
    y = _dot(t_ref[...], wd_ref[...])
    out = x + (0.5 * mod_ref[0, k0 + 2:k0 + 3, :]) * y
    if final:
        out = _rmsnorm(out, fg_ref[...])
    o_ref[0] = out


def _ffn(xs, mod, g, wg, wu, wd, *, li, k0, gi, n_tok, n_lat_tiles, ctx_row, final_g=None, ctx=None):
    b, _, d = xs.shape
    f = wg.shape[2]
    tm = TOKEN_TILE
    final = final_g is not None
    split_in = ctx is not None
    if split_in:
        in_specs = [pl.BlockSpec((1, tm, d), lambda j, bb: (bb, jnp.minimum(j, n_lat_tiles - 1), 0)),
                    pl.BlockSpec((1, tm, d), lambda j, bb: (bb, jnp.maximum(j - n_lat_tiles, 0), 0))]
        args = [xs, ctx]
    else:
        in_specs = [_tok_spec(d, tm)]
        args = [xs]
    in_specs += [_mod_spec(d, n_lat_tiles, ctx_row), _resident(g.shape),
                 _layer_resident(wg, li), _layer_resident(wu, li), _layer_resident(wd, li)]
    args += [mod, g, wg, wu, wd]
    if final:
        in_specs.append(_resident((1, d)))
        args.append(final_g.reshape(1, d))
    return pl.pallas_call(
        functools.partial(_ffn_kernel, k0=k0, gi=gi, final=final, n_lat_tiles=n_lat_tiles, split_in=split_in),
        grid=(n_tok // tm, b),
        in_specs=in_specs,
        out_specs=_tok_spec(d, tm),
        out_shape=jax.ShapeDtypeStruct((b, n_tok, d), F32),
        scratch_shapes=[pltpu.VMEM((tm, f), BF16)],
        compiler_params=_params(2),
        name="ffn_half_final" if final else "ffn_half",
    )(*args)


def _inproj_kernel(x_ref, mod_ref, g_ref, w_ref, cos_ref, sa_ref, sb_ref,
                   u_ref, q_ref, k_ref, v_ref, nq_ref, nk_ref, nv_ref):
    x = x_ref[0]
    h = _mod_norm(x, g_ref[1:2, :], mod_ref[0, 3:4, :], mod_ref[0, 4:5, :]).astype(BF16)
    cos, sa, sb = cos_ref[...], sa_ref[...], sb_ref[...]
    scale = HEAD_DIM ** -0.5

    def mm(c0, n):
        return _dot(h, w_ref[:, c0:c0 + n])

    def rope(z):
        return z * cos + pltpu.roll(z, 16, 1) * sa + pltpu.roll(z, LANES - 16, 1) * sb

    tm, nb = x.shape[0], u_ref.shape[1] // x.shape[0]
    c0 = SSM_WIDTH + GQA_WIDTH + 2 * GQA_KV_WIDTH
    z = mm(0, c0)
    col = lambda n: z[:, n * LANES:(n + 1) * LANES]
    n = 0
    for blk in range(SSM_BLOCKS):
        u_ref[blk, pl.ds(pl.program_id(1), tm, stride=nb), :] = col(n)
        n += 1
    for j in range(GQA_WIDTH // LANES):
        q_ref[0, :, j * LANES:(j + 1) * LANES] = (rope(col(n)) * scale).astype(BF16)
        n += 1
    k_ref[0] = rope(col(n)).astype(BF16)
    v_ref[0] = col(n + 1).astype(BF16)
    nq_ref[0] = (mm(c0, NA_WIDTH) * scale).astype(BF16)
    c0 += NA_WIDTH
    nk_ref[0] = mm(c0, NA_WIDTH).astype(BF16)
    c0 += NA_WIDTH
    nv_ref[0] = mm(c0, NA_WIDTH).astype(BF16)


def _inproj(xs, mod, g, w_in, cos_t, sa_t, sb_t, *, li, n_lat_tiles, ctx_row):
    b, s, d = xs.shape
    tm = TOKEN_TILE
    assert w_in.shape[2] == MIXER_COLS
    widths = (GQA_WIDTH, GQA_KV_WIDTH, GQA_KV_WIDTH, NA_WIDTH, NA_WIDTH, NA_WIDTH)
    dtypes = (BF16,) * len(widths)
    rope_spec = pl.BlockSpec((tm, LANES), lambda j, bb: (j, 0))
    return pl.pallas_call(
        _inproj_kernel,
        grid=(s // tm, b),
        in_specs=[_tok_spec(d, tm), _mod_spec(d, n_lat_tiles, ctx_row), _resident(g.shape),
                  _layer_resident(w_in, li), rope_spec, rope_spec, rope_spec],
        out_specs=[_slab_spec(tm, b)] + [_tok_spec(w, tm) for w in widths],
        out_shape=[jax.ShapeDtypeStruct((SSM_BLOCKS, s * b, LANES), F32)]
                  + [jax.ShapeDtypeStruct((b, s, w), dt) for w, dt in zip(widths, dtypes)],
        compiler_params=_params(2),
        name="mixer_in_proj",
    )(xs, mod, g, w_in, cos_t, sa_t, sb_t)


def _softmax_rows(s_ref, p_ref, extra=None):
    s = s_ref[...]
    m = jnp.max(s, axis=1, keepdims=True)
    if extra is not None:
        m = jnp.maximum(m, extra)
    e = jnp.exp(s - m)
    den = jnp.sum(e, axis=1, keepdims=True)
    if extra is not None:
        den = den + jnp.exp(extra - m)
    p_ref[...] = e.astype(BF16)
    return 1.0 / den


def _head_half(qp, low, half):
    return jnp.where(low if half == 0 else jnp.logical_not(low), qp, jnp.zeros_like(qp))


def _gqa_kernel(sink_ref, q_ref, *refs, n_lat):
    nkb = GQA_QBLOCKS + 2
    k_refs, kx_ref = refs[:nkb], refs[nkb]
    v_refs, vx_ref = refs[nkb + 1:2 * nkb + 1], refs[2 * nkb + 1]
    o_ref, s_ref, p_ref = refs[2 * nkb + 2:]
    n_keys = s_ref.shape[2]
    far = 8 * n_keys
    row = lax.broadcasted_iota(jnp.int32, (Q_BLOCK, n_keys), 0)
    col = lax.broadcasted_iota(jnp.int32, (Q_BLOCK, n_keys), 1)
    low = lax.broadcasted_iota(jnp.int32, (Q_BLOCK, LANES), 1) < HEAD_DIM
    sinks = jnp.concatenate([jnp.full((Q_BLOCK, 1), sink_ref[h], F32) for h in GQA_HEAD_PERM], axis=0)
    n_pairs = GQA_WIDTH // LANES
    for a in range(GQA_QBLOCKS):
        i = pl.program_id(1) * GQA_QBLOCKS + a
        is_lat = i < n_lat
        off_p = jnp.where(jnp.logical_and(is_lat, i > 0), 0, far)
        off_c = jnp.where(is_lat, 0, far)
        off_n = jnp.where(jnp.logical_and(is_lat, i < n_lat - 1), 0, far)
        ok = ((col >= row + off_p) & (col < Q_BLOCK)
              | (col >= Q_BLOCK + off_c) & (col < 2 * Q_BLOCK)
              | (col >= 2 * Q_BLOCK) & (col <= 2 * Q_BLOCK + row - off_n)
              | (col >= 3 * Q_BLOCK))
        keys = jnp.concatenate([k_refs[a][0], k_refs[a + 1][0], k_refs[a + 2][0], kx_ref[0]], axis=0)
        vals = jnp.concatenate([v_refs[a][0], v_refs[a + 1][0], v_refs[a + 2][0], vx_ref[0]], axis=0)
        qrows = slice(a * Q_BLOCK, (a + 1) * Q_BLOCK)
        for j in range(n_pairs):
            qp = q_ref[0, qrows, j * LANES:(j + 1) * LANES]
            for half in range(2):
                r0 = (2 * j + half) * Q_BLOCK
                s_ref[a, r0:r0 + Q_BLOCK, :] = jnp.where(ok, _dot_t(_head_half(qp, low, half), keys), NEG)
        inv = _softmax_rows(s_ref.at[a], p_ref.at[a], sinks)
        for j in range(n_pairs):
            outs = []
            for half in range(2):
                r0 = (2 * j + half) * Q_BLOCK
                outs.append(_dot(p_ref[a, r0:r0 + Q_BLOCK, :], vals) * inv[r0:r0 + Q_BLOCK])
            o_ref[0, qrows, j * LANES:(j + 1) * LANES] = jnp.where(low, outs[0], outs[1]).astype(BF16)


def _gqa(sink, q, k, v, *, n_lat, n_ctx_blocks, ctx_out, c_len):
    b = q.shape[0]
    nq = n_lat + (n_ctx_blocks if ctx_out else 0)
    assert n_lat % GQA_QBLOCKS == 0 and n_ctx_blocks % GQA_QBLOCKS == 0
    ctx_idx = n_lat * Q_BLOCK // c_len
    n_keys = 3 * Q_BLOCK + c_len
    qt = GQA_QBLOCKS * Q_BLOCK

    def kv_spec(t):
        return pl.BlockSpec((1, Q_BLOCK, GQA_KV_WIDTH),
                            lambda bb, g: (bb, jnp.clip(g * GQA_QBLOCKS - 1 + t, 0, n_lat - 1), 0))

    kv = [kv_spec(t) for t in range(GQA_QBLOCKS + 2)]
    kv.append(pl.BlockSpec((1, c_len, GQA_KV_WIDTH), lambda bb, g: (bb, ctx_idx, 0)))
    return pl.pallas_call(
        functools.partial(_gqa_kernel, n_lat=n_lat),
        grid=(b, nq // GQA_QBLOCKS),
        in_specs=[pl.BlockSpec(memory_space=pltpu.SMEM),
                  pl.BlockSpec((1, qt, GQA_WIDTH), lambda bb, g: (bb, g, 0))] + kv + kv,
        out_specs=pl.BlockSpec((1, qt, GQA_WIDTH), lambda bb, g: (bb, g, 0)),
        out_shape=jax.ShapeDtypeStruct((b, nq * Q_BLOCK, GQA_WIDTH), BF16),
        scratch_shapes=[pltpu.VMEM((GQA_QBLOCKS, GQA_HEADS * Q_BLOCK, n_keys), F32),
                        pltpu.VMEM((GQA_QBLOCKS, GQA_HEADS * Q_BLOCK, n_keys), BF16)],
        compiler_params=_params(2),
        name="window_gqa",
    )(sink, q, *([k] * (GQA_QBLOCKS + 3)), *([v] * (GQA_QBLOCKS + 3)))


def _na_band_start(r, rows):
    return jnp.clip(r - NA_ROWS // 2, 0, rows - NA_ROWS)


def _na_union_start(g, rows):
    return jnp.minimum(_na_band_start(g * NA_QROWS, rows), rows - NA_UNION)


def _na_kernel(q_ref, kb_ref, vb_ref, kx_ref, vx_ref, bias_ref, o_ref, s_ref, p_ref, *, rows):
    g = pl.program_id(1)
    n_loc = NA_ROWS * GRID_W
    low = lax.broadcasted_iota(jnp.int32, (GRID_W, LANES), 1) < HEAD_DIM
    n_pairs = NA_WIDTH // LANES
    u0 = _na_union_start(g, rows)
    for rho in range(NA_QROWS):
        r = g * NA_QROWS + rho
        is_ctx = r >= rows
        band = _na_band_start(r, rows)
        off = pl.multiple_of(jnp.where(is_ctx, 0, band - u0) * GRID_W, GRID_W)
        dr0 = jnp.where(is_ctx, 0, (NA_ROWS - 1) - (r - band))
        pen = jnp.where(is_ctx, NEG, 0.0).astype(F32)
        qrows = slice(rho * GRID_W, (rho + 1) * GRID_W)
        for j in range(n_pairs):
            sl = slice(j * LANES, (j + 1) * LANES)
            qp = q_ref[0, qrows, sl]
            keys = jnp.concatenate([kb_ref[0, pl.ds(off, n_loc), sl], kx_ref[0, :, sl]], axis=0)
            s2 = _dot_t(jnp.concatenate([_head_half(qp, low, 0), _head_half(qp, low, 1)], axis=0), keys)
            for half in range(2):
                h = 2 * j + half
                r0 = h * GRID_W
                s = s2[half * GRID_W:(half + 1) * GRID_W]
                bias = jnp.concatenate([bias_ref[h, dr0 + i] for i in range(0, NA_ROWS, 2)], axis=1)
                s_ref[rho, r0:r0 + GRID_W, 0:n_loc] = s[:, 0:n_loc] + (bias + pen)
                s_ref[rho, r0:r0 + GRID_W, n_loc:] = s[:, n_loc:]
        inv = _softmax_rows(s_ref.at[rho], p_ref.at[rho])
        for j in range(n_pairs):
            sl = slice(j * LANES, (j + 1) * LANES)
            vals = jnp.concatenate([vb_ref[0, pl.ds(off, n_loc), sl], vx_ref[0, :, sl]], axis=0)
            r0 = 2 * j * GRID_W
            o2 = _dot(p_ref[rho, r0:r0 + 2 * GRID_W, :], vals) * inv[r0:r0 + 2 * GRID_W]
            o_ref[0, qrows, sl] = jnp.where(low, o2[:GRID_W], o2[GRID_W:]).astype(BF16)


def _na(q, k, v, bias, *, rows, n_ctx_rows, ctx_out, c_len):
    b = q.shape[0]
    assert rows % NA_QROWS == 0 and n_ctx_rows % NA_QROWS == 0 and rows >= NA_UNION
    ng = (rows + (n_ctx_rows if ctx_out else 0)) // NA_QROWS
    qt = NA_QROWS * GRID_W
    ctx_idx = rows * GRID_W // c_len
    n_keys = NA_ROWS * GRID_W + c_len
    union_spec = pl.BlockSpec((pl.Element(1), pl.Element(NA_UNION * GRID_W), pl.Element(NA_WIDTH)),
                              lambda bb, g: (bb, _na_union_start(g, rows) * GRID_W, 0))
    ctx_spec = pl.BlockSpec((1, c_len, NA_WIDTH), lambda bb, g: (bb, ctx_idx, 0))
    return pl.pallas_call(
        functools.partial(_na_kernel, rows=rows),
        grid=(b, ng),
        in_specs=[pl.BlockSpec((1, qt, NA_WIDTH), lambda bb, g: (bb, g, 0)),
                  union_spec, union_spec, ctx_spec, ctx_spec, _resident(bias.shape)],
        out_specs=pl.BlockSpec((1, qt, NA_WIDTH), lambda bb, g: (bb, g, 0)),
        out_shape=jax.ShapeDtypeStruct((b, ng * qt, NA_WIDTH), BF16),
        scratch_shapes=[pltpu.VMEM((NA_QROWS, NA_HEADS * GRID_W, n_keys), F32),
                        pltpu.VMEM((NA_QROWS, NA_HEADS * GRID_W, n_keys), BF16)],
        compiler_params=_params(2),
        name="neighborhood_attn",
    )(q, k, v, k, v, bias)


def _na_bias_pairs(rpb):
    cols = np.arange(GRID_W)
    col_start = np.clip(cols - NA_COLS // 2, 0, GRID_W - NA_COLS)
    kc = np.arange(GRID_W)
    in_win = (kc[None, :] >= col_start[:, None]) & (kc[None, :] < col_start[:, None] + NA_COLS)
    col_idx = np.clip(kc[None, :] - cols[:, None] + (NA_COLS - 1), 0, 2 * NA_COLS - 2)
    tab = jnp.where(in_win[None, None], rpb.astype(F32)[:, :, col_idx], NEG)
    return jnp.concatenate([tab[:, :-1], tab[:, 1:]], axis=-1)


def _s5_prep_kernel(are_ref, aim_ref, ldt_ref, bre_ref, bim_ref, lbr_ref, lbi_ref, bbr_ref, bbi_ref):
    are, aim = are_ref[...], aim_ref[...]
    dt = jnp.exp(ldt_ref[...])
    mag = jnp.exp(are * dt)
    lbr, lbi = mag * jnp.cos(aim * dt), mag * jnp.sin(aim * dt)
    lbr_ref[...] = lbr
    lbi_ref[...] = lbi
    nr, ni = lbr - 1.0, lbi
    den = are * are + aim * aim
    cr, ci = (nr * are + ni * aim) / den, (ni * are - nr * aim) / den
    br, bi = bre_ref[...], bim_ref[...]
    bbr_ref[...] = cr * br - ci * bi
    bbi_ref[...] = cr * bi + ci * br


def _s5_prep(a_re, a_im, log_dt, b_re, b_im):
    nd, g, p = a_re.shape
    i = b_re.shape[-1]
    n = nd * g * p
    col = lambda z: z.reshape(n, 1)
    ldt = jnp.broadcast_to(log_dt[:, :, None], (nd, g, p))
    whole = lambda w: pl.BlockSpec((n, w), lambda: (0, 0))
    lbr, lbi, bbr, bbi = pl.pallas_call(
        _s5_prep_kernel,
        in_specs=[whole(1)] * 3 + [whole(i)] * 2,
        out_specs=[whole(1)] * 2 + [whole(i)] * 2,
        out_shape=[jax.ShapeDtypeStruct((n, 1), F32)] * 2 + [jax.ShapeDtypeStruct((n, i), F32)] * 2,
        name="s5_prep",
    )(col(a_re), col(a_im), col(ldt), b_re.reshape(n, i), b_im.reshape(n, i))
    return lbr.reshape(nd, g, p), lbi.reshape(nd, g, p), bbr.reshape(nd, g, p, i), bbi.reshape(nd, g, p, i)


def _s5_block_operands(lbr, lbi, bbr, bbi, c_re, c_im):
    nd, g, p = lbr.shape
    i = bbr.shape[-1]
    nb, gb = SSM_BLOCKS, SSM_BLOCK_GROUPS
    eye = jnp.eye(gb, dtype=F32)

    def in_map(z):
        z = z.reshape(nd, nb, gb, p, i).transpose(0, 1, 2, 4, 3)
        return (z[:, :, :, :, None, :] * eye[None, None, :, None, :, None]).reshape(nd, nb, gb * i, gb * p)

    def out_map(z):
        z = z.reshape(nd, nb, gb, i, p).transpose(0, 1, 2, 4, 3)
        return (z[:, :, :, :, None, :] * eye[None, None, :, None, :, None]).reshape(nd, nb, gb * p, gb * i)

    lam = jnp.stack([lbr.reshape(nd, nb, gb * p), lbi.reshape(nd, nb, gb * p)], axis=2)
    w_in = jnp.concatenate([in_map(bbr), in_map(bbi)], axis=-1).astype(BF16)
    w_out = jnp.concatenate([out_map(c_re.astype(F32)), -out_map(c_im.astype(F32))], axis=2).astype(BF16)
    return lam, w_in, w_out


def _s5_kernel(u_ref, lam_ref, win_ref, wout_ref, y_ref, st_ref, carry_ref, *, nb, tt):
    d = pl.program_id(0)
    ns = SSM_BLOCK_STATE

    @pl.when(pl.program_id(1) == 0)
    def _():
        carry_ref[...] = jnp.zeros_like(carry_ref)

    for j in range(SSM_BLOCKS):
        st_ref[j] = _dot(u_ref[j].astype(BF16), win_ref[0, j])

    lam = lam_ref[0]
    lr = [jnp.broadcast_to(lam[j, 0:1, :], (nb, ns)) for j in range(SSM_BLOCKS)]
    li = [jnp.broadcast_to(lam[j, 1:2, :], (nb, ns)) for j in range(SSM_BLOCKS)]

    def step(t, carry):
        r0 = pl.multiple_of(jnp.where(d == 0, t, tt - 1 - t) * nb, nb)
        new = []
        for j in range(SSM_BLOCKS):
            sr, si = carry[2 * j], carry[2 * j + 1]
            nr = lr[j] * sr - li[j] * si + st_ref[j, pl.ds(r0, nb), 0:ns]
            ni = lr[j] * si + li[j] * sr + st_ref[j, pl.ds(r0, nb), ns:2 * ns]
            st_ref[j, pl.ds(r0, nb), 0:ns] = nr
            st_ref[j, pl.ds(r0, nb), ns:2 * ns] = ni
            new += [nr, ni]
        return tuple(new)

    init = tuple(carry_ref[k] for k in range(2 * SSM_BLOCKS))
    last = lax.fori_loop(0, tt, step, init, unroll=2)
    for k in range(2 * SSM_BLOCKS):
        carry_ref[k] = last[k]
    for j in range(SSM_BLOCKS):
        y_ref[0, j] = _dot(st_ref[j].astype(BF16), wout_ref[0, j])


def _s5_scan(u2, lam, w_in, w_out, *, nb, n_lat_tiles, n_ctx_tiles):
    _, rows_total, _ = u2.shape
    tt = SSM_TILE
    rows = tt * nb
    n_tiles = n_lat_tiles + n_ctx_tiles
    ns2 = 2 * SSM_BLOCK_STATE

    def tile(d, k):
        fwd = jnp.where(k < n_ctx_tiles, n_lat_tiles + k, k - n_ctx_tiles)
        bwd = jnp.where(k < n_ctx_tiles, n_tiles - 1 - k, n_lat_tiles - 1 - (k - n_ctx_tiles))
        return jnp.where(d == 0, fwd, bwd)

    def par(*tail):
        return pl.BlockSpec((1,) + tail, lambda d, k: (d,) + (0,) * len(tail))

    return pl.pallas_call(
        functools.partial(_s5_kernel, nb=nb, tt=tt),
        grid=(2, n_tiles),
        in_specs=[pl.BlockSpec((SSM_BLOCKS, rows, LANES), lambda d, k: (0, tile(d, k), 0)),
                  par(SSM_BLOCKS, 2, SSM_BLOCK_STATE), par(SSM_BLOCKS, LANES, ns2), par(SSM_BLOCKS, ns2, LANES)],
        out_specs=pl.BlockSpec((1, SSM_BLOCKS, rows, LANES), lambda d, k: (d, 0, tile(d, k), 0)),
        out_shape=jax.ShapeDtypeStruct((2, SSM_BLOCKS, rows_total, LANES), F32),
        scratch_shapes=[pltpu.VMEM((SSM_BLOCKS, rows, ns2), F32),
                        pltpu.VMEM((2 * SSM_BLOCKS, nb, SSM_BLOCK_STATE), F32)],
        compiler_params=_params(2),
        name="s5_scan",
    )(u2, lam, w_in, w_out)


def _merge_kernel(x_ref, mod_ref, g_ref, ysf_ref, ysb_ref, u_ref, ya_ref, yn_ref, d_ref,
                  wgt_ref, wglu_ref, wps_ref, wpa_ref, wpn_ref, wo_ref, o_ref):
    d = x_ref.shape[2]
    tm = x_ref.shape[1]
    x = x_ref[0]
    h = _mod_norm(x, g_ref[1:2, :], mod_ref[0, 3:4, :], mod_ref[0, 4:5, :]).astype(BF16)
    gate = lambda n: jax.nn.sigmoid(_dot(h, wgt_ref[:, n * d:(n + 1) * d]))
    mine = pl.ds(pl.program_id(1), tm, stride=u_ref.shape[1] // tm)
    y = jnp.concatenate([ysf_ref[0, blk, mine, :] + ysb_ref[0, blk, mine, :] for blk in range(SSM_BLOCKS)], axis=1)
    y = y + d_ref[...] * jnp.concatenate([u_ref[blk, mine, :] for blk in range(SSM_BLOCKS)], axis=1)
    t = 0.5 * y * (1.0 + jnp.tanh(math.sqrt(2.0 / math.pi) * (y + 0.044715 * (y * y * y))))
    glu = (t * jax.nn.sigmoid(_dot(t.astype(BF16), wglu_ref[...]))).astype(BF16)
    m = (gate(0) * _dot(glu, wps_ref[...]) + gate(1) * _dot(ya_ref[0], wpa_ref[...])
         + gate(2) * _dot(yn_ref[0], wpn_ref[...]))
    o_ref[0] = x + mod_ref[0, 5:6, :] * _dot(m.astype(BF16), wo_ref[...])


def _merge(xs, mod, g, ys, u2, ya, yn, ssm_d, wgt, wglu, wps, wpa, wpn, wo, *, li, n_tok, n_lat_tiles, ctx_row):
    b, _, d = xs.shape
    tm = TOKEN_TILE
    return pl.pallas_call(
        _merge_kernel,
        grid=(n_tok // tm, b),
        in_specs=[_tok_spec(d, tm), _mod_spec(d, n_lat_tiles, ctx_row), _resident(g.shape),
                  _slab_spec(tm, b, (0,)), _slab_spec(tm, b, (1,)), _slab_spec(tm, b),
                  _tok_spec(GQA_WIDTH, tm), _tok_spec(NA_WIDTH, tm), _resident((1, SSM_WIDTH)),
                  _layer_resident(wgt, li), _layer_resident(wglu, li), _layer_resident(wps, li),
                  _layer_resident(wpa, li), _layer_resident(wpn, li), _layer_resident(wo, li)],
        out_specs=_tok_spec(d, tm),
        out_shape=jax.ShapeDtypeStruct((b, n_tok, d), F32),
        compiler_params=_params(2),
        name="mixer_merge",
    )(xs, mod, g, ys, ys, u2, ya, yn, ssm_d.reshape(1, SSM_WIDTH), wgt, wglu, wps, wpa, wpn, wo)


def _rope_tables(l, c_len):
    t = jnp.arange(l)
    pos = jnp.stack([t // GRID_W, t % GRID_W], axis=-1).astype(F32)
    half = HEAD_DIM // 2
    inv = ROPE_THETA ** (-jnp.arange(0, half, 2, dtype=F32) / half)
    ang = pos[:, :, None] * inv
    cos, sin = jnp.cos(ang), jnp.sin(ang)
    zero = jnp.zeros_like(sin[:, 0])
    cos_h = jnp.concatenate([cos[:, 0], cos[:, 0], cos[:, 1], cos[:, 1]], axis=-1)
    sa_h = jnp.concatenate([zero, sin[:, 0], zero, sin[:, 1]], axis=-1)
    sb_h = jnp.concatenate([-sin[:, 0], zero, -sin[:, 1], zero], axis=-1)

    def full(tab, ctx_fill):
        tab = jnp.concatenate([tab, tab], axis=-1)
        return jnp.concatenate([tab, jnp.full((c_len, LANES), ctx_fill, F32)], axis=0)

    return full(cos_h, 1.0), full(sa_h, 0.0), full(sb_h, 0.0)


def kernel(x, c, ctx, c_ctx, w_ada, b_ada, norm_g, ffn1_wg, ffn1_wu, ffn1_wd, ffn2_wg, ffn2_wu, ffn2_wd, w_in,
           ssm_a_re, ssm_a_im, ssm_log_dt, ssm_b_re, ssm_b_im, ssm_c_re, ssm_c_im, ssm_d, ssm_w_glu, gqa_sink,
           na_rpb, w_p_ssm, w_p_gqa, w_p_na, w_out, final_g):
    b, l, d = x.shape
    c_len = ctx.shape[1]
    s = l + c_len
    depth = w_ada.shape[0]
    tm = TOKEN_TILE
    assert b + 1 <= MOD_ROWS and l % tm == 0 and c_len % tm == 0 and l % (GRID_W * NA_ROWS) == 0
    assert c_len % Q_BLOCK == 0 and l % c_len == 0 and b % SUBLANES == 0
    assert l % SSM_TILE == 0 and c_len % SSM_TILE == 0
    n_lat_tiles = l // tm

    cin = jnp.zeros((MOD_ROWS, d), F32).at[:b].set(c).at[b].set(c_ctx)
    mods = _ada(cin, w_ada, b_ada).reshape(depth, MOD_ROWS, N_MOD, d)
    cos_t, sa_t, sb_t = _rope_tables(l, c_len)

    perm = np.asarray(GQA_HEAD_PERM)
    q0 = SSM_WIDTH
    w_q = w_in[:, :, q0:q0 + GQA_WIDTH].reshape(depth, d, GQA_HEADS, HEAD_DIM)[:, :, perm]
    w_in_p = jnp.concatenate([w_in[:, :, :q0], w_q.reshape(depth, d, GQA_WIDTH),
                              w_in[:, :, q0 + GQA_WIDTH:MIXER_COLS]], axis=-1).astype(BF16)
    w_gt = w_in[:, :, MIXER_COLS:].astype(BF16)
    w_pa = w_p_gqa.reshape(depth, GQA_HEADS, HEAD_DIM, d)[:, perm].reshape(depth, GQA_WIDTH, d).astype(BF16)
    bf = lambda z: z.astype(BF16)
    f1 = (bf(ffn1_wg), bf(ffn1_wu), bf(ffn1_wd))
    f2 = (bf(ffn2_wg), bf(ffn2_wu), bf(ffn2_wd))
    w_glu, w_ps, w_pn, w_o = bf(ssm_w_glu), bf(w_p_ssm), bf(w_p_na), bf(w_out)

    tile_kw = dict(n_lat_tiles=n_lat_tiles, ctx_row=b)
    xs = x
    for li in range(depth):
        ctx_out = li < depth - 1
        n_tok = s if ctx_out else l
        mod, g = mods[li], norm_g[li]
        xs = _ffn(xs, mod, g, *f1, li=li, k0=0, gi=0, n_tok=s, ctx=ctx if li == 0 else None, **tile_kw)
        u2, q, k, v, nq, nk, nv = _inproj(xs, mod, g, w_in_p, cos_t, sa_t, sb_t, li=li, **tile_kw)
        lam, s_in, s_out = _s5_block_operands(
            *_s5_prep(ssm_a_re[li], ssm_a_im[li], ssm_log_dt[li], ssm_b_re[li], ssm_b_im[li]),
            ssm_c_re[li], ssm_c_im[li])
        ys = _s5_scan(u2, lam, s_in, s_out, nb=b, n_lat_tiles=l // SSM_TILE, n_ctx_tiles=c_len // SSM_TILE)
        ya = _gqa(gqa_sink[li], q, k, v, n_lat=l // Q_BLOCK, n_ctx_blocks=c_len // Q_BLOCK, ctx_out=ctx_out,
                  c_len=c_len)
        yn = _na(nq, nk, nv, _na_bias_pairs(na_rpb[li]), rows=l // GRID_W, n_ctx_rows=c_len // GRID_W,
                 ctx_out=ctx_out, c_len=c_len)
        xs = _merge(xs, mod, g, ys, u2, ya, yn, ssm_d[li], w_gt, w_glu, w_ps, w_pa, w_pn, w_o,
                    li=li, n_tok=n_tok, **tile_kw)
        xs = _ffn(xs, mod, g, *f2, li=li, k0=6, gi=2, n_tok=n_tok, final_g=None if ctx_out else final_g, **tile_kw)
    return xs
```

```python
import functools
import math

import jax
import jax.numpy as jnp
import numpy as np
from jax import lax
from jax.experimental import pallas as pl
from jax.experimental.pallas import tpu as pltpu

F32 = jnp.float32
BF16 = jnp.bfloat16

EPS = 1e-6
NEG = -1e30
N_MOD = 9
GRID_W = 64
HEAD_DIM = 64
GQA_HEADS = 8
GQA_KV_HEADS = 2
NA_HEADS = 8
NA_ROWS = 8
NA_COLS = 16
Q_BLOCK = 128
ROPE_THETA = 10000.0
SSM_GROUP = 16
SSM_GROUPS = 24
SSM_STATE = 64
SSM_WIDTH = SSM_GROUP * SSM_GROUPS
GQA_WIDTH = GQA_HEADS * HEAD_DIM
GQA_KV_WIDTH = GQA_KV_HEADS * HEAD_DIM
NA_WIDTH = NA_HEADS * HEAD_DIM
N_BRANCH = 3
MIXER_COLS = SSM_WIDTH + GQA_WIDTH + 2 * GQA_KV_WIDTH + 3 * NA_WIDTH

LANES = 128
SUBLANES = 8
MOD_ROWS = 16
TOKEN_TILE = 256
TOKEN_BATCH = 2
FFN_CHUNK = 256
GQA_QBLOCKS = 2
NA_QROWS = 4
NA_UNION = NA_QROWS + NA_ROWS - 1
SSM_TILE = 64
SSM_BLOCK_GROUPS = LANES // SSM_GROUP
SSM_BLOCKS = SSM_GROUPS // SSM_BLOCK_GROUPS
SSM_BLOCK_STATE = SSM_BLOCK_GROUPS * SSM_STATE
VMEM_LIMIT = 56 * 1024 * 1024
GQA_HEAD_PERM = (0, 4, 1, 5, 2, 6, 3, 7)


def _params(n_axes):
    return pltpu.CompilerParams(dimension_semantics=("arbitrary",) * n_axes, vmem_limit_bytes=VMEM_LIMIT)


def _resident(shape):
    return pl.BlockSpec(shape, lambda *_: (0,) * len(shape), pipeline_mode=pl.Buffered(1))


def _layer_resident(stack, li):
    nd = stack.ndim
    return pl.BlockSpec((None,) + stack.shape[1:], lambda *_: (li,) + (0,) * (nd - 1),
                        pipeline_mode=pl.Buffered(1))


def _dot(a, b):
    return jnp.dot(a, b, preferred_element_type=F32)


def _dot_t(a, b):
    return lax.dot_general(a, b, (((1,), (1,)), ((), ())), preferred_element_type=F32)


def _rmsnorm(x, g):
    return x * lax.rsqrt(jnp.mean(x * x, axis=-1, keepdims=True) + EPS) * g


def _mod_norm(x, g, shift, scale):
    return _rmsnorm(x, g) * (1.0 + scale) + shift


def _ada_kernel(c_ref, w_ref, b_ref, o_ref):
    s = c_ref[...]
    s = s * jax.nn.sigmoid(s)
    o_ref[0] = _dot(s.astype(BF16), w_ref[0].astype(BF16)) + b_ref[0]


def _ada(cin, w_ada, b_ada):
    depth, d, n = w_ada.shape
    tn = n // 4
    return pl.pallas_call(
        _ada_kernel,
        grid=(depth, n // tn),
        in_specs=[
            pl.BlockSpec((MOD_ROWS, d), lambda l, j: (0, 0)),
            pl.BlockSpec((1, d, tn), lambda l, j: (l, 0, j)),
            pl.BlockSpec((1, 1, tn), lambda l, j: (l, 0, j)),
        ],
        out_specs=pl.BlockSpec((1, MOD_ROWS, tn), lambda l, j: (l, 0, j)),
        out_shape=jax.ShapeDtypeStruct((depth, MOD_ROWS, n), F32),
        compiler_params=_params(2),
        name="ada_mod",
    )(cin, w_ada, b_ada.reshape(depth, 1, n))


def _tok_spec(width, tm):
    return pl.BlockSpec((TOKEN_BATCH, tm, width), lambda j, b: (b, j, 0))


def _rows(val, sub, tm):
    return val[sub * tm:(sub + 1) * tm]


def _slab_rows(sub, tm, nb):
    return pl.ds(pl.program_id(1) * TOKEN_BATCH + sub, tm, stride=nb)


def _slab_spec(tm, nb, lead=()):
    n = len(lead)
    return pl.BlockSpec((1,) * n + (SSM_BLOCKS, tm * nb, LANES), lambda j, b: lead + (0, j, 0))


def _mod_spec(d, n_lat_tiles, ctx_group):
    return pl.BlockSpec((TOKEN_BATCH, N_MOD, d), lambda j, b: (jnp.where(j >= n_lat_tiles, ctx_group, b), 0, 0))


def _stacked_mod_norm(xs, g, mod_ref, shift_row):
    return jnp.concatenate(
        [_mod_norm(x, g, mod_ref[sub, shift_row:shift_row + 1, :], mod_ref[sub, shift_row + 1:shift_row + 2, :])
         for sub, x in enumerate(xs)], axis=0).astype(BF16)


def _ffn_kernel(*refs, k0, gi, final, n_lat_tiles, split_in):
    refs = list(refs)
    x_ref = refs.pop(0)
    ctx_ref = refs.pop(0) if split_in else None
    mod_ref, g_ref, wg_ref, wu_ref, wd_ref = refs[:5]
    fg_ref = refs[5] if final else None
    o_ref, t_ref = refs[-2:]
    tm = x_ref.shape[1]
    xs = [x_ref[sub] for sub in range(TOKEN_BATCH)]
    if split_in:
        is_ctx = pl.program_id(0) >= n_lat_tiles
        xs = [jnp.where(is_ctx, ctx_ref[sub], x) for sub, x in enumerate(xs)]
    h = _stacked_mod_norm(xs, g_ref[gi:gi + 1, :], mod_ref, k0)
    f = wg_ref.shape[1]
    for c0 in range(0, f, FFN_CHUNK):
        a = _dot(h, wg_ref[:, c0:c0 + FFN_CHUNK])
        u = _dot(h, wu_ref[:, c0:c0 + FFN_CHUNK])
        t_ref[:, c0:c0 + FFN_CHUNK] = (a * jax.nn.sigmoid(a) * u).astype(BF16)
    y = _dot(t_ref[...], wd_ref[...])
    for sub, x in enumerate(xs):
        out = x + (0.5 * mod_ref[sub, k0 + 2:k0 + 3, :]) * _rows(y, sub, tm)
        if final:
            out = _rmsnorm(out, fg_ref[...])
        o_ref[sub] = out


def _ffn(xs, mod, g, wg, wu, wd, *, li, k0, gi, n_tok, n_lat_tiles, ctx_group, final_g=None, ctx=None):
    b, _, d = xs.shape
    f = wg.shape[2]
    tm = TOKEN_TILE
    final = final_g is not None
    split_in = ctx is not None
    if split_in:
        in_specs = [pl.BlockSpec((TOKEN_BATCH, tm, d), lambda j, bb: (bb, jnp.minimum(j, n_lat_tiles - 1), 0)),
                    pl.BlockSpec((TOKEN_BATCH, tm, d), lambda j, bb: (bb, jnp.maximum(j - n_lat_tiles, 0), 0))]
        args = [xs, ctx]
    else:
        in_specs = [_tok_spec(d, tm)]
        args = [xs]
    in_specs += [_mod_spec(d, n_lat_tiles, ctx_group), _resident(g.shape),
                 _layer_resident(wg, li), _layer_resident(wu, li), _layer_resident(wd, li)]
    args += [mod, g, wg, wu, wd]
    if final:
        in_specs.append(_resident((1, d)))
        args.append(final_g.reshape(1, d))
    return pl.pallas_call(
        functools.partial(_ffn_kernel, k0=k0, gi=gi, final=final, n_lat_tiles=n_lat_tiles, split_in=split_in),
        grid=(n_tok // tm, b // TOKEN_BATCH),
        in_specs=in_specs,
        out_specs=_tok_spec(d, tm),
        out_shape=jax.ShapeDtypeStruct((b, n_tok, d), F32),
        scratch_shapes=[pltpu.VMEM((TOKEN_BATCH * tm, f), BF16)],
        compiler_params=_params(2),
        name="ffn_half_final" if final else "ffn_half",
    )(*args)


def _inproj_kernel(x_ref, mod_ref, g_ref, w_ref, cos_ref, sa_ref, sb_ref,
                   u_ref, q_ref, k_ref, v_ref, nq_ref, nk_ref, nv_ref):
    tm = x_ref.shape[1]
    nb = u_ref.shape[1] // tm
    h = _stacked_mod_norm([x_ref[sub] for sub in range(TOKEN_BATCH)], g_ref[1:2, :], mod_ref, 3)
    cos, sa, sb = (jnp.concatenate([t[...]] * TOKEN_BATCH, axis=0) for t in (cos_ref, sa_ref, sb_ref))
    scale = HEAD_DIM ** -0.5

    def mm(c0, n):
        return _dot(h, w_ref[:, c0:c0 + n])

    def rope(z):
        return z * cos + pltpu.roll(z, 16, 1) * sa + pltpu.roll(z, LANES - 16, 1) * sb

    def put(ref, val, lanes=slice(None)):
        for sub in range(TOKEN_BATCH):
            ref[sub, :, lanes] = _rows(val, sub, tm)

    c0 = SSM_WIDTH + GQA_WIDTH + 2 * GQA_KV_WIDTH
    z = mm(0, c0)
    col = lambda n: z[:, n * LANES:(n + 1) * LANES]
    n = 0
    for blk in range(SSM_BLOCKS):
        for sub in range(TOKEN_BATCH):
            u_ref[blk, _slab_rows(sub, tm, nb), :] = _rows(col(n), sub, tm)
        n += 1
    for j in range(GQA_WIDTH // LANES):
        put(q_ref, (rope(col(n)) * scale).astype(BF16), slice(j * LANES, (j + 1) * LANES))
        n += 1
    put(k_ref, rope(col(n)).astype(BF16))
    put(v_ref, col(n + 1).astype(BF16))
    put(nq_ref, (mm(c0, NA_WIDTH) * scale).astype(BF16))
    c0 += NA_WIDTH
    put(nk_ref, mm(c0, NA_WIDTH).astype(BF16))
    c0 += NA_WIDTH
    put(nv_ref, mm(c0, NA_WIDTH).astype(BF16))


def _inproj(xs, mod, g, w_in, cos_t, sa_t, sb_t, *, li, n_lat_tiles, ctx_group):
    b, s, d = xs.shape
    tm = TOKEN_TILE
    assert w_in.shape[2] == MIXER_COLS
    widths = (GQA_WIDTH, GQA_KV_WIDTH, GQA_KV_WIDTH, NA_WIDTH, NA_WIDTH, NA_WIDTH)
    dtypes = (BF16,) * len(widths)
    rope_spec = pl.BlockSpec((tm, LANES), lambda j, bb: (j, 0))
    return pl.pallas_call(
        _inproj_kernel,
        grid=(s // tm, b // TOKEN_BATCH),
        in_specs=[_tok_spec(d, tm), _mod_spec(d, n_lat_tiles, ctx_group), _resident(g.shape),
                  _layer_resident(w_in, li), rope_spec, rope_spec, rope_spec],
        out_specs=[_slab_spec(tm, b)] + [_tok_spec(w, tm) for w in widths],
        out_shape=[jax.ShapeDtypeStruct((SSM_BLOCKS, s * b, LANES), F32)]
                  + [jax.ShapeDtypeStruct((b, s, w), dt) for w, dt in zip(widths, dtypes)],
        compiler_params=_params(2),
        name="mixer_in_proj",
    )(xs, mod, g, w_in, cos_t, sa_t, sb_t)


def _softmax_rows(s_ref, p_ref, extra=None):
    s = s_ref[...]
    m = jnp.max(s, axis=1, keepdims=True)
    if extra is not None:
        m = jnp.maximum(m, extra)
    e = jnp.exp(s - m)
    den = jnp.sum(e, axis=1, keepdims=True)
    if extra is not None:
        den = den + jnp.exp(extra - m)
    p_ref[...] = e.astype(BF16)
    return 1.0 / den


def _head_half(qp, low, half):
    return jnp.where(low if half == 0 else jnp.logical_not(low), qp, jnp.zeros_like(qp))


def _gqa_kernel(sink_ref, q_ref, *refs, n_lat):
    nkb = GQA_QBLOCKS + 2
    k_refs, kx_ref = refs[:nkb], refs[nkb]
    v_refs, vx_ref = refs[nkb + 1:2 * nkb + 1], refs[2 * nkb + 1]
    o_ref, s_ref, p_ref = refs[2 * nkb + 2:]
    n_keys = s_ref.shape[2]
    far = 8 * n_keys
    row = lax.broadcasted_iota(jnp.int32, (Q_BLOCK, n_keys), 0)
    col = lax.broadcasted_iota(jnp.int32, (Q_BLOCK, n_keys), 1)
    low = lax.broadcasted_iota(jnp.int32, (Q_BLOCK, LANES), 1) < HEAD_DIM
    sinks = jnp.concatenate([jnp.full((Q_BLOCK, 1), sink_ref[h], F32) for h in GQA_HEAD_PERM], axis=0)
    n_pairs = GQA_WIDTH // LANES
    for a in range(GQA_QBLOCKS):
        i = pl.program_id(1) * GQA_QBLOCKS + a
        is_lat = i < n_lat
        off_p = jnp.where(jnp.logical_and(is_lat, i > 0), 0, far)
        off_c = jnp.where(is_lat, 0, far)
        off_n = jnp.where(jnp.logical_and(is_lat, i < n_lat - 1), 0, far)
        ok = ((col >= row + off_p) & (col < Q_BLOCK)
              | (col >= Q_BLOCK + off_c) & (col < 2 * Q_BLOCK)
              | (col >= 2 * Q_BLOCK) & (col <= 2 * Q_BLOCK + row - off_n)
              | (col >= 3 * Q_BLOCK))
        keys = jnp.concatenate([k_refs[a][0], k_refs[a + 1][0], k_refs[a + 2][0], kx_ref[0]], axis=0)
        vals = jnp.concatenate([v_refs[a][0], v_refs[a + 1][0], v_refs[a + 2][0], vx_ref[0]], axis=0)
        qrows = slice(a * Q_BLOCK, (a + 1) * Q_BLOCK)
        for j in range(n_pairs):
            qp = q_ref[0, qrows, j * LANES:(j + 1) * LANES]
            for half in range(2):
                r0 = (2 * j + half) * Q_BLOCK
                s_ref[a, r0:r0 + Q_BLOCK, :] = jnp.where(ok, _dot_t(_head_half(qp, low, half), keys), NEG)
        inv = _softmax_rows(s_ref.at[a], p_ref.at[a], sinks)
        for j in range(n_pairs):
            outs = []
            for half in range(2):
                r0 = (2 * j + half) * Q_BLOCK
                outs.append(_dot(p_ref[a, r0:r0 + Q_BLOCK, :], vals) * inv[r0:r0 + Q_BLOCK])
            o_ref[0, qrows, j * LANES:(j + 1) * LANES] = jnp.where(low, outs[0], outs[1]).astype(BF16)


def _gqa(sink, q, k, v, *, n_lat, n_ctx_blocks, ctx_out, c_len):
    b = q.shape[0]
    nq = n_lat + (n_ctx_blocks if ctx_out else 0)
    assert n_lat % GQA_QBLOCKS == 0 and n_ctx_blocks % GQA_QBLOCKS == 0
    ctx_idx = n_lat * Q_BLOCK // c_len
    n_keys = 3 * Q_BLOCK + c_len
    qt = GQA_QBLOCKS * Q_BLOCK

    def kv_spec(t):
        return pl.BlockSpec((1, Q_BLOCK, GQA_KV_WIDTH),
                            lambda bb, g: (bb, jnp.clip(g * GQA_QBLOCKS - 1 + t, 0, n_lat - 1), 0))

    kv = [kv_spec(t) for t in range(GQA_QBLOCKS + 2)]
    kv.append(pl.BlockSpec((1, c_len, GQA_KV_WIDTH), lambda bb, g: (bb, ctx_idx, 0)))
    return pl.pallas_call(
        functools.partial(_gqa_kernel, n_lat=n_lat),
        grid=(b, nq // GQA_QBLOCKS),
        in_specs=[pl.BlockSpec(memory_space=pltpu.SMEM),
                  pl.BlockSpec((1, qt, GQA_WIDTH), lambda bb, g: (bb, g, 0))] + kv + kv,
        out_specs=pl.BlockSpec((1, qt, GQA_WIDTH), lambda bb, g: (bb, g, 0)),
        out_shape=jax.ShapeDtypeStruct((b, nq * Q_BLOCK, GQA_WIDTH), BF16),
        scratch_shapes=[pltpu.VMEM((GQA_QBLOCKS, GQA_HEADS * Q_BLOCK, n_keys), F32),
                        pltpu.VMEM((GQA_QBLOCKS, GQA_HEADS * Q_BLOCK, n_keys), BF16)],
        compiler_params=_params(2),
        name="window_gqa",
    )(sink, q, *([k] * (GQA_QBLOCKS + 3)), *([v] * (GQA_QBLOCKS + 3)))


def _na_band_start(r, rows):
    return jnp.clip(r - NA_ROWS // 2, 0, rows - NA_ROWS)


def _na_union_start(g, rows):
    return jnp.minimum(_na_band_start(g * NA_QROWS, rows), rows - NA_UNION)


def _na_kernel(q_ref, kb_ref, vb_ref, kx_ref, vx_ref, bias_ref, o_ref, s_ref, p_ref, *, rows):
    g = pl.program_id(1)
    n_loc = NA_ROWS * GRID_W
    low = lax.broadcasted_iota(jnp.int32, (GRID_W, LANES), 1) < HEAD_DIM
    n_pairs = NA_WIDTH // LANES
    u0 = _na_union_start(g, rows)
    for rho in range(NA_QROWS):
        r = g * NA_QROWS + rho
        is_ctx = r >= rows
        band = _na_band_start(r, rows)
        off = pl.multiple_of(jnp.where(is_ctx, 0, band - u0) * GRID_W, GRID_W)
        dr0 = jnp.where(is_ctx, 0, (NA_ROWS - 1) - (r - band))
        pen = jnp.where(is_ctx, NEG, 0.0).astype(F32)
        qrows = slice(rho * GRID_W, (rho + 1) * GRID_W)
        for j in range(n_pairs):
            sl = slice(j * LANES, (j + 1) * LANES)
            qp = q_ref[0, qrows, sl]
            keys = jnp.concatenate([kb_ref[0, pl.ds(off, n_loc), sl], kx_ref[0, :, sl]], axis=0)
            s2 = _dot_t(jnp.concatenate([_head_half(qp, low, 0), _head_half(qp, low, 1)], axis=0), keys)
            for half in range(2):
                h = 2 * j + half
                r0 = h * GRID_W
                s = s2[half * GRID_W:(half + 1) * GRID_W]
                bias = jnp.concatenate([bias_ref[h, dr0 + i] for i in range(0, NA_ROWS, 2)], axis=1)
                s_ref[rho, r0:r0 + GRID_W, 0:n_loc] = s[:, 0:n_loc] + (bias + pen)
                s_ref[rho, r0:r0 + GRID_W, n_loc:] = s[:, n_loc:]
        inv = _softmax_rows(s_ref.at[rho], p_ref.at[rho])
        for j in range(n_pairs):
            sl = slice(j * LANES, (j + 1) * LANES)
            vals = jnp.concatenate([vb_ref[0, pl.ds(off, n_loc), sl], vx_ref[0, :, sl]], axis=0)
            r0 = 2 * j * GRID_W
            o2 = _dot(p_ref[rho, r0:r0 + 2 * GRID_W, :], vals) * inv[r0:r0 + 2 * GRID_W]
            o_ref[0, qrows, sl] = jnp.where(low, o2[:GRID_W], o2[GRID_W:]).astype(BF16)


def _na(q, k, v, bias, *, rows, n_ctx_groups, ctx_out, c_len):
    b = q.shape[0]
    assert rows % NA_QROWS == 0 and n_ctx_groups % NA_QROWS == 0 and rows >= NA_UNION
    ng = (rows + (n_ctx_groups if ctx_out else 0)) // NA_QROWS
    qt = NA_QROWS * GRID_W
    ctx_idx = rows * GRID_W // c_len
    n_keys = NA_ROWS * GRID_W + c_len
    union_spec = pl.BlockSpec((pl.Element(1), pl.Element(NA_UNION * GRID_W), pl.Element(NA_WIDTH)),
                              lambda bb, g: (bb, _na_union_start(g, rows) * GRID_W, 0))
    ctx_spec = pl.BlockSpec((1, c_len, NA_WIDTH), lambda bb, g: (bb, ctx_idx, 0))
    return pl.pallas_call(
        functools.partial(_na_kernel, rows=rows),
        grid=(b, ng),
        in_specs=[pl.BlockSpec((1, qt, NA_WIDTH), lambda bb, g: (bb, g, 0)),
                  union_spec, union_spec, ctx_spec, ctx_spec, _resident(bias.shape)],
        out_specs=pl.BlockSpec((1, qt, NA_WIDTH), lambda bb, g: (bb, g, 0)),
        out_shape=jax.ShapeDtypeStruct((b, ng * qt, NA_WIDTH), BF16),
        scratch_shapes=[pltpu.VMEM((NA_QROWS, NA_HEADS * GRID_W, n_keys), F32),
                        pltpu.VMEM((NA_QROWS, NA_HEADS * GRID_W, n_keys), BF16)],
        compiler_params=_params(2),
        name="neighborhood_attn",
    )(q, k, v, k, v, bias)


def _na_bias_pairs(rpb):
    cols = np.arange(GRID_W)
    col_start = np.clip(cols - NA_COLS // 2, 0, GRID_W - NA_COLS)
    kc = np.arange(GRID_W)
    in_win = (kc[None, :] >= col_start[:, None]) & (kc[None, :] < col_start[:, None] + NA_COLS)
    col_idx = np.clip(kc[None, :] - cols[:, None] + (NA_COLS - 1), 0, 2 * NA_COLS - 2)
    tab = jnp.where(in_win[None, None], rpb.astype(F32)[:, :, col_idx], NEG)
    return jnp.concatenate([tab[:, :-1], tab[:, 1:]], axis=-1)


def _s5_prep_kernel(are_ref, aim_ref, ldt_ref, bre_ref, bim_ref, lbr_ref, lbi_ref, bbr_ref, bbi_ref):
    are, aim = are_ref[...], aim_ref[...]
    dt = jnp.exp(ldt_ref[...])
    mag = jnp.exp(are * dt)
    lbr, lbi = mag * jnp.cos(aim * dt), mag * jnp.sin(aim * dt)
    lbr_ref[...] = lbr
    lbi_ref[...] = lbi
    nr, ni = lbr - 1.0, lbi
    den = are * are + aim * aim
    cr, ci = (nr * are + ni * aim) / den, (ni * are - nr * aim) / den
    br, bi = bre_ref[...], bim_ref[...]
    bbr_ref[...] = cr * br - ci * bi
    bbi_ref[...] = cr * bi + ci * br


def _s5_prep(a_re, a_im, log_dt, b_re, b_im):
    nd, g, p = a_re.shape
    i = b_re.shape[-1]
    n = nd * g * p
    col = lambda z: z.reshape(n, 1)
    ldt = jnp.broadcast_to(log_dt[:, :, None], (nd, g, p))
    whole = lambda w: pl.BlockSpec((n, w), lambda: (0, 0))
    lbr, lbi, bbr, bbi = pl.pallas_call(
        _s5_prep_kernel,
        in_specs=[whole(1)] * 3 + [whole(i)] * 2,
        out_specs=[whole(1)] * 2 + [whole(i)] * 2,
        out_shape=[jax.ShapeDtypeStruct((n, 1), F32)] * 2 + [jax.ShapeDtypeStruct((n, i), F32)] * 2,
        name="s5_prep",
    )(col(a_re), col(a_im), col(ldt), b_re.reshape(n, i), b_im.reshape(n, i))
    return lbr.reshape(nd, g, p), lbi.reshape(nd, g, p), bbr.reshape(nd, g, p, i), bbi.reshape(nd, g, p, i)


def _s5_block_operands(lbr, lbi, bbr, bbi, c_re, c_im):
    nd, g, p = lbr.shape
    i = bbr.shape[-1]
    nb, gb = SSM_BLOCKS, SSM_BLOCK_GROUPS
    eye = jnp.eye(gb, dtype=F32)

    def in_map(z):
        z = z.reshape(nd, nb, gb, p, i).transpose(0, 1, 2, 4, 3)
        return (z[:, :, :, :, None, :] * eye[None, None, :, None, :, None]).reshape(nd, nb, gb * i, gb * p)

    def out_map(z):
        z = z.reshape(nd, nb, gb, i, p).transpose(0, 1, 2, 4, 3)
        return (z[:, :, :, :, None, :] * eye[None, None, :, None, :, None]).reshape(nd, nb, gb * p, gb * i)

    lam = jnp.stack([lbr.reshape(nd, nb, gb * p), lbi.reshape(nd, nb, gb * p)], axis=2)
    w_in = jnp.concatenate([in_map(bbr), in_map(bbi)], axis=-1).astype(BF16)
    w_out = jnp.concatenate([out_map(c_re.astype(F32)), -out_map(c_im.astype(F32))], axis=2).astype(BF16)
    return lam, w_in, w_out


def _s5_kernel(u_ref, lam_ref, win_ref, wout_ref, y_ref, st_ref, carry_ref, *, nb, tt):
    d = pl.program_id(0)
    ns = SSM_BLOCK_STATE

    @pl.when(pl.program_id(1) == 0)
    def _():
        carry_ref[...] = jnp.zeros_like(carry_ref)

    for j in range(SSM_BLOCKS):
        st_ref[j] = _dot(u_ref[j].astype(BF16), win_ref[0, j])

    lam = lam_ref[0]
    lr = [jnp.broadcast_to(lam[j, 0:1, :], (nb, ns)) for j in range(SSM_BLOCKS)]
    li = [jnp.broadcast_to(lam[j, 1:2, :], (nb, ns)) for j in range(SSM_BLOCKS)]

    def step(t, carry):
        r0 = pl.multiple_of(jnp.where(d == 0, t, tt - 1 - t) * nb, nb)
        new = []
        for j in range(SSM_BLOCKS):
            sr, si = carry[2 * j], carry[2 * j + 1]
            nr = lr[j] * sr - li[j] * si + st_ref[j, pl.ds(r0, nb), 0:ns]
            ni = lr[j] * si + li[j] * sr + st_ref[j, pl.ds(r0, nb), ns:2 * ns]
            st_ref[j, pl.ds(r0, nb), 0:ns] = nr
            st_ref[j, pl.ds(r0, nb), ns:2 * ns] = ni
            new += [nr, ni]
        return tuple(new)

    init = tuple(carry_ref[k] for k in range(2 * SSM_BLOCKS))
    last = lax.fori_loop(0, tt, step, init, unroll=2)
    for k in range(2 * SSM_BLOCKS):
        carry_ref[k] = last[k]
    for j in range(SSM_BLOCKS):
        y_ref[0, j] = _dot(st_ref[j].astype(BF16), wout_ref[0, j])


def _s5_scan(u2, lam, w_in, w_out, *, nb, n_lat_tiles, n_ctx_tiles):
    _, rows_total, _ = u2.shape
    tt = SSM_TILE
    rows = tt * nb
    n_tiles = n_lat_tiles + n_ctx_tiles
    ns2 = 2 * SSM_BLOCK_STATE

    def tile(d, k):
        fwd = jnp.where(k < n_ctx_tiles, n_lat_tiles + k, k - n_ctx_tiles)
        bwd = jnp.where(k < n_ctx_tiles, n_tiles - 1 - k, n_lat_tiles - 1 - (k - n_ctx_tiles))
        return jnp.where(d == 0, fwd, bwd)

    def par(*tail):
        return pl.BlockSpec((1,) + tail, lambda d, k: (d,) + (0,) * len(tail))

    return pl.pallas_call(
        functools.partial(_s5_kernel, nb=nb, tt=tt),
        grid=(2, n_tiles),
        in_specs=[pl.BlockSpec((SSM_BLOCKS, rows, LANES), lambda d, k: (0, tile(d, k), 0)),
                  par(SSM_BLOCKS, 2, SSM_BLOCK_STATE), par(SSM_BLOCKS, LANES, ns2), par(SSM_BLOCKS, ns2, LANES)],
        out_specs=pl.BlockSpec((1, SSM_BLOCKS, rows, LANES), lambda d, k: (d, 0, tile(d, k), 0)),
        out_shape=jax.ShapeDtypeStruct((2, SSM_BLOCKS, rows_total, LANES), F32),
        scratch_shapes=[pltpu.VMEM((SSM_BLOCKS, rows, ns2), F32),
                        pltpu.VMEM((2 * SSM_BLOCKS, nb, SSM_BLOCK_STATE), F32)],
        compiler_params=_params(2),
        name="s5_scan",
    )(u2, lam, w_in, w_out)


def _merge_kernel(x_ref, mod_ref, g_ref, ysf_ref, ysb_ref, u_ref, ya_ref, yn_ref, d_ref,
                  wgt_ref, wglu_ref, wps_ref, wpa_ref, wpn_ref, wo_ref, o_ref):
    d = x_ref.shape[2]
    tm = x_ref.shape[1]
    nb = u_ref.shape[1] // tm
    xs = [x_ref[sub] for sub in range(TOKEN_BATCH)]
    h = _stacked_mod_norm(xs, g_ref[1:2, :], mod_ref, 3)
    gate = lambda n: jax.nn.sigmoid(_dot(h, wgt_ref[:, n * d:(n + 1) * d]))

    def from_slabs(load):
        return jnp.concatenate([jnp.concatenate([load(blk, _slab_rows(sub, tm, nb)) for blk in range(SSM_BLOCKS)],
                                                axis=1) for sub in range(TOKEN_BATCH)], axis=0)

    y = from_slabs(lambda blk, rows: ysf_ref[0, blk, rows, :] + ysb_ref[0, blk, rows, :])
    y = y + d_ref[...] * from_slabs(lambda blk, rows: u_ref[blk, rows, :])
    t = 0.5 * y * (1.0 + jnp.tanh(math.sqrt(2.0 / math.pi) * (y + 0.044715 * (y * y * y))))
    glu = (t * jax.nn.sigmoid(_dot(t.astype(BF16), wglu_ref[...]))).astype(BF16)
    stacked = lambda ref: jnp.concatenate([ref[sub] for sub in range(TOKEN_BATCH)], axis=0)
    m = (gate(0) * _dot(glu, wps_ref[...]) + gate(1) * _dot(stacked(ya_ref), wpa_ref[...])
         + gate(2) * _dot(stacked(yn_ref), wpn_ref[...]))
    out = _dot(m.astype(BF16), wo_ref[...])
    for sub, x in enumerate(xs):
        o_ref[sub] = x + mod_ref[sub, 5:6, :] * _rows(out, sub, tm)


def _merge(xs, mod, g, ys, u2, ya, yn, ssm_d, wgt, wglu, wps, wpa, wpn, wo, *, li, n_tok, n_lat_tiles, ctx_group):
    b, _, d = xs.shape
    tm = TOKEN_TILE
    return pl.pallas_call(
        _merge_kernel,
        grid=(n_tok // tm, b // TOKEN_BATCH),
        in_specs=[_tok_spec(d, tm), _mod_spec(d, n_lat_tiles, ctx_group), _resident(g.shape),
                  _slab_spec(tm, b, (0,)), _slab_spec(tm, b, (1,)), _slab_spec(tm, b),
                  _tok_spec(GQA_WIDTH, tm), _tok_spec(NA_WIDTH, tm), _resident((1, SSM_WIDTH)),
                  _layer_resident(wgt, li), _layer_resident(wglu, li), _layer_resident(wps, li),
                  _layer_resident(wpa, li), _layer_resident(wpn, li), _layer_resident(wo, li)],
        out_specs=_tok_spec(d, tm),
        out_shape=jax.ShapeDtypeStruct((b, n_tok, d), F32),
        compiler_params=_params(2),
        name="mixer_merge",
    )(xs, mod, g, ys, ys, u2, ya, yn, ssm_d.reshape(1, SSM_WIDTH), wgt, wglu, wps, wpa, wpn, wo)


def _rope_tables(l, c_len):
    t = jnp.arange(l)
    pos = jnp.stack([t // GRID_W, t % GRID_W], axis=-1).astype(F32)
    half = HEAD_DIM // 2
    inv = ROPE_THETA ** (-jnp.arange(0, half, 2, dtype=F32) / half)
    ang = pos[:, :, None] * inv
    cos, sin = jnp.cos(ang), jnp.sin(ang)
    zero = jnp.zeros_like(sin[:, 0])
    cos_h = jnp.concatenate([cos[:, 0], cos[:, 0], cos[:, 1], cos[:, 1]], axis=-1)
    sa_h = jnp.concatenate([zero, sin[:, 0], zero, sin[:, 1]], axis=-1)
    sb_h = jnp.concatenate([-sin[:, 0], zero, -sin[:, 1], zero], axis=-1)

    def full(tab, ctx_fill):
        tab = jnp.concatenate([tab, tab], axis=-1)
        return jnp.concatenate([tab, jnp.full((c_len, LANES), ctx_fill, F32)], axis=0)

    return full(cos_h, 1.0), full(sa_h, 0.0), full(sb_h, 0.0)


def kernel(x, c, ctx, c_ctx, w_ada, b_ada, norm_g, ffn1_wg, ffn1_wu, ffn1_wd, ffn2_wg, ffn2_wu, ffn2_wd, w_in,
           ssm_a_re, ssm_a_im, ssm_log_dt, ssm_b_re, ssm_b_im, ssm_c_re, ssm_c_im, ssm_d, ssm_w_glu, gqa_sink,
           na_rpb, w_p_ssm, w_p_gqa, w_p_na, w_out, final_g):
    b, l, d = x.shape
    c_len = ctx.shape[1]
    s = l + c_len
    depth = w_ada.shape[0]
    tm = TOKEN_TILE
    assert b + TOKEN_BATCH <= MOD_ROWS and b % TOKEN_BATCH == 0 and l % tm == 0 and c_len % tm == 0 and l % (GRID_W * NA_ROWS) == 0
    assert c_len % Q_BLOCK == 0 and l % c_len == 0 and b % SUBLANES == 0
    assert l % SSM_TILE == 0 and c_len % SSM_TILE == 0
    n_lat_tiles = l // tm

    cin = jnp.zeros((MOD_ROWS, d), F32).at[:b].set(c).at[b:b + TOKEN_BATCH].set(c_ctx)
    mods = _ada(cin, w_ada, b_ada).reshape(depth, MOD_ROWS, N_MOD, d)
    cos_t, sa_t, sb_t = _rope_tables(l, c_len)

    perm = np.asarray(GQA_HEAD_PERM)
    q0 = SSM_WIDTH
    w_q = w_in[:, :, q0:q0 + GQA_WIDTH].reshape(depth, d, GQA_HEADS, HEAD_DIM)[:, :, perm]
    w_in_p = jnp.concatenate([w_in[:, :, :q0], w_q.reshape(depth, d, GQA_WIDTH),
                              w_in[:, :, q0 + GQA_WIDTH:MIXER_COLS]], axis=-1).astype(BF16)
    w_gt = w_in[:, :, MIXER_COLS:].astype(BF16)
    w_pa = w_p_gqa.reshape(depth, GQA_HEADS, HEAD_DIM, d)[:, perm].reshape(depth, GQA_WIDTH, d).astype(BF16)
    bf = lambda z: z.astype(BF16)
    f1 = (bf(ffn1_wg), bf(ffn1_wu), bf(ffn1_wd))
    f2 = (bf(ffn2_wg), bf(ffn2_wu), bf(ffn2_wd))
    w_glu, w_ps, w_pn, w_o = bf(ssm_w_glu), bf(w_p_ssm), bf(w_p_na), bf(w_out)

    tile_kw = dict(n_lat_tiles=n_lat_tiles, ctx_group=b // TOKEN_BATCH)
    xs = x
    for li in range(depth):
        ctx_out = li < depth - 1
        n_tok = s if ctx_out else l
        mod, g = mods[li], norm_g[li]
        xs = _ffn(xs, mod, g, *f1, li=li, k0=0, gi=0, n_tok=s, ctx=ctx if li == 0 else None, **tile_kw)
        u2, q, k, v, nq, nk, nv = _inproj(xs, mod, g, w_in_p, cos_t, sa_t, sb_t, li=li, **tile_kw)
        lam, s_in, s_out = _s5_block_operands(
            *_s5_prep(ssm_a_re[li], ssm_a_im[li], ssm_log_dt[li], ssm_b_re[li], ssm_b_im[li]),
            ssm_c_re[li], ssm_c_im[li])
        ys = _s5_scan(u2, lam, s_in, s_out, nb=b, n_lat_tiles=l // SSM_TILE, n_ctx_tiles=c_len // SSM_TILE)
        ya = _gqa(gqa_sink[li], q, k, v, n_lat=l // Q_BLOCK, n_ctx_blocks=c_len // Q_BLOCK, ctx_out=ctx_out,
                  c_len=c_len)
        yn = _na(nq, nk, nv, _na_bias_pairs(na_rpb[li]), rows=l // GRID_W, n_ctx_groups=c_len // GRID_W,
                 ctx_out=ctx_out, c_len=c_len)
        xs = _merge(xs, mod, g, ys, u2, ya, yn, ssm_d[li], w_gt, w_glu, w_ps, w_pa, w_pn, w_o,
                    li=li, n_tok=n_tok, **tile_kw)
        xs = _ffn(xs, mod, g, *f2, li=li, k0=6, gi=2, n_tok=n_tok, final_g=None if ctx_out else final_g, **tile_kw)
    return xs
```

```python
import functools
import math

import jax
import jax.numpy as jnp
import numpy as np
from jax import lax
from jax.experimental import pallas as pl
from jax.experimental.pallas import tpu as pltpu

F32 = jnp.float32
BF16 = jnp.bfloat16

EPS = 1e-6
NEG = -1e30
N_MOD = 9
GRID_W = 64
HEAD_DIM = 64
GQA_HEADS = 8
GQA_KV_HEADS = 2
NA_HEADS = 8
NA_ROWS = 8
NA_COLS = 16
Q_BLOCK = 128
ROPE_THETA = 10000.0
SSM_GROUP = 16
SSM_GROUPS = 24
SSM_STATE = 64
SSM_WIDTH = SSM_GROUP * SSM_GROUPS
GQA_WIDTH = GQA_HEADS * HEAD_DIM
GQA_KV_WIDTH = GQA_KV_HEADS * HEAD_DIM
NA_WIDTH = NA_HEADS * HEAD_DIM
N_BRANCH = 3
MIXER_COLS = SSM_WIDTH + GQA_WIDTH + 2 * GQA_KV_WIDTH + 3 * NA_WIDTH

LANES = 128
SUBLANES = 8
MOD_ROWS = 16
TOKEN_TILE = 256
TOKEN_BATCH = 2
FFN_CHUNK = 256
GQA_QBLOCKS = 2
NA_QROWS = 4
NA_UNION = NA_QROWS + NA_ROWS - 1
SSM_TILE = 64
SSM_PIPE_TILES = 4
SSM_BLOCK_GROUPS = LANES // SSM_GROUP
SSM_BLOCKS = SSM_GROUPS // SSM_BLOCK_GROUPS
SSM_BLOCK_STATE = SSM_BLOCK_GROUPS * SSM_STATE
VMEM_LIMIT = 56 * 1024 * 1024
GQA_HEAD_PERM = (0, 4, 1, 5, 2, 6, 3, 7)


def _params(n_axes):
    return pltpu.CompilerParams(dimension_semantics=("arbitrary",) * n_axes, vmem_limit_bytes=VMEM_LIMIT)


def _resident(shape):
    return pl.BlockSpec(shape, lambda *_: (0,) * len(shape), pipeline_mode=pl.Buffered(1))


def _layer_resident(stack, li):
    nd = stack.ndim
    return pl.BlockSpec((None,) + stack.shape[1:], lambda *_: (li,) + (0,) * (nd - 1),
                        pipeline_mode=pl.Buffered(1))


def _dot(a, b):
    return jnp.dot(a, b, preferred_element_type=F32)


def _dot_t(a, b):
    return lax.dot_general(a, b, (((1,), (1,)), ((), ())), preferred_element_type=F32)


def _rmsnorm(x, g):
    return x * lax.rsqrt(jnp.mean(x * x, axis=-1, keepdims=True) + EPS) * g


def _mod_norm(x, g, shift, scale):
    return _rmsnorm(x, g) * (1.0 + scale) + shift


def _ada_kernel(c_ref, w_ref, b_ref, o_ref):
    s = c_ref[...]
    s = s * jax.nn.sigmoid(s)
    o_ref[0] = _dot(s.astype(BF16), w_ref[0].astype(BF16)) + b_ref[0]


def _ada(cin, w_ada, b_ada):
    depth, d, n = w_ada.shape
    tn = n // 4
    return pl.pallas_call(
        _ada_kernel,
        grid=(depth, n // tn),
        in_specs=[
            pl.BlockSpec((MOD_ROWS, d), lambda l, j: (0, 0)),
            pl.BlockSpec((1, d, tn), lambda l, j: (l, 0, j)),
            pl.BlockSpec((1, 1, tn), lambda l, j: (l, 0, j)),
        ],
        out_specs=pl.BlockSpec((1, MOD_ROWS, tn), lambda l, j: (l, 0, j)),
        out_shape=jax.ShapeDtypeStruct((depth, MOD_ROWS, n), F32),
        compiler_params=_params(2),
        name="ada_mod",
    )(cin, w_ada, b_ada.reshape(depth, 1, n))


def _tok_spec(width, tm):
    return pl.BlockSpec((TOKEN_BATCH, tm, width), lambda j, b: (b, j, 0))


def _rows(val, sub, tm):
    return val[sub * tm:(sub + 1) * tm]


def _slab_rows(sub, tm, nb):
    return pl.ds(pl.program_id(1) * TOKEN_BATCH + sub, tm, stride=nb)


def _slab_spec(tm, nb):
    return pl.BlockSpec((SSM_BLOCKS, tm * nb, LANES), lambda j, b: (0, j, 0))


def _mod_spec(d, n_lat_tiles, ctx_group):
    return pl.BlockSpec((TOKEN_BATCH, N_MOD, d), lambda j, b: (jnp.where(j >= n_lat_tiles, ctx_group, b), 0, 0))


def _stacked_mod_norm(xs, g, mod_ref, shift_row):
    return jnp.concatenate(
        [_mod_norm(x, g, mod_ref[sub, shift_row:shift_row + 1, :], mod_ref[sub, shift_row + 1:shift_row + 2, :])
         for sub, x in enumerate(xs)], axis=0).astype(BF16)


def _ffn_kernel(*refs, k0, gi, final, n_lat_tiles, split_in):
    refs = list(refs)
    x_ref = refs.pop(0)
    ctx_ref = refs.pop(0) if split_in else None
    mod_ref, g_ref, wg_ref, wu_ref, wd_ref = refs[:5]
    fg_ref = refs[5] if final else None
    o_ref, t_ref = refs[-2:]
    tm = x_ref.shape[1]
    xs = [x_ref[sub] for sub in range(TOKEN_BATCH)]
    if split_in:
        is_ctx = pl.program_id(0) >= n_lat_tiles
        xs = [jnp.where(is_ctx, ctx_ref[sub], x) for sub, x in enumerate(xs)]
    h = _stacked_mod_norm(xs, g_ref[gi:gi + 1, :], mod_ref, k0)
    f = wg_ref.shape[1]
    for c0 in range(0, f, FFN_CHUNK):
        a = _dot(h, wg_ref[:, c0:c0 + FFN_CHUNK])
        u = _dot(h, wu_ref[:, c0:c0 + FFN_CHUNK])
        t_ref[:, c0:c0 + FFN_CHUNK] = (a * jax.nn.sigmoid(a) * u).astype(BF16)
    y = _dot(t_ref[...], wd_ref[...])
    for sub, x in enumerate(xs):
        out = x + (0.5 * mod_ref[sub, k0 + 2:k0 + 3, :]) * _rows(y, sub, tm)
        if final:
            out = _rmsnorm(out, fg_ref[...])
        o_ref[sub] = out


def _ffn(xs, mod, g, wg, wu, wd, *, li, k0, gi, n_tok, n_lat_tiles, ctx_group, final_g=None, ctx=None):
    b, _, d = xs.shape
    f = wg.shape[2]
    tm = TOKEN_TILE
    final = final_g is not None
    split_in = ctx is not None
    if split_in:
        in_specs = [pl.BlockSpec((TOKEN_BATCH, tm, d), lambda j, bb: (bb, jnp.minimum(j, n_lat_tiles - 1), 0)),
                    pl.BlockSpec((TOKEN_BATCH, tm, d), lambda j, bb: (bb, jnp.maximum(j - n_lat_tiles, 0), 0))]
        args = [xs, ctx]
    else:
        in_specs = [_tok_spec(d, tm)]
        args = [xs]
    in_specs += [_mod_spec(d, n_lat_tiles, ctx_group), _resident(g.shape),
                 _layer_resident(wg, li), _layer_resident(wu, li), _layer_resident(wd, li)]
    args += [mod, g, wg, wu, wd]
    if final:
        in_specs.append(_resident((1, d)))
        args.append(final_g.reshape(1, d))
    return pl.pallas_call(
        functools.partial(_ffn_kernel, k0=k0, gi=gi, final=final, n_lat_tiles=n_lat_tiles, split_in=split_in),
        grid=(n_tok // tm, b // TOKEN_BATCH),
        in_specs=in_specs,
        out_specs=_tok_spec(d, tm),
        out_shape=jax.ShapeDtypeStruct((b, n_tok, d), F32),
        scratch_shapes=[pltpu.VMEM((TOKEN_BATCH * tm, f), BF16)],
        compiler_params=_params(2),
        name="ffn_half_final" if final else "ffn_half",
    )(*args)


def _inproj_kernel(x_ref, mod_ref, g_ref, w_ref, cos_ref, sa_ref, sb_ref,
                   u_ref, q_ref, k_ref, v_ref, nq_ref, nk_ref, nv_ref):
    tm = x_ref.shape[1]
    nb = u_ref.shape[1] // tm
    h = _stacked_mod_norm([x_ref[sub] for sub in range(TOKEN_BATCH)], g_ref[1:2, :], mod_ref, 3)
    cos, sa, sb = (jnp.concatenate([t[...]] * TOKEN_BATCH, axis=0) for t in (cos_ref, sa_ref, sb_ref))
    scale = HEAD_DIM ** -0.5

    def mm(c0, n):
        return _dot(h, w_ref[:, c0:c0 + n])

    def rope(z):
        return z * cos + pltpu.roll(z, 16, 1) * sa + pltpu.roll(z, LANES - 16, 1) * sb

    def put(ref, val, lanes=slice(None)):
        for sub in range(TOKEN_BATCH):
            ref[sub, :, lanes] = _rows(val, sub, tm)

    c0 = SSM_WIDTH + GQA_WIDTH + 2 * GQA_KV_WIDTH
    z = mm(0, c0)
    col = lambda n: z[:, n * LANES:(n + 1) * LANES]
    n = 0
    for blk in range(SSM_BLOCKS):
        for sub in range(TOKEN_BATCH):
            u_ref[blk, _slab_rows(sub, tm, nb), :] = _rows(col(n), sub, tm)
        n += 1
    for j in range(GQA_WIDTH // LANES):
        put(q_ref, (rope(col(n)) * scale).astype(BF16), slice(j * LANES, (j + 1) * LANES))
        n += 1
    put(k_ref, rope(col(n)).astype(BF16))
    put(v_ref, col(n + 1).astype(BF16))
    put(nq_ref, (mm(c0, NA_WIDTH) * scale).astype(BF16))
    c0 += NA_WIDTH
    put(nk_ref, mm(c0, NA_WIDTH).astype(BF16))
    c0 += NA_WIDTH
    put(nv_ref, mm(c0, NA_WIDTH).astype(BF16))


def _inproj(xs, mod, g, w_in, cos_t, sa_t, sb_t, *, li, n_lat_tiles, ctx_group):
    b, s, d = xs.shape
    tm = TOKEN_TILE
    assert w_in.shape[2] == MIXER_COLS
    widths = (GQA_WIDTH, GQA_KV_WIDTH, GQA_KV_WIDTH, NA_WIDTH, NA_WIDTH, NA_WIDTH)
    dtypes = (BF16,) * len(widths)
    rope_spec = pl.BlockSpec((tm, LANES), lambda j, bb: (j, 0))
    return pl.pallas_call(
        _inproj_kernel,
        grid=(s // tm, b // TOKEN_BATCH),
        in_specs=[_tok_spec(d, tm), _mod_spec(d, n_lat_tiles, ctx_group), _resident(g.shape),
                  _layer_resident(w_in, li), rope_spec, rope_spec, rope_spec],
        out_specs=[_slab_spec(tm, b)] + [_tok_spec(w, tm) for w in widths],
        out_shape=[jax.ShapeDtypeStruct((SSM_BLOCKS, s * b, LANES), F32)]
                  + [jax.ShapeDtypeStruct((b, s, w), dt) for w, dt in zip(widths, dtypes)],
        compiler_params=_params(2),
        name="mixer_in_proj",
    )(xs, mod, g, w_in, cos_t, sa_t, sb_t)


def _softmax_rows(s_ref, p_ref, extra=None):
    s = s_ref[...]
    m = jnp.max(s, axis=1, keepdims=True)
    if extra is not None:
        m = jnp.maximum(m, extra)
    e = jnp.exp(s - m)
    den = jnp.sum(e, axis=1, keepdims=True)
    if extra is not None:
        den = den + jnp.exp(extra - m)
    p_ref[...] = e.astype(BF16)
    return 1.0 / den


def _head_half(qp, low, half):
    return jnp.where(low if half == 0 else jnp.logical_not(low), qp, jnp.zeros_like(qp))


def _gqa_kernel(sink_ref, q_ref, *refs, n_lat):
    nkb = GQA_QBLOCKS + 2
    k_refs, kx_ref = refs[:nkb], refs[nkb]
    v_refs, vx_ref = refs[nkb + 1:2 * nkb + 1], refs[2 * nkb + 1]
    o_ref, s_ref, p_ref = refs[2 * nkb + 2:]
    n_keys = s_ref.shape[2]
    far = 8 * n_keys
    row = lax.broadcasted_iota(jnp.int32, (Q_BLOCK, n_keys), 0)
    col = lax.broadcasted_iota(jnp.int32, (Q_BLOCK, n_keys), 1)
    low = lax.broadcasted_iota(jnp.int32, (Q_BLOCK, LANES), 1) < HEAD_DIM
    sinks = jnp.concatenate([jnp.full((Q_BLOCK, 1), sink_ref[h], F32) for h in GQA_HEAD_PERM], axis=0)
    n_pairs = GQA_WIDTH // LANES
    for a in range(GQA_QBLOCKS):
        i = pl.program_id(1) * GQA_QBLOCKS + a
        is_lat = i < n_lat
        off_p = jnp.where(jnp.logical_and(is_lat, i > 0), 0, far)
        off_c = jnp.where(is_lat, 0, far)
        off_n = jnp.where(jnp.logical_and(is_lat, i < n_lat - 1), 0, far)
        ok = ((col >= row + off_p) & (col < Q_BLOCK)
              | (col >= Q_BLOCK + off_c) & (col < 2 * Q_BLOCK)
              | (col >= 2 * Q_BLOCK) & (col <= 2 * Q_BLOCK + row - off_n)
              | (col >= 3 * Q_BLOCK))
        keys = jnp.concatenate([k_refs[a][0], k_refs[a + 1][0], k_refs[a + 2][0], kx_ref[0]], axis=0)
        vals = jnp.concatenate([v_refs[a][0], v_refs[a + 1][0], v_refs[a + 2][0], vx_ref[0]], axis=0)
        qrows = slice(a * Q_BLOCK, (a + 1) * Q_BLOCK)
        for j in range(n_pairs):
            qp = q_ref[0, qrows, j * LANES:(j + 1) * LANES]
            for half in range(2):
                r0 = (2 * j + half) * Q_BLOCK
                s_ref[a, r0:r0 + Q_BLOCK, :] = jnp.where(ok, _dot_t(_head_half(qp, low, half), keys), NEG)
        inv = _softmax_rows(s_ref.at[a], p_ref.at[a], sinks)
        for j in range(n_pairs):
            outs = []
            for half in range(2):
                r0 = (2 * j + half) * Q_BLOCK
                outs.append(_dot(p_ref[a, r0:r0 + Q_BLOCK, :], vals) * inv[r0:r0 + Q_BLOCK])
            o_ref[0, qrows, j * LANES:(j + 1) * LANES] = jnp.where(low, outs[0], outs[1]).astype(BF16)


def _gqa(sink, q, k, v, *, n_lat, n_ctx_blocks, ctx_out, c_len):
    b = q.shape[0]
    nq = n_lat + (n_ctx_blocks if ctx_out else 0)
    assert n_lat % GQA_QBLOCKS == 0 and n_ctx_blocks % GQA_QBLOCKS == 0
    ctx_idx = n_lat * Q_BLOCK // c_len
    n_keys = 3 * Q_BLOCK + c_len
    qt = GQA_QBLOCKS * Q_BLOCK

    def kv_spec(t):
        return pl.BlockSpec((1, Q_BLOCK, GQA_KV_WIDTH),
                            lambda bb, g: (bb, jnp.clip(g * GQA_QBLOCKS - 1 + t, 0, n_lat - 1), 0))

    kv = [kv_spec(t) for t in range(GQA_QBLOCKS + 2)]
    kv.append(pl.BlockSpec((1, c_len, GQA_KV_WIDTH), lambda bb, g: (bb, ctx_idx, 0)))
    return pl.pallas_call(
        functools.partial(_gqa_kernel, n_lat=n_lat),
        grid=(b, nq // GQA_QBLOCKS),
        in_specs=[pl.BlockSpec(memory_space=pltpu.SMEM),
                  pl.BlockSpec((1, qt, GQA_WIDTH), lambda bb, g: (bb, g, 0))] + kv + kv,
        out_specs=pl.BlockSpec((1, qt, GQA_WIDTH), lambda bb, g: (bb, g, 0)),
        out_shape=jax.ShapeDtypeStruct((b, nq * Q_BLOCK, GQA_WIDTH), BF16),
        scratch_shapes=[pltpu.VMEM((GQA_QBLOCKS, GQA_HEADS * Q_BLOCK, n_keys), F32),
                        pltpu.VMEM((GQA_QBLOCKS, GQA_HEADS * Q_BLOCK, n_keys), BF16)],
        compiler_params=_params(2),
        name="window_gqa",
    )(sink, q, *([k] * (GQA_QBLOCKS + 3)), *([v] * (GQA_QBLOCKS + 3)))


def _na_band_start(r, rows):
    return jnp.clip(r - NA_ROWS // 2, 0, rows - NA_ROWS)


def _na_union_start(g, rows):
    return jnp.minimum(_na_band_start(g * NA_QROWS, rows), rows - NA_UNION)


def _na_kernel(q_ref, kb_ref, vb_ref, kx_ref, vx_ref, bias_ref, o_ref, s_ref, p_ref, *, rows):
    g = pl.program_id(1)
    n_loc = NA_ROWS * GRID_W
    low = lax.broadcasted_iota(jnp.int32, (GRID_W, LANES), 1) < HEAD_DIM
    n_pairs = NA_WIDTH // LANES
    u0 = _na_union_start(g, rows)
    for rho in range(NA_QROWS):
        r = g * NA_QROWS + rho
        is_ctx = r >= rows
        band = _na_band_start(r, rows)
        off = pl.multiple_of(jnp.where(is_ctx, 0, band - u0) * GRID_W, GRID_W)
        dr0 = jnp.where(is_ctx, 0, (NA_ROWS - 1) - (r - band))
        pen = jnp.where(is_ctx, NEG, 0.0).astype(F32)
        qrows = slice(rho * GRID_W, (rho + 1) * GRID_W)
        for j in range(n_pairs):
            sl = slice(j * LANES, (j + 1) * LANES)
            qp = q_ref[0, qrows, sl]
            keys = jnp.concatenate([kb_ref[0, pl.ds(off, n_loc), sl], kx_ref[0, :, sl]], axis=0)
            s2 = _dot_t(jnp.concatenate([_head_half(qp, low, 0), _head_half(qp, low, 1)], axis=0), keys)
            for half in range(2):
                h = 2 * j + half
                r0 = h * GRID_W
                s = s2[half * GRID_W:(half + 1) * GRID_W]
                bias = jnp.concatenate([bias_ref[h, dr0 + i] for i in range(0, NA_ROWS, 2)], axis=1)
                s_ref[rho, r0:r0 + GRID_W, 0:n_loc] = s[:, 0:n_loc] + (bias + pen)
                s_ref[rho, r0:r0 + GRID_W, n_loc:] = s[:, n_loc:]
        inv = _softmax_rows(s_ref.at[rho], p_ref.at[rho])
        for j in range(n_pairs):
            sl = slice(j * LANES, (j + 1) * LANES)
            vals = jnp.concatenate([vb_ref[0, pl.ds(off, n_loc), sl], vx_ref[0, :, sl]], axis=0)
            r0 = 2 * j * GRID_W
            o2 = _dot(p_ref[rho, r0:r0 + 2 * GRID_W, :], vals) * inv[r0:r0 + 2 * GRID_W]
            o_ref[0, qrows, sl] = jnp.where(low, o2[:GRID_W], o2[GRID_W:]).astype(BF16)


def _na(q, k, v, bias, *, rows, n_ctx_rows, ctx_out, c_len):
    b = q.shape[0]
    assert rows % NA_QROWS == 0 and n_ctx_rows % NA_QROWS == 0 and rows >= NA_UNION
    ng = (rows + (n_ctx_rows if ctx_out else 0)) // NA_QROWS
    qt = NA_QROWS * GRID_W
    ctx_idx = rows * GRID_W // c_len
    n_keys = NA_ROWS * GRID_W + c_len
    union_spec = pl.BlockSpec((pl.Element(1), pl.Element(NA_UNION * GRID_W), pl.Element(NA_WIDTH)),
                              lambda bb, g: (bb, _na_union_start(g, rows) * GRID_W, 0))
    ctx_spec = pl.BlockSpec((1, c_len, NA_WIDTH), lambda bb, g: (bb, ctx_idx, 0))
    return pl.pallas_call(
        functools.partial(_na_kernel, rows=rows),
        grid=(b, ng),
        in_specs=[pl.BlockSpec((1, qt, NA_WIDTH), lambda bb, g: (bb, g, 0)),
                  union_spec, union_spec, ctx_spec, ctx_spec, _resident(bias.shape)],
        out_specs=pl.BlockSpec((1, qt, NA_WIDTH), lambda bb, g: (bb, g, 0)),
        out_shape=jax.ShapeDtypeStruct((b, ng * qt, NA_WIDTH), BF16),
        scratch_shapes=[pltpu.VMEM((NA_QROWS, NA_HEADS * GRID_W, n_keys), F32),
                        pltpu.VMEM((NA_QROWS, NA_HEADS * GRID_W, n_keys), BF16)],
        compiler_params=_params(2),
        name="neighborhood_attn",
    )(q, k, v, k, v, bias)


def _na_bias_pairs(rpb):
    cols = np.arange(GRID_W)
    col_start = np.clip(cols - NA_COLS // 2, 0, GRID_W - NA_COLS)
    kc = np.arange(GRID_W)
    in_win = (kc[None, :] >= col_start[:, None]) & (kc[None, :] < col_start[:, None] + NA_COLS)
    col_idx = np.clip(kc[None, :] - cols[:, None] + (NA_COLS - 1), 0, 2 * NA_COLS - 2)
    tab = jnp.where(in_win[None, None], rpb.astype(F32)[:, :, col_idx], NEG)
    return jnp.concatenate([tab[:, :-1], tab[:, 1:]], axis=-1)


def _s5_prep_kernel(are_ref, aim_ref, ldt_ref, bre_ref, bim_ref, lbr_ref, lbi_ref, bbr_ref, bbi_ref):
    are, aim = are_ref[...], aim_ref[...]
    dt = jnp.exp(ldt_ref[...])
    mag = jnp.exp(are * dt)
    lbr, lbi = mag * jnp.cos(aim * dt), mag * jnp.sin(aim * dt)
    lbr_ref[...] = lbr
    lbi_ref[...] = lbi
    nr, ni = lbr - 1.0, lbi
    den = are * are + aim * aim
    cr, ci = (nr * are + ni * aim) / den, (ni * are - nr * aim) / den
    br, bi = bre_ref[...], bim_ref[...]
    bbr_ref[...] = cr * br - ci * bi
    bbi_ref[...] = cr * bi + ci * br


def _s5_prep(a_re, a_im, log_dt, b_re, b_im):
    nd, g, p = a_re.shape
    i = b_re.shape[-1]
    n = nd * g * p
    col = lambda z: z.reshape(n, 1)
    ldt = jnp.broadcast_to(log_dt[:, :, None], (nd, g, p))
    whole = lambda w: pl.BlockSpec((n, w), lambda: (0, 0))
    lbr, lbi, bbr, bbi = pl.pallas_call(
        _s5_prep_kernel,
        in_specs=[whole(1)] * 3 + [whole(i)] * 2,
        out_specs=[whole(1)] * 2 + [whole(i)] * 2,
        out_shape=[jax.ShapeDtypeStruct((n, 1), F32)] * 2 + [jax.ShapeDtypeStruct((n, i), F32)] * 2,
        name="s5_prep",
    )(col(a_re), col(a_im), col(ldt), b_re.reshape(n, i), b_im.reshape(n, i))
    return lbr.reshape(nd, g, p), lbi.reshape(nd, g, p), bbr.reshape(nd, g, p, i), bbi.reshape(nd, g, p, i)


def _s5_block_operands(lbr, lbi, bbr, bbi, c_re, c_im):
    nd, g, p = lbr.shape
    i = bbr.shape[-1]
    nb, gb = SSM_BLOCKS, SSM_BLOCK_GROUPS
    eye = jnp.eye(gb, dtype=F32)

    def in_map(z):
        z = z.reshape(nd, nb, gb, p, i).transpose(0, 1, 2, 4, 3)
        return (z[:, :, :, :, None, :] * eye[None, None, :, None, :, None]).reshape(nd, nb, gb * i, gb * p)

    def out_map(z):
        z = z.reshape(nd, nb, gb, i, p).transpose(0, 1, 2, 4, 3)
        return (z[:, :, :, :, None, :] * eye[None, None, :, None, :, None]).reshape(nd, nb, gb * p, gb * i)

    lam = jnp.stack([lbr.reshape(nd, nb, gb * p), lbi.reshape(nd, nb, gb * p)], axis=2)
    w_in = jnp.concatenate([in_map(bbr), in_map(bbi)], axis=-1).astype(BF16)
    w_out = jnp.concatenate([out_map(c_re.astype(F32)), -out_map(c_im.astype(F32))], axis=2).astype(BF16)
    return lam, w_in, w_out


def _s5_kernel(u_ref, lam_ref, win_ref, wout_ref, y_ref, st_ref, carry_ref, *, nb, tt, reverse):
    ns = SSM_BLOCK_STATE
    rows = tt * nb
    n_pipe = SSM_PIPE_TILES

    @pl.when(pl.program_id(0) == 0)
    def _():
        carry_ref[...] = jnp.zeros_like(carry_ref)

    lr = [jnp.broadcast_to(lam_ref[j, 0:1, :], (nb, ns)) for j in range(SSM_BLOCKS)]
    li = [jnp.broadcast_to(lam_ref[j, 1:2, :], (nb, ns)) for j in range(SSM_BLOCKS)]
    order = list(range(n_pipe))[::-1] if reverse else list(range(n_pipe))
    steps = list(range(tt))[::-1] if reverse else list(range(tt))
    carry = [carry_ref[k] for k in range(2 * SSM_BLOCKS)]
    for i in range(n_pipe + 2):
        if i < n_pipe:
            trows = slice(order[i] * rows, (order[i] + 1) * rows)
            for j in range(SSM_BLOCKS):
                st_ref[i % 3, j] = _dot(u_ref[j, trows, :].astype(BF16), win_ref[j])
        if 1 <= i <= n_pipe:
            buf = (i - 1) % 3
            for t in steps:
                r = slice(t * nb, (t + 1) * nb)
                for j in range(SSM_BLOCKS):
                    sr, si = carry[2 * j], carry[2 * j + 1]
                    nr = lr[j] * sr - li[j] * si + st_ref[buf, j, r, 0:ns]
                    ni = lr[j] * si + li[j] * sr + st_ref[buf, j, r, ns:2 * ns]
                    st_ref[buf, j, r, 0:ns] = nr
                    st_ref[buf, j, r, ns:2 * ns] = ni
                    carry[2 * j], carry[2 * j + 1] = nr, ni
        if i >= 2:
            trows = slice(order[i - 2] * rows, (order[i - 2] + 1) * rows)
            for j in range(SSM_BLOCKS):
                y_ref[j, trows, :] = _dot(st_ref[(i - 2) % 3, j].astype(BF16), wout_ref[j])
    for k in range(2 * SSM_BLOCKS):
        carry_ref[k] = carry[k]


def _s5_scan(u2, lam, w_in, w_out, *, direction, nb, n_lat, n_ctx):
    _, rows_total, _ = u2.shape
    tt = SSM_TILE
    span = SSM_PIPE_TILES * tt
    assert n_lat % span == 0 and n_ctx % span == 0
    n_lat_g, n_ctx_g = n_lat // span, n_ctx // span
    n_g = n_lat_g + n_ctx_g
    ns2 = 2 * SSM_BLOCK_STATE
    reverse = direction == 1

    def group(k):
        if reverse:
            return jnp.where(k < n_ctx_g, n_g - 1 - k, n_lat_g - 1 - (k - n_ctx_g))
        return jnp.where(k < n_ctx_g, n_lat_g + k, k - n_ctx_g)

    def par(*tail):
        return pl.BlockSpec((None,) + tail, lambda k: (direction,) + (0,) * len(tail))

    stream = pl.BlockSpec((SSM_BLOCKS, span * nb, LANES), lambda k: (0, group(k), 0))
    return pl.pallas_call(
        functools.partial(_s5_kernel, nb=nb, tt=tt, reverse=reverse),
        grid=(n_g,),
        in_specs=[stream, par(SSM_BLOCKS, 2, SSM_BLOCK_STATE), par(SSM_BLOCKS, LANES, ns2),
                  par(SSM_BLOCKS, ns2, LANES)],
        out_specs=stream,
        out_shape=jax.ShapeDtypeStruct((SSM_BLOCKS, rows_total, LANES), F32),
        scratch_shapes=[pltpu.VMEM((3, SSM_BLOCKS, tt * nb, ns2), F32),
                        pltpu.VMEM((2 * SSM_BLOCKS, nb, SSM_BLOCK_STATE), F32)],
        compiler_params=_params(1),
        name="s5_scan_bwd" if reverse else "s5_scan_fwd",
    )(u2, lam, w_in, w_out)


def _merge_kernel(x_ref, mod_ref, g_ref, ysf_ref, ysb_ref, u_ref, ya_ref, yn_ref, d_ref,
                  wgt_ref, wglu_ref, wps_ref, wpa_ref, wpn_ref, wo_ref, o_ref):
    d = x_ref.shape[2]
    tm = x_ref.shape[1]
    nb = u_ref.shape[1] // tm
    xs = [x_ref[sub] for sub in range(TOKEN_BATCH)]
    h = _stacked_mod_norm(xs, g_ref[1:2, :], mod_ref, 3)
    gate = lambda n: jax.nn.sigmoid(_dot(h, wgt_ref[:, n * d:(n + 1) * d]))

    def from_slabs(load):
        return jnp.concatenate([jnp.concatenate([load(blk, _slab_rows(sub, tm, nb)) for blk in range(SSM_BLOCKS)],
                                                axis=1) for sub in range(TOKEN_BATCH)], axis=0)

    y = from_slabs(lambda blk, rows: ysf_ref[blk, rows, :] + ysb_ref[blk, rows, :])
    y = y + d_ref[...] * from_slabs(lambda blk, rows: u_ref[blk, rows, :])
    t = 0.5 * y * (1.0 + jnp.tanh(math.sqrt(2.0 / math.pi) * (y + 0.044715 * (y * y * y))))
    glu = (t * jax.nn.sigmoid(_dot(t.astype(BF16), wglu_ref[...]))).astype(BF16)
    stacked = lambda ref: jnp.concatenate([ref[sub] for sub in range(TOKEN_BATCH)], axis=0)
    m = (gate(0) * _dot(glu, wps_ref[...]) + gate(1) * _dot(stacked(ya_ref), wpa_ref[...])
         + gate(2) * _dot(stacked(yn_ref), wpn_ref[...]))
    out = _dot(m.astype(BF16), wo_ref[...])
    for sub, x in enumerate(xs):
        o_ref[sub] = x + mod_ref[sub, 5:6, :] * _rows(out, sub, tm)


def _merge(xs, mod, g, ysf, ysb, u2, ya, yn, ssm_d, wgt, wglu, wps, wpa, wpn, wo, *, li, n_tok, n_lat_tiles,
           ctx_group):
    b, _, d = xs.shape
    tm = TOKEN_TILE
    return pl.pallas_call(
        _merge_kernel,
        grid=(n_tok // tm, b // TOKEN_BATCH),
        in_specs=[_tok_spec(d, tm), _mod_spec(d, n_lat_tiles, ctx_group), _resident(g.shape),
                  _slab_spec(tm, b), _slab_spec(tm, b), _slab_spec(tm, b),
                  _tok_spec(GQA_WIDTH, tm), _tok_spec(NA_WIDTH, tm), _resident((1, SSM_WIDTH)),
                  _layer_resident(wgt, li), _layer_resident(wglu, li), _layer_resident(wps, li),
                  _layer_resident(wpa, li), _layer_resident(wpn, li), _layer_resident(wo, li)],
        out_specs=_tok_spec(d, tm),
        out_shape=jax.ShapeDtypeStruct((b, n_tok, d), F32),
        compiler_params=_params(2),
        name="mixer_merge",
    )(xs, mod, g, ysf, ysb, u2, ya, yn, ssm_d.reshape(1, SSM_WIDTH), wgt, wglu, wps, wpa, wpn, wo)


def _rope_tables(l, c_len):
    t = jnp.arange(l)
    pos = jnp.stack([t // GRID_W, t % GRID_W], axis=-1).astype(F32)
    half = HEAD_DIM // 2
    inv = ROPE_THETA ** (-jnp.arange(0, half, 2, dtype=F32) / half)
    ang = pos[:, :, None] * inv
    cos, sin = jnp.cos(ang), jnp.sin(ang)
    zero = jnp.zeros_like(sin[:, 0])
    cos_h = jnp.concatenate([cos[:, 0], cos[:, 0], cos[:, 1], cos[:, 1]], axis=-1)
    sa_h = jnp.concatenate([zero, sin[:, 0], zero, sin[:, 1]], axis=-1)
    sb_h = jnp.concatenate([-sin[:, 0], zero, -sin[:, 1], zero], axis=-1)

    def full(tab, ctx_fill):
        tab = jnp.concatenate([tab, tab], axis=-1)
        return jnp.concatenate([tab, jnp.full((c_len, LANES), ctx_fill, F32)], axis=0)

    return full(cos_h, 1.0), full(sa_h, 0.0), full(sb_h, 0.0)


def kernel(x, c, ctx, c_ctx, w_ada, b_ada, norm_g, ffn1_wg, ffn1_wu, ffn1_wd, ffn2_wg, ffn2_wu, ffn2_wd, w_in,
           ssm_a_re, ssm_a_im, ssm_log_dt, ssm_b_re, ssm_b_im, ssm_c_re, ssm_c_im, ssm_d, ssm_w_glu, gqa_sink,
           na_rpb, w_p_ssm, w_p_gqa, w_p_na, w_out, final_g):
    b, l, d = x.shape
    c_len = ctx.shape[1]
    s = l + c_len
    depth = w_ada.shape[0]
    tm = TOKEN_TILE
    assert b + TOKEN_BATCH <= MOD_ROWS and b % TOKEN_BATCH == 0 and l % tm == 0 and c_len % tm == 0 and l % (GRID_W * NA_ROWS) == 0
    assert c_len % Q_BLOCK == 0 and l % c_len == 0 and b % SUBLANES == 0
    assert l % SSM_TILE == 0 and c_len % SSM_TILE == 0
    n_lat_tiles = l // tm

    cin = jnp.zeros((MOD_ROWS, d), F32).at[:b].set(c).at[b:b + TOKEN_BATCH].set(c_ctx)
    mods = _ada(cin, w_ada, b_ada).reshape(depth, MOD_ROWS, N_MOD, d)
    cos_t, sa_t, sb_t = _rope_tables(l, c_len)

    perm = np.asarray(GQA_HEAD_PERM)
    q0 = SSM_WIDTH
    w_q = w_in[:, :, q0:q0 + GQA_WIDTH].reshape(depth, d, GQA_HEADS, HEAD_DIM)[:, :, perm]
    w_in_p = jnp.concatenate([w_in[:, :, :q0], w_q.reshape(depth, d, GQA_WIDTH),
                              w_in[:, :, q0 + GQA_WIDTH:MIXER_COLS]], axis=-1).astype(BF16)
    w_gt = w_in[:, :, MIXER_COLS:].astype(BF16)
    w_pa = w_p_gqa.reshape(depth, GQA_HEADS, HEAD_DIM, d)[:, perm].reshape(depth, GQA_WIDTH, d).astype(BF16)
    bf = lambda z: z.astype(BF16)
    f1 = (bf(ffn1_wg), bf(ffn1_wu), bf(ffn1_wd))
    f2 = (bf(ffn2_wg), bf(ffn2_wu), bf(ffn2_wd))
    w_glu, w_ps, w_pn, w_o = bf(ssm_w_glu), bf(w_p_ssm), bf(w_p_na), bf(w_out)

    tile_kw = dict(n_lat_tiles=n_lat_tiles, ctx_group=b // TOKEN_BATCH)
    xs = x
    for li in range(depth):
        ctx_out = li < depth - 1
        n_tok = s if ctx_out else l
        mod, g = mods[li], norm_g[li]
        xs = _ffn(xs, mod, g, *f1, li=li, k0=0, gi=0, n_tok=s, ctx=ctx if li == 0 else None, **tile_kw)
        u2, q, k, v, nq, nk, nv = _inproj(xs, mod, g, w_in_p, cos_t, sa_t, sb_t, li=li, **tile_kw)
        lam, s_in, s_out = _s5_block_operands(
            *_s5_prep(ssm_a_re[li], ssm_a_im[li], ssm_log_dt[li], ssm_b_re[li], ssm_b_im[li]),
            ssm_c_re[li], ssm_c_im[li])
        ysf, ysb = (_s5_scan(u2, lam, s_in, s_out, direction=dd, nb=b, n_lat=l, n_ctx=c_len) for dd in range(2))
        ya = _gqa(gqa_sink[li], q, k, v, n_lat=l // Q_BLOCK, n_ctx_blocks=c_len // Q_BLOCK, ctx_out=ctx_out,
                  c_len=c_len)
        yn = _na(nq, nk, nv, _na_bias_pairs(na_rpb[li]), rows=l // GRID_W, n_ctx_rows=c_len // GRID_W,
                 ctx_out=ctx_out, c_len=c_len)
        xs = _merge(xs, mod, g, ysf, ysb, u2, ya, yn, ssm_d[li], w_gt, w_glu, w_ps, w_pa, w_pn, w_o,
                    li=li, n_tok=n_tok, **tile_kw)
        xs = _ffn(xs, mod, g, *f2, li=li, k0=6, gi=2, n_tok=n_tok, final_g=None if ctx_out else final_g, **tile_kw)
    return xs
```

```python
import functools
import math

import jax
import jax.numpy as jnp
import numpy as np
from jax import lax
from jax.experimental import pallas as pl
from jax.experimental.pallas import tpu as pltpu

F32 = jnp.float32
BF16 = jnp.bfloat16

EPS = 1e-6
NEG = -1e30
N_MOD = 9
GRID_W = 64
HEAD_DIM = 64
GQA_HEADS = 8
GQA_KV_HEADS = 2
NA_HEADS = 8
NA_ROWS = 8
NA_COLS = 16
Q_BLOCK = 128
ROPE_THETA = 10000.0
SSM_GROUP = 16
SSM_GROUPS = 24
SSM_STATE = 64
SSM_WIDTH = SSM_GROUP * SSM_GROUPS
GQA_WIDTH = GQA_HEADS * HEAD_DIM
GQA_KV_WIDTH = GQA_KV_HEADS * HEAD_DIM
NA_WIDTH = NA_HEADS * HEAD_DIM
N_BRANCH = 3
MIXER_COLS = SSM_WIDTH + GQA_WIDTH + 2 * GQA_KV_WIDTH + 3 * NA_WIDTH

LANES = 128
SUBLANES = 8
MOD_ROWS = 16
TOKEN_TILE = 256
TOKEN_BATCH = 2
FFN_CHUNK = 256
GQA_QBLOCKS = 2
NA_QROWS = 4
NA_UNION = NA_QROWS + NA_ROWS - 1
SSM_TILE = 64
SSM_PIPE_TILES = 4
SSM_BLOCK_GROUPS = LANES // SSM_GROUP
SSM_BLOCKS = SSM_GROUPS // SSM_BLOCK_GROUPS
SSM_BLOCK_STATE = SSM_BLOCK_GROUPS * SSM_STATE
VMEM_LIMIT = 56 * 1024 * 1024
GQA_HEAD_PERM = (0, 4, 1, 5, 2, 6, 3, 7)


def _params(n_axes):
    return pltpu.CompilerParams(dimension_semantics=("arbitrary",) * n_axes, vmem_limit_bytes=VMEM_LIMIT)


def _resident(shape):
    return pl.BlockSpec(shape, lambda *_: (0,) * len(shape), pipeline_mode=pl.Buffered(1))


def _layer_resident(stack, li):
    nd = stack.ndim
    return pl.BlockSpec((None,) + stack.shape[1:], lambda *_: (li,) + (0,) * (nd - 1),
                        pipeline_mode=pl.Buffered(1))


def _dot(a, b):
    return jnp.dot(a, b, preferred_element_type=F32)


def _dot_t(a, b):
    return lax.dot_general(a, b, (((1,), (1,)), ((), ())), preferred_element_type=F32)


def _rmsnorm(x, g):
    return x * lax.rsqrt(jnp.mean(x * x, axis=-1, keepdims=True) + EPS) * g


def _mod_norm(x, g, shift, scale):
    return _rmsnorm(x, g) * (1.0 + scale) + shift


def _ada_kernel(c_ref, w_ref, b_ref, o_ref):
    s = c_ref[...]
    s = s * jax.nn.sigmoid(s)
    o_ref[0] = _dot(s.astype(BF16), w_ref[0].astype(BF16)) + b_ref[0]


def _ada(cin, w_ada, b_ada):
    depth, d, n = w_ada.shape
    tn = n // 4
    return pl.pallas_call(
        _ada_kernel,
        grid=(depth, n // tn),
        in_specs=[
            pl.BlockSpec((MOD_ROWS, d), lambda l, j: (0, 0)),
            pl.BlockSpec((1, d, tn), lambda l, j: (l, 0, j)),
            pl.BlockSpec((1, 1, tn), lambda l, j: (l, 0, j)),
        ],
        out_specs=pl.BlockSpec((1, MOD_ROWS, tn), lambda l, j: (l, 0, j)),
        out_shape=jax.ShapeDtypeStruct((depth, MOD_ROWS, n), F32),
        compiler_params=_params(2),
        name="ada_mod",
    )(cin, w_ada, b_ada.reshape(depth, 1, n))


def _tok_spec(width, tm):
    return pl.BlockSpec((TOKEN_BATCH, tm, width), lambda j, b: (b, j, 0))


def _rows(val, sub, tm):
    return val[sub * tm:(sub + 1) * tm]


def _slab_rows(sub, tm, nb):
    return pl.ds(pl.program_id(1) * TOKEN_BATCH + sub, tm, stride=nb)


def _slab_spec(tm, nb):
    return pl.BlockSpec((SSM_BLOCKS, tm * nb, LANES), lambda j, b: (0, j, 0))


def _mod_spec(d, n_lat_tiles, ctx_group):
    return pl.BlockSpec((TOKEN_BATCH, N_MOD, d), lambda j, b: (jnp.where(j >= n_lat_tiles, ctx_group, b), 0, 0))


def _stacked_mod_norm(xs, g, mod_ref, shift_row):
    return jnp.concatenate(
        [_mod_norm(x, g, mod_ref[sub, shift_row:shift_row + 1, :], mod_ref[sub, shift_row + 1:shift_row + 2, :])
         for sub, x in enumerate(xs)], axis=0).astype(BF16)


def _ffn_kernel(*refs, k0, gi, final, n_lat_tiles, split_in):
    refs = list(refs)
    x_ref = refs.pop(0)
    ctx_ref = refs.pop(0) if split_in else None
    mod_ref, g_ref, wg_ref, wu_ref, wd_ref = refs[:5]
    fg_ref = refs[5] if final else None
    o_ref, t_ref = refs[-2:]
    tm = x_ref.shape[1]
    xs = [x_ref[sub] for sub in range(TOKEN_BATCH)]
    if split_in:
        is_ctx = pl.program_id(0) >= n_lat_tiles
        xs = [jnp.where(is_ctx, ctx_ref[sub], x) for sub, x in enumerate(xs)]
    h = _stacked_mod_norm(xs, g_ref[gi:gi + 1, :], mod_ref, k0)
    f = wg_ref.shape[1]
    for c0 in range(0, f, FFN_CHUNK):
        a = _dot(h, wg_ref[:, c0:c0 + FFN_CHUNK])
        u = _dot(h, wu_ref[:, c0:c0 + FFN_CHUNK])
        t_ref[:, c0:c0 + FFN_CHUNK] = (a * jax.nn.sigmoid(a) * u).astype(BF16)
    y = _dot(t_ref[...], wd_ref[...])
    for sub, x in enumerate(xs):
        out = x + (0.5 * mod_ref[sub, k0 + 2:k0 + 3, :]) * _rows(y, sub, tm)
        if final:
            out = _rmsnorm(out, fg_ref[...])
        o_ref[sub] = out


def _ffn(xs, mod, g, wg, wu, wd, *, li, k0, gi, n_tok, n_lat_tiles, ctx_group, final_g=None, ctx=None):
    b, _, d = xs.shape
    f = wg.shape[2]
    tm = TOKEN_TILE
    final = final_g is not None
    split_in = ctx is not None
    if split_in:
        in_specs = [pl.BlockSpec((TOKEN_BATCH, tm, d), lambda j, bb: (bb, jnp.minimum(j, n_lat_tiles - 1), 0)),
                    pl.BlockSpec((TOKEN_BATCH, tm, d), lambda j, bb: (bb, jnp.maximum(j - n_lat_tiles, 0), 0))]
        args = [xs, ctx]
    else:
        in_specs = [_tok_spec(d, tm)]
        args = [xs]
    in_specs += [_mod_spec(d, n_lat_tiles, ctx_group), _resident(g.shape),
                 _layer_resident(wg, li), _layer_resident(wu, li), _layer_resident(wd, li)]
    args += [mod, g, wg, wu, wd]
    if final:
        in_specs.append(_resident((1, d)))
        args.append(final_g.reshape(1, d))
    return pl.pallas_call(
        functools.partial(_ffn_kernel, k0=k0, gi=gi, final=final, n_lat_tiles=n_lat_tiles, split_in=split_in),
        grid=(n_tok // tm, b // TOKEN_BATCH),
        in_specs=in_specs,
        out_specs=_tok_spec(d, tm),
        out_shape=jax.ShapeDtypeStruct((b, n_tok, d), F32),
        scratch_shapes=[pltpu.VMEM((TOKEN_BATCH * tm, f), BF16)],
        compiler_params=_params(2),
        name="ffn_half_final" if final else "ffn_half",
    )(*args)


def _inproj_kernel(x_ref, mod_ref, g_ref, w_ref, cos_ref, sa_ref, sb_ref,
                   u_ref, q_ref, k_ref, v_ref, nq_ref, nk_ref, nv_ref):
    tm = x_ref.shape[1]
    nb = u_ref.shape[1] // tm
    h = _stacked_mod_norm([x_ref[sub] for sub in range(TOKEN_BATCH)], g_ref[1:2, :], mod_ref, 3)
    cos, sa, sb = (jnp.concatenate([t[...]] * TOKEN_BATCH, axis=0) for t in (cos_ref, sa_ref, sb_ref))
    scale = HEAD_DIM ** -0.5

    def mm(c0, n):
        return _dot(h, w_ref[:, c0:c0 + n])

    def rope(z):
        return z * cos + pltpu.roll(z, 16, 1) * sa + pltpu.roll(z, LANES - 16, 1) * sb

    def put(ref, val, lanes=slice(None)):
        for sub in range(TOKEN_BATCH):
            ref[sub, :, lanes] = _rows(val, sub, tm)

    c0 = SSM_WIDTH + GQA_WIDTH + 2 * GQA_KV_WIDTH
    z = mm(0, c0)
    col = lambda n: z[:, n * LANES:(n + 1) * LANES]
    n = 0
    for blk in range(SSM_BLOCKS):
        for sub in range(TOKEN_BATCH):
            u_ref[blk, _slab_rows(sub, tm, nb), :] = _rows(col(n), sub, tm)
        n += 1
    for j in range(GQA_WIDTH // LANES):
        put(q_ref, (rope(col(n)) * scale).astype(BF16), slice(j * LANES, (j + 1) * LANES))
        n += 1
    put(k_ref, rope(col(n)).astype(BF16))
    put(v_ref, col(n + 1).astype(BF16))
    put(nq_ref, (mm(c0, NA_WIDTH) * scale).astype(BF16))
    c0 += NA_WIDTH
    put(nk_ref, mm(c0, NA_WIDTH).astype(BF16))
    c0 += NA_WIDTH
    put(nv_ref, mm(c0, NA_WIDTH).astype(BF16))


def _inproj(xs, mod, g, w_in, cos_t, sa_t, sb_t, *, li, n_lat_tiles, ctx_group):
    b, s, d = xs.shape
    tm = TOKEN_TILE
    assert w_in.shape[2] == MIXER_COLS
    widths = (GQA_WIDTH, GQA_KV_WIDTH, GQA_KV_WIDTH, NA_WIDTH, NA_WIDTH, NA_WIDTH)
    dtypes = (BF16,) * len(widths)
    rope_spec = pl.BlockSpec((tm, LANES), lambda j, bb: (j, 0))
    return pl.pallas_call(
        _inproj_kernel,
        grid=(s // tm, b // TOKEN_BATCH),
        in_specs=[_tok_spec(d, tm), _mod_spec(d, n_lat_tiles, ctx_group), _resident(g.shape),
                  _layer_resident(w_in, li), rope_spec, rope_spec, rope_spec],
        out_specs=[_slab_spec(tm, b)] + [_tok_spec(w, tm) for w in widths],
        out_shape=[jax.ShapeDtypeStruct((SSM_BLOCKS, s * b, LANES), F32)]
                  + [jax.ShapeDtypeStruct((b, s, w), dt) for w, dt in zip(widths, dtypes)],
        compiler_params=_params(2),
        name="mixer_in_proj",
    )(xs, mod, g, w_in, cos_t, sa_t, sb_t)


def _softmax_rows(s_ref, p_ref, extra=None):
    s = s_ref[...]
    m = jnp.max(s, axis=1, keepdims=True)
    if extra is not None:
        m = jnp.maximum(m, extra)
    e = jnp.exp(s - m)
    den = jnp.sum(e, axis=1, keepdims=True)
    if extra is not None:
        den = den + jnp.exp(extra - m)
    p_ref[...] = e.astype(BF16)
    return 1.0 / den


def _head_half(qp, low, half):
    return jnp.where(low if half == 0 else jnp.logical_not(low), qp, jnp.zeros_like(qp))


def _gqa_kernel(sink_ref, q_ref, *refs, n_lat):
    nkb = GQA_QBLOCKS + 2
    k_refs, kx_ref = refs[:nkb], refs[nkb]
    v_refs, vx_ref = refs[nkb + 1:2 * nkb + 1], refs[2 * nkb + 1]
    o_ref, s_ref, p_ref = refs[2 * nkb + 2:]
    n_keys = s_ref.shape[1]
    far = 8 * n_keys
    row = lax.broadcasted_iota(jnp.int32, (Q_BLOCK, n_keys), 0)
    col = lax.broadcasted_iota(jnp.int32, (Q_BLOCK, n_keys), 1)
    low = lax.broadcasted_iota(jnp.int32, (Q_BLOCK, LANES), 1) < HEAD_DIM
    oks, keys, vals = [], [], []
    for a in range(GQA_QBLOCKS):
        i = pl.program_id(1) * GQA_QBLOCKS + a
        is_lat = i < n_lat
        off_p = jnp.where(jnp.logical_and(is_lat, i > 0), 0, far)
        off_c = jnp.where(is_lat, 0, far)
        off_n = jnp.where(jnp.logical_and(is_lat, i < n_lat - 1), 0, far)
        oks.append((col >= row + off_p) & (col < Q_BLOCK)
                   | (col >= Q_BLOCK + off_c) & (col < 2 * Q_BLOCK)
                   | (col >= 2 * Q_BLOCK) & (col <= 2 * Q_BLOCK + row - off_n)
                   | (col >= 3 * Q_BLOCK))
        keys.append(jnp.concatenate([k_refs[a][0], k_refs[a + 1][0], k_refs[a + 2][0], kx_ref[0]], axis=0))
        vals.append(jnp.concatenate([v_refs[a][0], v_refs[a + 1][0], v_refs[a + 2][0], vx_ref[0]], axis=0))
    items = [(a, h) for a in range(GQA_QBLOCKS) for h in range(GQA_HEADS)]
    invs, outs = {}, {}
    slot = lambda n: slice((n % 3) * Q_BLOCK, (n % 3 + 1) * Q_BLOCK)
    for n in range(len(items) + 2):
        if n < len(items):
            a, h = items[n]
            qp = q_ref[0, a * Q_BLOCK:(a + 1) * Q_BLOCK, (h // 2) * LANES:(h // 2 + 1) * LANES]
            s_ref[slot(n), :] = jnp.where(oks[a], _dot_t(_head_half(qp, low, h % 2), keys[a]), NEG)
        if 1 <= n <= len(items):
            a, h = items[n - 1]
            sink = jnp.full((Q_BLOCK, 1), sink_ref[GQA_HEAD_PERM[h]], F32)
            invs[a, h] = _softmax_rows(s_ref.at[slot(n - 1)], p_ref.at[slot(n - 1)], sink)
        if n >= 2:
            a, h = items[n - 2]
            outs[a, h] = _dot(p_ref[slot(n - 2), :], vals[a]) * invs[a, h]
            if h % 2 == 1:
                o_ref[0, a * Q_BLOCK:(a + 1) * Q_BLOCK, (h // 2) * LANES:(h // 2 + 1) * LANES] = jnp.where(
                    low, outs[a, h - 1], outs[a, h]).astype(BF16)


def _gqa(sink, q, k, v, *, n_lat, n_ctx_blocks, ctx_out, c_len):
    b = q.shape[0]
    nq = n_lat + (n_ctx_blocks if ctx_out else 0)
    assert n_lat % GQA_QBLOCKS == 0 and n_ctx_blocks % GQA_QBLOCKS == 0
    ctx_idx = n_lat * Q_BLOCK // c_len
    n_keys = 3 * Q_BLOCK + c_len
    qt = GQA_QBLOCKS * Q_BLOCK

    def kv_spec(t):
        return pl.BlockSpec((1, Q_BLOCK, GQA_KV_WIDTH),
                            lambda bb, g: (bb, jnp.clip(g * GQA_QBLOCKS - 1 + t, 0, n_lat - 1), 0))

    kv = [kv_spec(t) for t in range(GQA_QBLOCKS + 2)]
    kv.append(pl.BlockSpec((1, c_len, GQA_KV_WIDTH), lambda bb, g: (bb, ctx_idx, 0)))
    return pl.pallas_call(
        functools.partial(_gqa_kernel, n_lat=n_lat),
        grid=(b, nq // GQA_QBLOCKS),
        in_specs=[pl.BlockSpec(memory_space=pltpu.SMEM),
                  pl.BlockSpec((1, qt, GQA_WIDTH), lambda bb, g: (bb, g, 0))] + kv + kv,
        out_specs=pl.BlockSpec((1, qt, GQA_WIDTH), lambda bb, g: (bb, g, 0)),
        out_shape=jax.ShapeDtypeStruct((b, nq * Q_BLOCK, GQA_WIDTH), BF16),
        scratch_shapes=[pltpu.VMEM((3 * Q_BLOCK, n_keys), F32), pltpu.VMEM((3 * Q_BLOCK, n_keys), BF16)],
        compiler_params=_params(2),
        name="window_gqa",
    )(sink, q, *([k] * (GQA_QBLOCKS + 3)), *([v] * (GQA_QBLOCKS + 3)))


def _na_band_start(r, rows):
    return jnp.clip(r - NA_ROWS // 2, 0, rows - NA_ROWS)


def _na_union_start(g, rows):
    return jnp.minimum(_na_band_start(g * NA_QROWS, rows), rows - NA_UNION)


def _na_kernel(q_ref, kb_ref, vb_ref, kx_ref, vx_ref, bias_ref, o_ref, s_ref, p_ref, *, rows):
    g = pl.program_id(1)
    n_loc = NA_ROWS * GRID_W
    low = lax.broadcasted_iota(jnp.int32, (GRID_W, LANES), 1) < HEAD_DIM
    n_pairs = NA_WIDTH // LANES
    u0 = _na_union_start(g, rows)
    offs, dr0s, pens = [], [], []
    for rho in range(NA_QROWS):
        r = g * NA_QROWS + rho
        is_ctx = r >= rows
        band = _na_band_start(r, rows)
        offs.append(pl.multiple_of(jnp.where(is_ctx, 0, band - u0) * GRID_W, GRID_W))
        dr0s.append(jnp.where(is_ctx, 0, (NA_ROWS - 1) - (r - band)))
        pens.append(jnp.where(is_ctx, NEG, 0.0).astype(F32))
    items = [(rho, j) for rho in range(NA_QROWS) for j in range(n_pairs)]
    rows2 = 2 * GRID_W
    slot = lambda n: slice((n % 3) * rows2, (n % 3 + 1) * rows2)
    invs = {}
    for n in range(len(items) + 2):
        if n < len(items):
            rho, j = items[n]
            sl = slice(j * LANES, (j + 1) * LANES)
            qp = q_ref[0, rho * GRID_W:(rho + 1) * GRID_W, sl]
            keys = jnp.concatenate([kb_ref[0, pl.ds(offs[rho], n_loc), sl], kx_ref[0, :, sl]], axis=0)
            s2 = _dot_t(jnp.concatenate([_head_half(qp, low, 0), _head_half(qp, low, 1)], axis=0), keys)
            bias = jnp.concatenate(
                [jnp.concatenate([bias_ref[2 * j + half, dr0s[rho] + i] for i in range(0, NA_ROWS, 2)], axis=1)
                 for half in range(2)], axis=0)
            s_ref[slot(n), 0:n_loc] = s2[:, 0:n_loc] + (bias + pens[rho])
            s_ref[slot(n), n_loc:] = s2[:, n_loc:]
        if 1 <= n <= len(items):
            invs[n - 1] = _softmax_rows(s_ref.at[slot(n - 1)], p_ref.at[slot(n - 1)])
        if n >= 2:
            rho, j = items[n - 2]
            sl = slice(j * LANES, (j + 1) * LANES)
            vals = jnp.concatenate([vb_ref[0, pl.ds(offs[rho], n_loc), sl], vx_ref[0, :, sl]], axis=0)
            o2 = _dot(p_ref[slot(n - 2), :], vals) * invs[n - 2]
            o_ref[0, rho * GRID_W:(rho + 1) * GRID_W, sl] = jnp.where(low, o2[:GRID_W], o2[GRID_W:]).astype(BF16)


def _na(q, k, v, bias, *, rows, n_ctx_rows, ctx_out, c_len):
    b = q.shape[0]
    assert rows % NA_QROWS == 0 and n_ctx_rows % NA_QROWS == 0 and rows >= NA_UNION
    ng = (rows + (n_ctx_rows if ctx_out else 0)) // NA_QROWS
    qt = NA_QROWS * GRID_W
    ctx_idx = rows * GRID_W // c_len
    n_keys = NA_ROWS * GRID_W + c_len
    union_spec = pl.BlockSpec((pl.Element(1), pl.Element(NA_UNION * GRID_W), pl.Element(NA_WIDTH)),
                              lambda bb, g: (bb, _na_union_start(g, rows) * GRID_W, 0))
    ctx_spec = pl.BlockSpec((1, c_len, NA_WIDTH), lambda bb, g: (bb, ctx_idx, 0))
    return pl.pallas_call(
        functools.partial(_na_kernel, rows=rows),
        grid=(b, ng),
        in_specs=[pl.BlockSpec((1, qt, NA_WIDTH), lambda bb, g: (bb, g, 0)),
                  union_spec, union_spec, ctx_spec, ctx_spec, _resident(bias.shape)],
        out_specs=pl.BlockSpec((1, qt, NA_WIDTH), lambda bb, g: (bb, g, 0)),
        out_shape=jax.ShapeDtypeStruct((b, ng * qt, NA_WIDTH), BF16),
        scratch_shapes=[pltpu.VMEM((3 * 2 * GRID_W, n_keys), F32), pltpu.VMEM((3 * 2 * GRID_W, n_keys), BF16)],
        compiler_params=_params(2),
        name="neighborhood_attn",
    )(q, k, v, k, v, bias)


def _na_bias_pairs(rpb):
    cols = np.arange(GRID_W)
    col_start = np.clip(cols - NA_COLS // 2, 0, GRID_W - NA_COLS)
    kc = np.arange(GRID_W)
    in_win = (kc[None, :] >= col_start[:, None]) & (kc[None, :] < col_start[:, None] + NA_COLS)
    col_idx = np.clip(kc[None, :] - cols[:, None] + (NA_COLS - 1), 0, 2 * NA_COLS - 2)
    tab = jnp.where(in_win[None, None], rpb.astype(F32)[:, :, col_idx], NEG)
    return jnp.concatenate([tab[:, :-1], tab[:, 1:]], axis=-1)


def _s5_prep_kernel(are_ref, aim_ref, ldt_ref, bre_ref, bim_ref, lbr_ref, lbi_ref, bbr_ref, bbi_ref):
    are, aim = are_ref[...], aim_ref[...]
    dt = jnp.exp(ldt_ref[...])
    mag = jnp.exp(are * dt)
    lbr, lbi = mag * jnp.cos(aim * dt), mag * jnp.sin(aim * dt)
    lbr_ref[...] = lbr
    lbi_ref[...] = lbi
    nr, ni = lbr - 1.0, lbi
    den = are * are + aim * aim
    cr, ci = (nr * are + ni * aim) / den, (ni * are - nr * aim) / den
    br, bi = bre_ref[...], bim_ref[...]
    bbr_ref[...] = cr * br - ci * bi
    bbi_ref[...] = cr * bi + ci * br


def _s5_prep(a_re, a_im, log_dt, b_re, b_im):
    nd, g, p = a_re.shape
    i = b_re.shape[-1]
    n = nd * g * p
    col = lambda z: z.reshape(n, 1)
    ldt = jnp.broadcast_to(log_dt[:, :, None], (nd, g, p))
    whole = lambda w: pl.BlockSpec((n, w), lambda: (0, 0))
    lbr, lbi, bbr, bbi = pl.pallas_call(
        _s5_prep_kernel,
        in_specs=[whole(1)] * 3 + [whole(i)] * 2,
        out_specs=[whole(1)] * 2 + [whole(i)] * 2,
        out_shape=[jax.ShapeDtypeStruct((n, 1), F32)] * 2 + [jax.ShapeDtypeStruct((n, i), F32)] * 2,
        name="s5_prep",
    )(col(a_re), col(a_im), col(ldt), b_re.reshape(n, i), b_im.reshape(n, i))
    return lbr.reshape(nd, g, p), lbi.reshape(nd, g, p), bbr.reshape(nd, g, p, i), bbi.reshape(nd, g, p, i)


def _s5_block_operands(lbr, lbi, bbr, bbi, c_re, c_im):
    nd, g, p = lbr.shape
    i = bbr.shape[-1]
    nb, gb = SSM_BLOCKS, SSM_BLOCK_GROUPS
    eye = jnp.eye(gb, dtype=F32)

    def in_map(z):
        z = z.reshape(nd, nb, gb, p, i).transpose(0, 1, 2, 4, 3)
        return (z[:, :, :, :, None, :] * eye[None, None, :, None, :, None]).reshape(nd, nb, gb * i, gb * p)

    def out_map(z):
        z = z.reshape(nd, nb, gb, i, p).transpose(0, 1, 2, 4, 3)
        return (z[:, :, :, :, None, :] * eye[None, None, :, None, :, None]).reshape(nd, nb, gb * p, gb * i)

    lam = jnp.stack([lbr.reshape(nd, nb, gb * p), lbi.reshape(nd, nb, gb * p)], axis=2)
    w_in = jnp.concatenate([in_map(bbr), in_map(bbi)], axis=-1).astype(BF16)
    w_out = jnp.concatenate([out_map(c_re.astype(F32)), -out_map(c_im.astype(F32))], axis=2).astype(BF16)
    return lam, w_in, w_out


def _s5_kernel(u_ref, lam_ref, win_ref, wout_ref, y_ref, st_ref, carry_ref, *, nb, tt, reverse):
    ns = SSM_BLOCK_STATE
    rows = tt * nb
    n_pipe = SSM_PIPE_TILES

    @pl.when(pl.program_id(0) == 0)
    def _():
        carry_ref[...] = jnp.zeros_like(carry_ref)

    lr = [jnp.broadcast_to(lam_ref[j, 0:1, :], (nb, ns)) for j in range(SSM_BLOCKS)]
    li = [jnp.broadcast_to(lam_ref[j, 1:2, :], (nb, ns)) for j in range(SSM_BLOCKS)]
    order = list(range(n_pipe))[::-1] if reverse else list(range(n_pipe))
    steps = list(range(tt))[::-1] if reverse else list(range(tt))
    carry = [carry_ref[k] for k in range(2 * SSM_BLOCKS)]
    for i in range(n_pipe + 2):
        if i < n_pipe:
            trows = slice(order[i] * rows, (order[i] + 1) * rows)
            for j in range(SSM_BLOCKS):
                st_ref[i % 3, j] = _dot(u_ref[j, trows, :].astype(BF16), win_ref[j])
        if 1 <= i <= n_pipe:
            buf = (i - 1) % 3
            for t in steps:
                r = slice(t * nb, (t + 1) * nb)
                for j in range(SSM_BLOCKS):
                    sr, si = carry[2 * j], carry[2 * j + 1]
                    nr = lr[j] * sr - li[j] * si + st_ref[buf, j, r, 0:ns]
                    ni = lr[j] * si + li[j] * sr + st_ref[buf, j, r, ns:2 * ns]
                    st_ref[buf, j, r, 0:ns] = nr
                    st_ref[buf, j, r, ns:2 * ns] = ni
                    carry[2 * j], carry[2 * j + 1] = nr, ni
        if i >= 2:
            trows = slice(order[i - 2] * rows, (order[i - 2] + 1) * rows)
            for j in range(SSM_BLOCKS):
                y_ref[j, trows, :] = _dot(st_ref[(i - 2) % 3, j].astype(BF16), wout_ref[j])
    for k in range(2 * SSM_BLOCKS):
        carry_ref[k] = carry[k]


def _s5_scan(u2, lam, w_in, w_out, *, direction, nb, n_lat, n_ctx):
    _, rows_total, _ = u2.shape
    tt = SSM_TILE
    span = SSM_PIPE_TILES * tt
    assert n_lat % span == 0 and n_ctx % span == 0
    n_lat_g, n_ctx_g = n_lat // span, n_ctx // span
    n_g = n_lat_g + n_ctx_g
    ns2 = 2 * SSM_BLOCK_STATE
    reverse = direction == 1

    def group(k):
        if reverse:
            return jnp.where(k < n_ctx_g, n_g - 1 - k, n_lat_g - 1 - (k - n_ctx_g))
        return jnp.where(k < n_ctx_g, n_lat_g + k, k - n_ctx_g)

    def par(*tail):
        return pl.BlockSpec((None,) + tail, lambda k: (direction,) + (0,) * len(tail))

    stream = pl.BlockSpec((SSM_BLOCKS, span * nb, LANES), lambda k: (0, group(k), 0))
    return pl.pallas_call(
        functools.partial(_s5_kernel, nb=nb, tt=tt, reverse=reverse),
        grid=(n_g,),
        in_specs=[stream, par(SSM_BLOCKS, 2, SSM_BLOCK_STATE), par(SSM_BLOCKS, LANES, ns2),
                  par(SSM_BLOCKS, ns2, LANES)],
        out_specs=stream,
        out_shape=jax.ShapeDtypeStruct((SSM_BLOCKS, rows_total, LANES), F32),
        scratch_shapes=[pltpu.VMEM((3, SSM_BLOCKS, tt * nb, ns2), F32),
                        pltpu.VMEM((2 * SSM_BLOCKS, nb, SSM_BLOCK_STATE), F32)],
        compiler_params=_params(1),
        name="s5_scan_bwd" if reverse else "s5_scan_fwd",
    )(u2, lam, w_in, w_out)


def _merge_kernel(x_ref, mod_ref, g_ref, ysf_ref, ysb_ref, u_ref, ya_ref, yn_ref, d_ref,
                  wgt_ref, wglu_ref, wps_ref, wpa_ref, wpn_ref, wo_ref, o_ref):
    d = x_ref.shape[2]
    tm = x_ref.shape[1]
    nb = u_ref.shape[1] // tm
    xs = [x_ref[sub] for sub in range(TOKEN_BATCH)]
    h = _stacked_mod_norm(xs, g_ref[1:2, :], mod_ref, 3)
    gate = lambda n: jax.nn.sigmoid(_dot(h, wgt_ref[:, n * d:(n + 1) * d]))

    def from_slabs(load):
        return jnp.concatenate([jnp.concatenate([load(blk, _slab_rows(sub, tm, nb)) for blk in range(SSM_BLOCKS)],
                                                axis=1) for sub in range(TOKEN_BATCH)], axis=0)

    y = from_slabs(lambda blk, rows: ysf_ref[blk, rows, :] + ysb_ref[blk, rows, :])
    y = y + d_ref[...] * from_slabs(lambda blk, rows: u_ref[blk, rows, :])
    t = 0.5 * y * (1.0 + jnp.tanh(math.sqrt(2.0 / math.pi) * (y + 0.044715 * (y * y * y))))
    glu = (t * jax.nn.sigmoid(_dot(t.astype(BF16), wglu_ref[...]))).astype(BF16)
    stacked = lambda ref: jnp.concatenate([ref[sub] for sub in range(TOKEN_BATCH)], axis=0)
    m = (gate(0) * _dot(glu, wps_ref[...]) + gate(1) * _dot(stacked(ya_ref), wpa_ref[...])
         + gate(2) * _dot(stacked(yn_ref), wpn_ref[...]))
    out = _dot(m.astype(BF16), wo_ref[...])
    for sub, x in enumerate(xs):
        o_ref[sub] = x + mod_ref[sub, 5:6, :] * _rows(out, sub, tm)


def _merge(xs, mod, g, ysf, ysb, u2, ya, yn, ssm_d, wgt, wglu, wps, wpa, wpn, wo, *, li, n_tok, n_lat_tiles,
           ctx_group):
    b, _, d = xs.shape
    tm = TOKEN_TILE
    return pl.pallas_call(
        _merge_kernel,
        grid=(n_tok // tm, b // TOKEN_BATCH),
        in_specs=[_tok_spec(d, tm), _mod_spec(d, n_lat_tiles, ctx_group), _resident(g.shape),
                  _slab_spec(tm, b), _slab_spec(tm, b), _slab_spec(tm, b),
                  _tok_spec(GQA_WIDTH, tm), _tok_spec(NA_WIDTH, tm), _resident((1, SSM_WIDTH)),
                  _layer_resident(wgt, li), _layer_resident(wglu, li), _layer_resident(wps, li),
                  _layer_resident(wpa, li), _layer_resident(wpn, li), _layer_resident(wo, li)],
        out_specs=_tok_spec(d, tm),
        out_shape=jax.ShapeDtypeStruct((b, n_tok, d), F32),
        compiler_params=_params(2),
        name="mixer_merge",
    )(xs, mod, g, ysf, ysb, u2, ya, yn, ssm_d.reshape(1, SSM_WIDTH), wgt, wglu, wps, wpa, wpn, wo)


def _rope_tables(l, c_len):
    t = jnp.arange(l)
    pos = jnp.stack([t // GRID_W, t % GRID_W], axis=-1).astype(F32)
    half = HEAD_DIM // 2
    inv = ROPE_THETA ** (-jnp.arange(0, half, 2, dtype=F32) / half)
    ang = pos[:, :, None] * inv
    cos, sin = jnp.cos(ang), jnp.sin(ang)
    zero = jnp.zeros_like(sin[:, 0])
    cos_h = jnp.concatenate([cos[:, 0], cos[:, 0], cos[:, 1], cos[:, 1]], axis=-1)
    sa_h = jnp.concatenate([zero, sin[:, 0], zero, sin[:, 1]], axis=-1)
    sb_h = jnp.concatenate([-sin[:, 0], zero, -sin[:, 1], zero], axis=-1)

    def full(tab, ctx_fill):
        tab = jnp.concatenate([tab, tab], axis=-1)
        return jnp.concatenate([tab, jnp.full((c_len, LANES), ctx_fill, F32)], axis=0)

    return full(cos_h, 1.0), full(sa_h, 0.0), full(sb_h, 0.0)


def kernel(x, c, ctx, c_ctx, w_ada, b_ada, norm_g, ffn1_wg, ffn1_wu, ffn1_wd, ffn2_wg, ffn2_wu, ffn2_wd, w_in,
           ssm_a_re, ssm_a_im, ssm_log_dt, ssm_b_re, ssm_b_im, ssm_c_re, ssm_c_im, ssm_d, ssm_w_glu, gqa_sink,
           na_rpb, w_p_ssm, w_p_gqa, w_p_na, w_out, final_g):
    b, l, d = x.shape
    c_len = ctx.shape[1]
    s = l + c_len
    depth = w_ada.shape[0]
    tm = TOKEN_TILE
    assert b + TOKEN_BATCH <= MOD_ROWS and b % TOKEN_BATCH == 0 and l % tm == 0 and c_len % tm == 0 and l % (GRID_W * NA_ROWS) == 0
    assert c_len % Q_BLOCK == 0 and l % c_len == 0 and b % SUBLANES == 0
    assert l % SSM_TILE == 0 and c_len % SSM_TILE == 0
    n_lat_tiles = l // tm

    cin = jnp.zeros((MOD_ROWS, d), F32).at[:b].set(c).at[b:b + TOKEN_BATCH].set(c_ctx)
    mods = _ada(cin, w_ada, b_ada).reshape(depth, MOD_ROWS, N_MOD, d)
    cos_t, sa_t, sb_t = _rope_tables(l, c_len)

    perm = np.asarray(GQA_HEAD_PERM)
    q0 = SSM_WIDTH
    w_q = w_in[:, :, q0:q0 + GQA_WIDTH].reshape(depth, d, GQA_HEADS, HEAD_DIM)[:, :, perm]
    w_in_p = jnp.concatenate([w_in[:, :, :q0], w_q.reshape(depth, d, GQA_WIDTH),
                              w_in[:, :, q0 + GQA_WIDTH:MIXER_COLS]], axis=-1).astype(BF16)
    w_gt = w_in[:, :, MIXER_COLS:].astype(BF16)
    w_pa = w_p_gqa.reshape(depth, GQA_HEADS, HEAD_DIM, d)[:, perm].reshape(depth, GQA_WIDTH, d).astype(BF16)
    bf = lambda z: z.astype(BF16)
    f1 = (bf(ffn1_wg), bf(ffn1_wu), bf(ffn1_wd))
    f2 = (bf(ffn2_wg), bf(ffn2_wu), bf(ffn2_wd))
    w_glu, w_ps, w_pn, w_o = bf(ssm_w_glu), bf(w_p_ssm), bf(w_p_na), bf(w_out)

    tile_kw = dict(n_lat_tiles=n_lat_tiles, ctx_group=b // TOKEN_BATCH)
    xs = x
    for li in range(depth):
        ctx_out = li < depth - 1
        n_tok = s if ctx_out else l
        mod, g = mods[li], norm_g[li]
        xs = _ffn(xs, mod, g, *f1, li=li, k0=0, gi=0, n_tok=s, ctx=ctx if li == 0 else None, **tile_kw)
        u2, q, k, v, nq, nk, nv = _inproj(xs, mod, g, w_in_p, cos_t, sa_t, sb_t, li=li, **tile_kw)
        lam, s_in, s_out = _s5_block_operands(
            *_s5_prep(ssm_a_re[li], ssm_a_im[li], ssm_log_dt[li], ssm_b_re[li], ssm_b_im[li]),
            ssm_c_re[li], ssm_c_im[li])
        ysf, ysb = (_s5_scan(u2, lam, s_in, s_out, direction=dd, nb=b, n_lat=l, n_ctx=c_len) for dd in range(2))
        ya = _gqa(gqa_sink[li], q, k, v, n_lat=l // Q_BLOCK, n_ctx_blocks=c_len // Q_BLOCK, ctx_out=ctx_out,
                  c_len=c_len)
        yn = _na(nq, nk, nv, _na_bias_pairs(na_rpb[li]), rows=l // GRID_W, n_ctx_rows=c_len // GRID_W,
                 ctx_out=ctx_out, c_len=c_len)
        xs = _merge(xs, mod, g, ysf, ysb, u2, ya, yn, ssm_d[li], w_gt, w_glu, w_ps, w_pa, w_pn, w_o,
                    li=li, n_tok=n_tok, **tile_kw)
        xs = _ffn(xs, mod, g, *f2, li=li, k0=6, gi=2, n_tok=n_tok, final_g=None if ctx_out else final_g, **tile_kw)
    return xs
```

```python
import functools
import math

import jax
import jax.numpy as jnp
import numpy as np
from jax import lax
from jax.experimental import pallas as pl
from jax.experimental.pallas import tpu as pltpu

F32 = jnp.float32
BF16 = jnp.bfloat16

EPS = 1e-6
NEG = -1e30
N_MOD = 9
GRID_W = 64
HEAD_DIM = 64
GQA_HEADS = 8
GQA_KV_HEADS = 2
NA_HEADS = 8
NA_ROWS = 8
NA_COLS = 16
Q_BLOCK = 128
ROPE_THETA = 10000.0
SSM_GROUP = 16
SSM_GROUPS = 24
SSM_STATE = 64
SSM_WIDTH = SSM_GROUP * SSM_GROUPS
GQA_WIDTH = GQA_HEADS * HEAD_DIM
GQA_KV_WIDTH = GQA_KV_HEADS * HEAD_DIM
NA_WIDTH = NA_HEADS * HEAD_DIM
N_BRANCH = 3
MIXER_COLS = SSM_WIDTH + GQA_WIDTH + 2 * GQA_KV_WIDTH + 3 * NA_WIDTH

LANES = 128
SUBLANES = 8
MOD_ROWS = 16
TOKEN_TILE = 256
TOKEN_BATCH = 2
FFN_CHUNK = 256
GQA_QBLOCKS = 2
NA_QROWS = 4
NA_UNION = NA_QROWS + NA_ROWS - 1
SSM_TILE = 64
SSM_PIPE_TILES = 4
SSM_BLOCK_GROUPS = LANES // SSM_GROUP
SSM_BLOCKS = SSM_GROUPS // SSM_BLOCK_GROUPS
SSM_BLOCK_STATE = SSM_BLOCK_GROUPS * SSM_STATE
VMEM_LIMIT = 56 * 1024 * 1024
GQA_HEAD_PERM = (0, 4, 1, 5, 2, 6, 3, 7)


def _params(n_axes):
    return pltpu.CompilerParams(dimension_semantics=("arbitrary",) * n_axes, vmem_limit_bytes=VMEM_LIMIT)


def _resident(shape):
    return pl.BlockSpec(shape, lambda *_: (0,) * len(shape), pipeline_mode=pl.Buffered(1))


def _layer_resident(stack, li):
    nd = stack.ndim
    return pl.BlockSpec((None,) + stack.shape[1:], lambda *_: (li,) + (0,) * (nd - 1),
                        pipeline_mode=pl.Buffered(1))


def _dot(a, b):
    return jnp.dot(a, b, preferred_element_type=F32)


def _dot_t(a, b):
    return lax.dot_general(a, b, (((1,), (1,)), ((), ())), preferred_element_type=F32)


def _rmsnorm(x, g):
    return x * lax.rsqrt(jnp.mean(x * x, axis=-1, keepdims=True) + EPS) * g


def _mod_norm(x, g, shift, scale):
    return _rmsnorm(x, g) * (1.0 + scale) + shift


def _ada_kernel(c_ref, w_ref, b_ref, o_ref):
    s = c_ref[...]
    s = s * jax.nn.sigmoid(s)
    o_ref[0] = _dot(s.astype(BF16), w_ref[0].astype(BF16)) + b_ref[0]


def _ada(cin, w_ada, b_ada):
    depth, d, n = w_ada.shape
    tn = n // 4
    return pl.pallas_call(
        _ada_kernel,
        grid=(depth, n // tn),
        in_specs=[
            pl.BlockSpec((MOD_ROWS, d), lambda l, j: (0, 0)),
            pl.BlockSpec((1, d, tn), lambda l, j: (l, 0, j)),
            pl.BlockSpec((1, 1, tn), lambda l, j: (l, 0, j)),
        ],
        out_specs=pl.BlockSpec((1, MOD_ROWS, tn), lambda l, j: (l, 0, j)),
        out_shape=jax.ShapeDtypeStruct((depth, MOD_ROWS, n), F32),
        compiler_params=_params(2),
        name="ada_mod",
    )(cin, w_ada, b_ada.reshape(depth, 1, n))


def _tok_spec(width, tm):
    return pl.BlockSpec((TOKEN_BATCH, tm, width), lambda j, b: (b, j, 0))


def _rows(val, sub, tm):
    return val[sub * tm:(sub + 1) * tm]


def _slab_rows(sub, tm, nb):
    return pl.ds(pl.program_id(1) * TOKEN_BATCH + sub, tm, stride=nb)


def _slab_spec(tm, nb):
    return pl.BlockSpec((SSM_BLOCKS, tm * nb, LANES), lambda j, b: (0, j, 0))


def _mod_spec(d, n_lat_tiles, ctx_group):
    return pl.BlockSpec((TOKEN_BATCH, N_MOD, d), lambda j, b: (jnp.where(j >= n_lat_tiles, ctx_group, b), 0, 0))


def _stacked_mod_norm(xs, g, mod_ref, shift_row):
    return jnp.concatenate(
        [_mod_norm(x, g, mod_ref[sub, shift_row:shift_row + 1, :], mod_ref[sub, shift_row + 1:shift_row + 2, :])
         for sub, x in enumerate(xs)], axis=0).astype(BF16)


def _ffn_kernel(*refs, k0, gi, final, n_lat_tiles, split_in):
    refs = list(refs)
    x_ref = refs.pop(0)
    ctx_ref = refs.pop(0) if split_in else None
    mod_ref, g_ref, wg_ref, wu_ref, wd_ref = refs[:5]
    fg_ref = refs[5] if final else None
    o_ref, t_ref = refs[-2:]
    tm = x_ref.shape[1]
    xs = [x_ref[sub] for sub in range(TOKEN_BATCH)]
    if split_in:
        is_ctx = pl.program_id(0) >= n_lat_tiles
        xs = [jnp.where(is_ctx, ctx_ref[sub], x) for sub, x in enumerate(xs)]
    h = _stacked_mod_norm(xs, g_ref[gi:gi + 1, :], mod_ref, k0)
    f = wg_ref.shape[1]
    for c0 in range(0, f, FFN_CHUNK):
        a = _dot(h, wg_ref[:, c0:c0 + FFN_CHUNK])
        u = _dot(h, wu_ref[:, c0:c0 + FFN_CHUNK])
        t_ref[:, c0:c0 + FFN_CHUNK] = (a * jax.nn.sigmoid(a) * u).astype(BF16)
    y = _dot(t_ref[...], wd_ref[...])
    for sub, x in enumerate(xs):
        out = x + (0.5 * mod_ref[sub, k0 + 2:k0 + 3, :]) * _rows(y, sub, tm)
        if final:
            out = _rmsnorm(out, fg_ref[...])
        o_ref[sub] = out


def _ffn(xs, mod, g, wg, wu, wd, *, li, k0, gi, n_tok, n_lat_tiles, ctx_group, final_g=None, ctx=None):
    b, _, d = xs.shape
    f = wg.shape[2]
    tm = TOKEN_TILE
    final = final_g is not None
    split_in = ctx is not None
    if split_in:
        in_specs = [pl.BlockSpec((TOKEN_BATCH, tm, d), lambda j, bb: (bb, jnp.minimum(j, n_lat_tiles - 1), 0)),
                    pl.BlockSpec((TOKEN_BATCH, tm, d), lambda j, bb: (bb, jnp.maximum(j - n_lat_tiles, 0), 0))]
        args = [xs, ctx]
    else:
        in_specs = [_tok_spec(d, tm)]
        args = [xs]
    in_specs += [_mod_spec(d, n_lat_tiles, ctx_group), _resident(g.shape),
                 _layer_resident(wg, li), _layer_resident(wu, li), _layer_resident(wd, li)]
    args += [mod, g, wg, wu, wd]
    if final:
        in_specs.append(_resident((1, d)))
        args.append(final_g.reshape(1, d))
    return pl.pallas_call(
        functools.partial(_ffn_kernel, k0=k0, gi=gi, final=final, n_lat_tiles=n_lat_tiles, split_in=split_in),
        grid=(n_tok // tm, b // TOKEN_BATCH),
        in_specs=in_specs,
        out_specs=_tok_spec(d, tm),
        out_shape=jax.ShapeDtypeStruct((b, n_tok, d), F32),
        scratch_shapes=[pltpu.VMEM((TOKEN_BATCH * tm, f), BF16)],
        compiler_params=_params(2),
        name="ffn_half_final" if final else "ffn_half",
    )(*args)


def _inproj_kernel(x_ref, mod_ref, g_ref, w_ref, cos_ref, sa_ref, sb_ref,
                   u_ref, q_ref, k_ref, v_ref, nq_ref, nk_ref, nv_ref):
    tm = x_ref.shape[1]
    nb = u_ref.shape[1] // tm
    h = _stacked_mod_norm([x_ref[sub] for sub in range(TOKEN_BATCH)], g_ref[1:2, :], mod_ref, 3)
    cos, sa, sb = (jnp.concatenate([t[...]] * TOKEN_BATCH, axis=0) for t in (cos_ref, sa_ref, sb_ref))
    scale = HEAD_DIM ** -0.5

    def mm(c0, n):
        return _dot(h, w_ref[:, c0:c0 + n])

    def rope(z):
        return z * cos + pltpu.roll(z, 16, 1) * sa + pltpu.roll(z, LANES - 16, 1) * sb

    def put(ref, val, lanes=slice(None)):
        for sub in range(TOKEN_BATCH):
            ref[sub, :, lanes] = _rows(val, sub, tm)

    c0 = SSM_WIDTH + GQA_WIDTH + 2 * GQA_KV_WIDTH
    z = mm(0, c0)
    col = lambda n: z[:, n * LANES:(n + 1) * LANES]
    n = 0
    for blk in range(SSM_BLOCKS):
        for sub in range(TOKEN_BATCH):
            u_ref[blk, _slab_rows(sub, tm, nb), :] = _rows(col(n), sub, tm)
        n += 1
    for j in range(GQA_WIDTH // LANES):
        put(q_ref, (rope(col(n)) * scale).astype(BF16), slice(j * LANES, (j + 1) * LANES))
        n += 1
    put(k_ref, rope(col(n)).astype(BF16))
    put(v_ref, col(n + 1).astype(BF16))
    put(nq_ref, (mm(c0, NA_WIDTH) * scale).astype(BF16))
    c0 += NA_WIDTH
    put(nk_ref, mm(c0, NA_WIDTH).astype(BF16))
    c0 += NA_WIDTH
    put(nv_ref, mm(c0, NA_WIDTH).astype(BF16))


def _inproj(xs, mod, g, w_in, cos_t, sa_t, sb_t, *, li, n_lat_tiles, ctx_group):
    b, s, d = xs.shape
    tm = TOKEN_TILE
    assert w_in.shape[2] == MIXER_COLS
    widths = (GQA_WIDTH, GQA_KV_WIDTH, GQA_KV_WIDTH, NA_WIDTH, NA_WIDTH, NA_WIDTH)
    dtypes = (BF16,) * len(widths)
    rope_spec = pl.BlockSpec((tm, LANES), lambda j, bb: (j, 0))
    return pl.pallas_call(
        _inproj_kernel,
        grid=(s // tm, b // TOKEN_BATCH),
        in_specs=[_tok_spec(d, tm), _mod_spec(d, n_lat_tiles, ctx_group), _resident(g.shape),
                  _layer_resident(w_in, li), rope_spec, rope_spec, rope_spec],
        out_specs=[_slab_spec(tm, b)] + [_tok_spec(w, tm) for w in widths],
        out_shape=[jax.ShapeDtypeStruct((SSM_BLOCKS, s * b, LANES), F32)]
                  + [jax.ShapeDtypeStruct((b, s, w), dt) for w, dt in zip(widths, dtypes)],
        compiler_params=_params(2),
        name="mixer_in_proj",
    )(xs, mod, g, w_in, cos_t, sa_t, sb_t)


def _softmax_rows(s_ref, p_ref, extra=None):
    s = s_ref[...]
    m = jnp.max(s, axis=1, keepdims=True)
    if extra is not None:
        m = jnp.maximum(m, extra)
    e = jnp.exp(s - m)
    den = jnp.sum(e, axis=1, keepdims=True)
    if extra is not None:
        den = den + jnp.exp(extra - m)
    p_ref[...] = e.astype(BF16)
    return 1.0 / den


def _head_half(qp, low, half):
    return jnp.where(low if half == 0 else jnp.logical_not(low), qp, jnp.zeros_like(qp))


def _gqa_kernel(sink_ref, q_ref, *refs, n_lat):
    nkb = GQA_QBLOCKS + 2
    k_refs, kx_ref = refs[:nkb], refs[nkb]
    v_refs, vx_ref = refs[nkb + 1:2 * nkb + 1], refs[2 * nkb + 1]
    o_ref, s_ref, p_ref = refs[2 * nkb + 2:]
    n_keys = s_ref.shape[1]
    far = 8 * n_keys
    row = lax.broadcasted_iota(jnp.int32, (Q_BLOCK, n_keys), 0)
    col = lax.broadcasted_iota(jnp.int32, (Q_BLOCK, n_keys), 1)
    low = lax.broadcasted_iota(jnp.int32, (Q_BLOCK, LANES), 1) < HEAD_DIM
    oks, keys, vals = [], [], []
    for a in range(GQA_QBLOCKS):
        i = pl.program_id(1) * GQA_QBLOCKS + a
        is_lat = i < n_lat
        off_p = jnp.where(jnp.logical_and(is_lat, i > 0), 0, far)
        off_c = jnp.where(is_lat, 0, far)
        off_n = jnp.where(jnp.logical_and(is_lat, i < n_lat - 1), 0, far)
        oks.append((col >= row + off_p) & (col < Q_BLOCK)
                   | (col >= Q_BLOCK + off_c) & (col < 2 * Q_BLOCK)
                   | (col >= 2 * Q_BLOCK) & (col <= 2 * Q_BLOCK + row - off_n)
                   | (col >= 3 * Q_BLOCK))
        keys.append(jnp.concatenate([k_refs[a][0], k_refs[a + 1][0], k_refs[a + 2][0], kx_ref[0]], axis=0))
        vals.append(jnp.concatenate([v_refs[a][0], v_refs[a + 1][0], v_refs[a + 2][0], vx_ref[0]], axis=0))
    items = [(a, h) for a in range(GQA_QBLOCKS) for h in range(GQA_HEADS)]
    invs, outs = {}, {}
    slot = lambda n: slice((n % 3) * Q_BLOCK, (n % 3 + 1) * Q_BLOCK)
    for n in range(len(items) + 2):
        if n < len(items):
            a, h = items[n]
            qp = q_ref[0, a * Q_BLOCK:(a + 1) * Q_BLOCK, (h // 2) * LANES:(h // 2 + 1) * LANES]
            s_ref[slot(n), :] = jnp.where(oks[a], _dot_t(_head_half(qp, low, h % 2), keys[a]), NEG)
        if 1 <= n <= len(items):
            a, h = items[n - 1]
            sink = jnp.full((Q_BLOCK, 1), sink_ref[GQA_HEAD_PERM[h]], F32)
            invs[a, h] = _softmax_rows(s_ref.at[slot(n - 1)], p_ref.at[slot(n - 1)], sink)
        if n >= 2:
            a, h = items[n - 2]
            outs[a, h] = _dot(p_ref[slot(n - 2), :], vals[a]) * invs[a, h]
            if h % 2 == 1:
                o_ref[0, a * Q_BLOCK:(a + 1) * Q_BLOCK, (h // 2) * LANES:(h // 2 + 1) * LANES] = jnp.where(
                    low, outs[a, h - 1], outs[a, h]).astype(BF16)


def _gqa(sink, q, k, v, *, n_lat, n_ctx_blocks, ctx_out, c_len):
    b = q.shape[0]
    nq = n_lat + (n_ctx_blocks if ctx_out else 0)
    assert n_lat % GQA_QBLOCKS == 0 and n_ctx_blocks % GQA_QBLOCKS == 0
    ctx_idx = n_lat * Q_BLOCK // c_len
    n_keys = 3 * Q_BLOCK + c_len
    qt = GQA_QBLOCKS * Q_BLOCK

    def kv_spec(t):
        return pl.BlockSpec((1, Q_BLOCK, GQA_KV_WIDTH),
                            lambda bb, g: (bb, jnp.clip(g * GQA_QBLOCKS - 1 + t, 0, n_lat - 1), 0))

    kv = [kv_spec(t) for t in range(GQA_QBLOCKS + 2)]
    kv.append(pl.BlockSpec((1, c_len, GQA_KV_WIDTH), lambda bb, g: (bb, ctx_idx, 0)))
    return pl.pallas_call(
        functools.partial(_gqa_kernel, n_lat=n_lat),
        grid=(b, nq // GQA_QBLOCKS),
        in_specs=[pl.BlockSpec(memory_space=pltpu.SMEM),
                  pl.BlockSpec((1, qt, GQA_WIDTH), lambda bb, g: (bb, g, 0))] + kv + kv,
        out_specs=pl.BlockSpec((1, qt, GQA_WIDTH), lambda bb, g: (bb, g, 0)),
        out_shape=jax.ShapeDtypeStruct((b, nq * Q_BLOCK, GQA_WIDTH), BF16),
        scratch_shapes=[pltpu.VMEM((3 * Q_BLOCK, n_keys), F32), pltpu.VMEM((3 * Q_BLOCK, n_keys), BF16)],
        compiler_params=_params(2),
        name="window_gqa",
    )(sink, q, *([k] * (GQA_QBLOCKS + 3)), *([v] * (GQA_QBLOCKS + 3)))


def _na_band_start(r, rows):
    return jnp.clip(r - NA_ROWS // 2, 0, rows - NA_ROWS)


def _na_union_start(g, rows):
    return jnp.minimum(_na_band_start(g * NA_QROWS, rows), rows - NA_UNION)


def _na_kernel(q_ref, kb_ref, vb_ref, kx_ref, vx_ref, bias_ref, o_ref, s_ref, p_ref, *, rows):
    g = pl.program_id(1)
    n_loc = NA_ROWS * GRID_W
    low = lax.broadcasted_iota(jnp.int32, (GRID_W, LANES), 1) < HEAD_DIM
    n_pairs = NA_WIDTH // LANES
    u0 = _na_union_start(g, rows)
    offs, dr0s, pens = [], [], []
    for rho in range(NA_QROWS):
        r = g * NA_QROWS + rho
        is_ctx = r >= rows
        band = _na_band_start(r, rows)
        offs.append(pl.multiple_of(jnp.where(is_ctx, 0, band - u0) * GRID_W, GRID_W))
        dr0s.append(jnp.where(is_ctx, 0, (NA_ROWS - 1) - (r - band)))
        pens.append(jnp.where(is_ctx, NEG, 0.0).astype(F32))
    items = [(rho, j) for rho in range(NA_QROWS) for j in range(n_pairs)]
    rows2 = 2 * GRID_W
    slot = lambda n: slice((n % 3) * rows2, (n % 3 + 1) * rows2)
    invs = {}
    for n in range(len(items) + 2):
        if n < len(items):
            rho, j = items[n]
            sl = slice(j * LANES, (j + 1) * LANES)
            qp = q_ref[0, rho * GRID_W:(rho + 1) * GRID_W, sl]
            keys = jnp.concatenate([kb_ref[0, pl.ds(offs[rho], n_loc), sl], kx_ref[0, :, sl]], axis=0)
            s2 = _dot_t(jnp.concatenate([_head_half(qp, low, 0), _head_half(qp, low, 1)], axis=0), keys)
            bias = jnp.concatenate(
                [jnp.concatenate([bias_ref[2 * j + half, dr0s[rho] + i] for i in range(0, NA_ROWS, 2)], axis=1)
                 for half in range(2)], axis=0)
            s_ref[slot(n), 0:n_loc] = s2[:, 0:n_loc] + (bias + pens[rho])
            s_ref[slot(n), n_loc:] = s2[:, n_loc:]
        if 1 <= n <= len(items):
            invs[n - 1] = _softmax_rows(s_ref.at[slot(n - 1)], p_ref.at[slot(n - 1)])
        if n >= 2:
            rho, j = items[n - 2]
            sl = slice(j * LANES, (j + 1) * LANES)
            vals = jnp.concatenate([vb_ref[0, pl.ds(offs[rho], n_loc), sl], vx_ref[0, :, sl]], axis=0)
            o2 = _dot(p_ref[slot(n - 2), :], vals) * invs[n - 2]
            o_ref[0, rho * GRID_W:(rho + 1) * GRID_W, sl] = jnp.where(low, o2[:GRID_W], o2[GRID_W:]).astype(BF16)


def _na(q, k, v, bias, *, li, rows, n_ctx_rows, ctx_out, c_len):
    b = q.shape[0]
    assert rows % NA_QROWS == 0 and n_ctx_rows % NA_QROWS == 0 and rows >= NA_UNION
    ng = (rows + (n_ctx_rows if ctx_out else 0)) // NA_QROWS
    qt = NA_QROWS * GRID_W
    ctx_idx = rows * GRID_W // c_len
    n_keys = NA_ROWS * GRID_W + c_len
    union_spec = pl.BlockSpec((pl.Element(1), pl.Element(NA_UNION * GRID_W), pl.Element(NA_WIDTH)),
                              lambda bb, g: (bb, _na_union_start(g, rows) * GRID_W, 0))
    ctx_spec = pl.BlockSpec((1, c_len, NA_WIDTH), lambda bb, g: (bb, ctx_idx, 0))
    return pl.pallas_call(
        functools.partial(_na_kernel, rows=rows),
        grid=(b, ng),
        in_specs=[pl.BlockSpec((1, qt, NA_WIDTH), lambda bb, g: (bb, g, 0)),
                  union_spec, union_spec, ctx_spec, ctx_spec, _layer_resident(bias, li)],
        out_specs=pl.BlockSpec((1, qt, NA_WIDTH), lambda bb, g: (bb, g, 0)),
        out_shape=jax.ShapeDtypeStruct((b, ng * qt, NA_WIDTH), BF16),
        scratch_shapes=[pltpu.VMEM((3 * 2 * GRID_W, n_keys), F32), pltpu.VMEM((3 * 2 * GRID_W, n_keys), BF16)],
        compiler_params=_params(2),
        name="neighborhood_attn",
    )(q, k, v, k, v, bias)


def _na_bias_pairs(rpb):
    cols = np.arange(GRID_W)
    col_start = np.clip(cols - NA_COLS // 2, 0, GRID_W - NA_COLS)
    kc = np.arange(GRID_W)
    in_win = (kc[None, :] >= col_start[:, None]) & (kc[None, :] < col_start[:, None] + NA_COLS)
    col_idx = np.clip(kc[None, :] - cols[:, None] + (NA_COLS - 1), 0, 2 * NA_COLS - 2)
    tab = jnp.where(in_win, rpb.astype(F32)[..., col_idx], NEG)
    return jnp.concatenate([tab[:, :, :-1], tab[:, :, 1:]], axis=-1)


def _s5_prep_kernel(are_ref, aim_ref, ldt_ref, bre_ref, bim_ref, lbr_ref, lbi_ref, bbr_ref, bbi_ref):
    are, aim = are_ref[...], aim_ref[...]
    dt = jnp.exp(ldt_ref[...])
    mag = jnp.exp(are * dt)
    lbr, lbi = mag * jnp.cos(aim * dt), mag * jnp.sin(aim * dt)
    lbr_ref[...] = lbr
    lbi_ref[...] = lbi
    nr, ni = lbr - 1.0, lbi
    den = are * are + aim * aim
    cr, ci = (nr * are + ni * aim) / den, (ni * are - nr * aim) / den
    br, bi = bre_ref[...], bim_ref[...]
    bbr_ref[...] = cr * br - ci * bi
    bbi_ref[...] = cr * bi + ci * br


def _s5_prep(a_re, a_im, log_dt, b_re, b_im):
    nd, g, p = a_re.shape
    i = b_re.shape[-1]
    n = nd * g * p
    col = lambda z: z.reshape(n, 1)
    ldt = jnp.broadcast_to(log_dt[:, :, None], (nd, g, p))
    whole = lambda w: pl.BlockSpec((n, w), lambda: (0, 0))
    lbr, lbi, bbr, bbi = pl.pallas_call(
        _s5_prep_kernel,
        in_specs=[whole(1)] * 3 + [whole(i)] * 2,
        out_specs=[whole(1)] * 2 + [whole(i)] * 2,
        out_shape=[jax.ShapeDtypeStruct((n, 1), F32)] * 2 + [jax.ShapeDtypeStruct((n, i), F32)] * 2,
        name="s5_prep",
    )(col(a_re), col(a_im), col(ldt), b_re.reshape(n, i), b_im.reshape(n, i))
    return lbr.reshape(nd, g, p), lbi.reshape(nd, g, p), bbr.reshape(nd, g, p, i), bbi.reshape(nd, g, p, i)


def _s5_block_operands(lbr, lbi, bbr, bbi, c_re, c_im):
    nd, g, p = lbr.shape
    i = bbr.shape[-1]
    nb, gb = SSM_BLOCKS, SSM_BLOCK_GROUPS
    eye = jnp.eye(gb, dtype=F32)

    def in_map(z):
        z = z.reshape(nd, nb, gb, p, i).transpose(0, 1, 2, 4, 3)
        return (z[:, :, :, :, None, :] * eye[None, None, :, None, :, None]).reshape(nd, nb, gb * i, gb * p)

    def out_map(z):
        z = z.reshape(nd, nb, gb, i, p).transpose(0, 1, 2, 4, 3)
        return (z[:, :, :, :, None, :] * eye[None, None, :, None, :, None]).reshape(nd, nb, gb * p, gb * i)

    lam = jnp.stack([lbr.reshape(nd, nb, gb * p), lbi.reshape(nd, nb, gb * p)], axis=2)
    w_in = jnp.concatenate([in_map(bbr), in_map(bbi)], axis=-1).astype(BF16)
    w_out = jnp.concatenate([out_map(c_re.astype(F32)), -out_map(c_im.astype(F32))], axis=2).astype(BF16)
    return lam, w_in, w_out


def _s5_kernel(u_ref, lam_ref, win_ref, wout_ref, y_ref, st_ref, carry_ref, *, nb, tt, reverse):
    ns = SSM_BLOCK_STATE
    rows = tt * nb
    n_pipe = SSM_PIPE_TILES

    @pl.when(pl.program_id(0) == 0)
    def _():
        carry_ref[...] = jnp.zeros_like(carry_ref)

    lr = [jnp.broadcast_to(lam_ref[j, 0:1, :], (nb, ns)) for j in range(SSM_BLOCKS)]
    li = [jnp.broadcast_to(lam_ref[j, 1:2, :], (nb, ns)) for j in range(SSM_BLOCKS)]
    order = list(range(n_pipe))[::-1] if reverse else list(range(n_pipe))
    steps = list(range(tt))[::-1] if reverse else list(range(tt))
    carry = [carry_ref[k] for k in range(2 * SSM_BLOCKS)]
    for i in range(n_pipe + 2):
        if i < n_pipe:
            trows = slice(order[i] * rows, (order[i] + 1) * rows)
            for j in range(SSM_BLOCKS):
                st_ref[i % 3, j] = _dot(u_ref[j, trows, :].astype(BF16), win_ref[j])
        if 1 <= i <= n_pipe:
            buf = (i - 1) % 3
            for t in steps:
                r = slice(t * nb, (t + 1) * nb)
                for j in range(SSM_BLOCKS):
                    sr, si = carry[2 * j], carry[2 * j + 1]
                    nr = lr[j] * sr - li[j] * si + st_ref[buf, j, r, 0:ns]
                    ni = lr[j] * si + li[j] * sr + st_ref[buf, j, r, ns:2 * ns]
                    st_ref[buf, j, r, 0:ns] = nr
                    st_ref[buf, j, r, ns:2 * ns] = ni
                    carry[2 * j], carry[2 * j + 1] = nr, ni
        if i >= 2:
            trows = slice(order[i - 2] * rows, (order[i - 2] + 1) * rows)
            for j in range(SSM_BLOCKS):
                y_ref[j, trows, :] = _dot(st_ref[(i - 2) % 3, j].astype(BF16), wout_ref[j])
    for k in range(2 * SSM_BLOCKS):
        carry_ref[k] = carry[k]


def _s5_scan(u2, lam, w_in, w_out, *, li, direction, nb, n_lat, n_ctx):
    _, rows_total, _ = u2.shape
    tt = SSM_TILE
    span = SSM_PIPE_TILES * tt
    assert n_lat % span == 0 and n_ctx % span == 0
    n_lat_g, n_ctx_g = n_lat // span, n_ctx // span
    n_g = n_lat_g + n_ctx_g
    ns2 = 2 * SSM_BLOCK_STATE
    reverse = direction == 1

    def group(k):
        if reverse:
            return jnp.where(k < n_ctx_g, n_g - 1 - k, n_lat_g - 1 - (k - n_ctx_g))
        return jnp.where(k < n_ctx_g, n_lat_g + k, k - n_ctx_g)

    def par(*tail):
        return pl.BlockSpec((None,) + tail, lambda k: (2 * li + direction,) + (0,) * len(tail))

    stream = pl.BlockSpec((SSM_BLOCKS, span * nb, LANES), lambda k: (0, group(k), 0))
    return pl.pallas_call(
        functools.partial(_s5_kernel, nb=nb, tt=tt, reverse=reverse),
        grid=(n_g,),
        in_specs=[stream, par(SSM_BLOCKS, 2, SSM_BLOCK_STATE), par(SSM_BLOCKS, LANES, ns2),
                  par(SSM_BLOCKS, ns2, LANES)],
        out_specs=stream,
        out_shape=jax.ShapeDtypeStruct((SSM_BLOCKS, rows_total, LANES), F32),
        scratch_shapes=[pltpu.VMEM((3, SSM_BLOCKS, tt * nb, ns2), F32),
                        pltpu.VMEM((2 * SSM_BLOCKS, nb, SSM_BLOCK_STATE), F32)],
        compiler_params=_params(1),
        name="s5_scan_bwd" if reverse else "s5_scan_fwd",
    )(u2, lam, w_in, w_out)


def _merge_kernel(x_ref, mod_ref, g_ref, ysf_ref, ysb_ref, u_ref, ya_ref, yn_ref, d_ref,
                  wgt_ref, wglu_ref, wps_ref, wpa_ref, wpn_ref, wo_ref, o_ref):
    d = x_ref.shape[2]
    tm = x_ref.shape[1]
    nb = u_ref.shape[1] // tm
    xs = [x_ref[sub] for sub in range(TOKEN_BATCH)]
    h = _stacked_mod_norm(xs, g_ref[1:2, :], mod_ref, 3)
    gate_logits = [_dot(h, wgt_ref[:, n * d:(n + 1) * d]) for n in range(N_BRANCH)]
    gate = lambda n: jax.nn.sigmoid(gate_logits[n])

    def from_slabs(load):
        return jnp.concatenate([jnp.concatenate([load(blk, _slab_rows(sub, tm, nb)) for blk in range(SSM_BLOCKS)],
                                                axis=1) for sub in range(TOKEN_BATCH)], axis=0)

    y = from_slabs(lambda blk, rows: ysf_ref[blk, rows, :] + ysb_ref[blk, rows, :])
    y = y + d_ref[...] * from_slabs(lambda blk, rows: u_ref[blk, rows, :])
    t = 0.5 * y * (1.0 + jnp.tanh(math.sqrt(2.0 / math.pi) * (y + 0.044715 * (y * y * y))))
    glu = (t * jax.nn.sigmoid(_dot(t.astype(BF16), wglu_ref[...]))).astype(BF16)
    stacked = lambda ref: jnp.concatenate([ref[sub] for sub in range(TOKEN_BATCH)], axis=0)
    m = (gate(0) * _dot(glu, wps_ref[...]) + gate(1) * _dot(stacked(ya_ref), wpa_ref[...])
         + gate(2) * _dot(stacked(yn_ref), wpn_ref[...]))
    out = _dot(m.astype(BF16), wo_ref[...])
    for sub, x in enumerate(xs):
        o_ref[sub] = x + mod_ref[sub, 5:6, :] * _rows(out, sub, tm)


def _merge(xs, mod, g, ysf, ysb, u2, ya, yn, ssm_d, wgt, wglu, wps, wpa, wpn, wo, *, li, n_tok, n_lat_tiles,
           ctx_group):
    b, _, d = xs.shape
    tm = TOKEN_TILE
    return pl.pallas_call(
        _merge_kernel,
        grid=(n_tok // tm, b // TOKEN_BATCH),
        in_specs=[_tok_spec(d, tm), _mod_spec(d, n_lat_tiles, ctx_group), _resident(g.shape),
                  _slab_spec(tm, b), _slab_spec(tm, b), _slab_spec(tm, b),
                  _tok_spec(GQA_WIDTH, tm), _tok_spec(NA_WIDTH, tm), _resident((1, SSM_WIDTH)),
                  _layer_resident(wgt, li), _layer_resident(wglu, li), _layer_resident(wps, li),
                  _layer_resident(wpa, li), _layer_resident(wpn, li), _layer_resident(wo, li)],
        out_specs=_tok_spec(d, tm),
        out_shape=jax.ShapeDtypeStruct((b, n_tok, d), F32),
        compiler_params=_params(2),
        name="mixer_merge",
    )(xs, mod, g, ysf, ysb, u2, ya, yn, ssm_d.reshape(1, SSM_WIDTH), wgt, wglu, wps, wpa, wpn, wo)


def _rope_tables(l, c_len):
    t = jnp.arange(l)
    pos = jnp.stack([t // GRID_W, t % GRID_W], axis=-1).astype(F32)
    half = HEAD_DIM // 2
    inv = ROPE_THETA ** (-jnp.arange(0, half, 2, dtype=F32) / half)
    ang = pos[:, :, None] * inv
    cos, sin = jnp.cos(ang), jnp.sin(ang)
    zero = jnp.zeros_like(sin[:, 0])
    cos_h = jnp.concatenate([cos[:, 0], cos[:, 0], cos[:, 1], cos[:, 1]], axis=-1)
    sa_h = jnp.concatenate([zero, sin[:, 0], zero, sin[:, 1]], axis=-1)
    sb_h = jnp.concatenate([-sin[:, 0], zero, -sin[:, 1], zero], axis=-1)

    def full(tab, ctx_fill):
        tab = jnp.concatenate([tab, tab], axis=-1)
        return jnp.concatenate([tab, jnp.full((c_len, LANES), ctx_fill, F32)], axis=0)

    return full(cos_h, 1.0), full(sa_h, 0.0), full(sb_h, 0.0)


def kernel(x, c, ctx, c_ctx, w_ada, b_ada, norm_g, ffn1_wg, ffn1_wu, ffn1_wd, ffn2_wg, ffn2_wu, ffn2_wd, w_in,
           ssm_a_re, ssm_a_im, ssm_log_dt, ssm_b_re, ssm_b_im, ssm_c_re, ssm_c_im, ssm_d, ssm_w_glu, gqa_sink,
           na_rpb, w_p_ssm, w_p_gqa, w_p_na, w_out, final_g):
    b, l, d = x.shape
    c_len = ctx.shape[1]
    s = l + c_len
    depth = w_ada.shape[0]
    tm = TOKEN_TILE
    assert b + TOKEN_BATCH <= MOD_ROWS and b % TOKEN_BATCH == 0 and l % tm == 0 and c_len % tm == 0
    assert l % (GRID_W * NA_ROWS) == 0 and c_len % Q_BLOCK == 0 and l % c_len == 0 and b % SUBLANES == 0
    assert l % SSM_TILE == 0 and c_len % SSM_TILE == 0
    n_lat_tiles = l // tm

    cin = jnp.zeros((MOD_ROWS, d), F32).at[:b].set(c).at[b:b + TOKEN_BATCH].set(c_ctx)
    mods = _ada(cin, w_ada, b_ada).reshape(depth, MOD_ROWS, N_MOD, d)
    cos_t, sa_t, sb_t = _rope_tables(l, c_len)

    perm = np.asarray(GQA_HEAD_PERM)
    q0 = SSM_WIDTH
    w_q = w_in[:, :, q0:q0 + GQA_WIDTH].reshape(depth, d, GQA_HEADS, HEAD_DIM)[:, :, perm]
    w_in_p = jnp.concatenate([w_in[:, :, :q0], w_q.reshape(depth, d, GQA_WIDTH),
                              w_in[:, :, q0 + GQA_WIDTH:MIXER_COLS]], axis=-1).astype(BF16)
    w_gt = w_in[:, :, MIXER_COLS:].astype(BF16)
    w_pa = w_p_gqa.reshape(depth, GQA_HEADS, HEAD_DIM, d)[:, perm].reshape(depth, GQA_WIDTH, d).astype(BF16)
    bf = lambda z: z.astype(BF16)
    f1 = (bf(ffn1_wg), bf(ffn1_wu), bf(ffn1_wd))
    f2 = (bf(ffn2_wg), bf(ffn2_wu), bf(ffn2_wd))
    w_glu, w_ps, w_pn, w_o = bf(ssm_w_glu), bf(w_p_ssm), bf(w_p_na), bf(w_out)

    merged = lambda z: z.reshape((depth * 2,) + z.shape[2:])
    lam, s_in, s_out = _s5_block_operands(
        *_s5_prep(merged(ssm_a_re), merged(ssm_a_im), merged(ssm_log_dt), merged(ssm_b_re), merged(ssm_b_im)),
        merged(ssm_c_re), merged(ssm_c_im))
    na_bias = _na_bias_pairs(na_rpb)

    tile_kw = dict(n_lat_tiles=n_lat_tiles, ctx_group=b // TOKEN_BATCH)
    xs = x
    for li in range(depth):
        ctx_out = li < depth - 1
        n_tok = s if ctx_out else l
        mod, g = mods[li], norm_g[li]
        xs = _ffn(xs, mod, g, *f1, li=li, k0=0, gi=0, n_tok=s, ctx=ctx if li == 0 else None, **tile_kw)
        u2, q, k, v, nq, nk, nv = _inproj(xs, mod, g, w_in_p, cos_t, sa_t, sb_t, li=li, **tile_kw)
        ysf, ysb = (_s5_scan(u2, lam, s_in, s_out, li=li, direction=dd, nb=b, n_lat=l, n_ctx=c_len)
                    for dd in range(2))
        ya = _gqa(gqa_sink[li], q, k, v, n_lat=l // Q_BLOCK, n_ctx_blocks=c_len // Q_BLOCK, ctx_out=ctx_out,
                  c_len=c_len)
        yn = _na(nq, nk, nv, na_bias, li=li, rows=l // GRID_W, n_ctx_rows=c_len // GRID_W, ctx_out=ctx_out,
                 c_len=c_len)
        xs = _merge(xs, mod, g, ysf, ysb, u2, ya, yn, ssm_d[li], w_gt, w_glu, w_ps, w_pa, w_pn, w_o,
                    li=li, n_tok=n_tok, **tile_kw)
        xs = _ffn(xs, mod, g, *f2, li=li, k0=6, gi=2, n_tok=n_tok, final_g=None if ctx_out else final_g, **tile_kw)
    return xs
```

```python
import functools
import math

import jax
import jax.numpy as jnp
import numpy as np
from jax import lax
from jax.experimental import pallas as pl
from jax.experimental.pallas import tpu as pltpu

F32 = jnp.float32
BF16 = jnp.bfloat16

EPS = 1e-6
NEG = -1e30
N_MOD = 9
GRID_W = 64
HEAD_DIM = 64
GQA_HEADS = 8
GQA_KV_HEADS = 2
NA_HEADS = 8
NA_ROWS = 8
NA_COLS = 16
Q_BLOCK = 128
ROPE_THETA = 10000.0
SSM_GROUP = 16
SSM_GROUPS = 24
SSM_STATE = 64
SSM_WIDTH = SSM_GROUP * SSM_GROUPS
GQA_WIDTH = GQA_HEADS * HEAD_DIM
GQA_KV_WIDTH = GQA_KV_HEADS * HEAD_DIM
NA_WIDTH = NA_HEADS * HEAD_DIM
N_BRANCH = 3
MIXER_COLS = SSM_WIDTH + GQA_WIDTH + 2 * GQA_KV_WIDTH + 3 * NA_WIDTH

LANES = 128
SUBLANES = 8
MOD_ROWS = 16
TOKEN_TILE = 256
TOKEN_BATCH = 2
FFN_CHUNK = 256
GQA_QBLOCKS = 2
NA_QROWS = 4
NA_UNION = NA_QROWS + NA_ROWS - 1
SSM_TILE = 64
SSM_PIPE_TILES = 4
SSM_BLOCK_GROUPS = LANES // SSM_GROUP
SSM_BLOCKS = SSM_GROUPS // SSM_BLOCK_GROUPS
SSM_BLOCK_STATE = SSM_BLOCK_GROUPS * SSM_STATE
VMEM_LIMIT = 56 * 1024 * 1024
GQA_HEAD_PERM = (0, 4, 1, 5, 2, 6, 3, 7)


def _params(n_axes):
    return pltpu.CompilerParams(dimension_semantics=("arbitrary",) * n_axes, vmem_limit_bytes=VMEM_LIMIT)


def _resident(shape):
    return pl.BlockSpec(shape, lambda *_: (0,) * len(shape), pipeline_mode=pl.Buffered(1))


def _layer_resident(stack, li):
    nd = stack.ndim
    return pl.BlockSpec((None,) + stack.shape[1:], lambda *_: (li,) + (0,) * (nd - 1),
                        pipeline_mode=pl.Buffered(1))


def _dot(a, b):
    return jnp.dot(a, b, preferred_element_type=F32)


def _dot_t(a, b):
    return lax.dot_general(a, b, (((1,), (1,)), ((), ())), preferred_element_type=F32)


def _rmsnorm(x, g):
    return x * lax.rsqrt(jnp.mean(x * x, axis=-1, keepdims=True) + EPS) * g


def _mod_norm(x, g, shift, scale):
    return _rmsnorm(x, g) * (1.0 + scale) + shift


def _ada_kernel(c_ref, w_ref, b_ref, o_ref):
    s = c_ref[...]
    s = s * jax.nn.sigmoid(s)
    o_ref[0] = _dot(s.astype(BF16), w_ref[0].astype(BF16)) + b_ref[0]


def _ada(cin, w_ada, b_ada):
    depth, d, n = w_ada.shape
    tn = n // 4
    return pl.pallas_call(
        _ada_kernel,
        grid=(depth, n // tn),
        in_specs=[
            pl.BlockSpec((MOD_ROWS, d), lambda l, j: (0, 0)),
            pl.BlockSpec((1, d, tn), lambda l, j: (l, 0, j)),
            pl.BlockSpec((1, 1, tn), lambda l, j: (l, 0, j)),
        ],
        out_specs=pl.BlockSpec((1, MOD_ROWS, tn), lambda l, j: (l, 0, j)),
        out_shape=jax.ShapeDtypeStruct((depth, MOD_ROWS, n), F32),
        compiler_params=_params(2),
        name="ada_mod",
    )(cin, w_ada, b_ada.reshape(depth, 1, n))


def _tok_spec(width, tm):
    return pl.BlockSpec((TOKEN_BATCH, tm, width), lambda j, b: (b, j, 0))


def _rows(val, sub, tm):
    return val[sub * tm:(sub + 1) * tm]


def _slab_rows(sub, tm, nb):
    return pl.ds(pl.program_id(1) * TOKEN_BATCH + sub, tm, stride=nb)


def _slab_spec(tm, nb):
    return pl.BlockSpec((SSM_BLOCKS, tm * nb, LANES), lambda j, b: (0, j, 0))


def _mod_spec(d, n_lat_tiles, ctx_group):
    return pl.BlockSpec((TOKEN_BATCH, N_MOD, d), lambda j, b: (jnp.where(j >= n_lat_tiles, ctx_group, b), 0, 0))


def _stacked_mod_norm(xs, g, mod_ref, shift_row):
    return jnp.concatenate(
        [_mod_norm(x, g, mod_ref[sub, shift_row:shift_row + 1, :], mod_ref[sub, shift_row + 1:shift_row + 2, :])
         for sub, x in enumerate(xs)], axis=0).astype(BF16)


def _ffn_kernel(*refs, k0, gi, final, n_lat_tiles, split_in, mixer):
    refs = list(refs)
    x_ref = refs.pop(0)
    ctx_ref = refs.pop(0) if split_in else None
    mod_ref, g_ref, wg_ref, wu_ref, wd_ref = refs[:5]
    n_in = 5
    fg_ref = refs[n_in] if final else None
    n_in += int(final)
    mixer_in = refs[n_in:n_in + 4] if mixer else ()
    n_in += len(mixer_in)
    o_ref, mixer_out, t_ref = refs[n_in], refs[n_in + 1:-1], refs[-1]
    tm = x_ref.shape[1]
    xs = [x_ref[sub] for sub in range(TOKEN_BATCH)]
    if split_in:
        is_ctx = pl.program_id(0) >= n_lat_tiles
        xs = [jnp.where(is_ctx, ctx_ref[sub], x) for sub, x in enumerate(xs)]
    h = _stacked_mod_norm(xs, g_ref[gi:gi + 1, :], mod_ref, k0)
    f = wg_ref.shape[1]
    for c0 in range(0, f, FFN_CHUNK):
        a = _dot(h, wg_ref[:, c0:c0 + FFN_CHUNK])
        u = _dot(h, wu_ref[:, c0:c0 + FFN_CHUNK])
        t_ref[:, c0:c0 + FFN_CHUNK] = (a * jax.nn.sigmoid(a) * u).astype(BF16)
    y = _dot(t_ref[...], wd_ref[...])
    outs = []
    for sub, x in enumerate(xs):
        out = x + (0.5 * mod_ref[sub, k0 + 2:k0 + 3, :]) * _rows(y, sub, tm)
        if final:
            out = _rmsnorm(out, fg_ref[...])
        o_ref[sub] = out
        outs.append(out)
    if mixer:
        _inproj_body(outs, mod_ref, g_ref, *mixer_in, *mixer_out)


def _ffn(xs, mod, g, wg, wu, wd, *, li, k0, gi, n_tok, n_lat_tiles, ctx_group, final_g=None, ctx=None, mixer=None):
    b, _, d = xs.shape
    f = wg.shape[2]
    tm = TOKEN_TILE
    final = final_g is not None
    split_in = ctx is not None
    if split_in:
        in_specs = [pl.BlockSpec((TOKEN_BATCH, tm, d), lambda j, bb: (bb, jnp.minimum(j, n_lat_tiles - 1), 0)),
                    pl.BlockSpec((TOKEN_BATCH, tm, d), lambda j, bb: (bb, jnp.maximum(j - n_lat_tiles, 0), 0))]
        args = [xs, ctx]
    else:
        in_specs = [_tok_spec(d, tm)]
        args = [xs]
    in_specs += [_mod_spec(d, n_lat_tiles, ctx_group), _resident(g.shape),
                 _layer_resident(wg, li), _layer_resident(wu, li), _layer_resident(wd, li)]
    args += [mod, g, wg, wu, wd]
    if final:
        in_specs.append(_resident((1, d)))
        args.append(final_g.reshape(1, d))
    out_specs = [_tok_spec(d, tm)]
    out_shape = [jax.ShapeDtypeStruct((b, n_tok, d), F32)]
    if mixer:
        w_in, cos_t, sa_t, sb_t = mixer
        assert w_in.shape[2] == MIXER_COLS
        rope_spec = pl.BlockSpec((tm, LANES), lambda j, bb: (j, 0))
        in_specs += [_layer_resident(w_in, li), rope_spec, rope_spec, rope_spec]
        args += [w_in, cos_t, sa_t, sb_t]
        widths = (GQA_WIDTH, GQA_KV_WIDTH, GQA_KV_WIDTH, NA_WIDTH, NA_WIDTH, NA_WIDTH)
        out_specs += [_slab_spec(tm, b)] + [_tok_spec(w, tm) for w in widths]
        out_shape += ([jax.ShapeDtypeStruct((SSM_BLOCKS, n_tok * b, LANES), F32)]
                      + [jax.ShapeDtypeStruct((b, n_tok, w), BF16) for w in widths])
    res = pl.pallas_call(
        functools.partial(_ffn_kernel, k0=k0, gi=gi, final=final, n_lat_tiles=n_lat_tiles, split_in=split_in,
                          mixer=bool(mixer)),
        grid=(n_tok // tm, b // TOKEN_BATCH),
        in_specs=in_specs,
        out_specs=out_specs,
        out_shape=out_shape,
        scratch_shapes=[pltpu.VMEM((TOKEN_BATCH * tm, f), BF16)],
        compiler_params=_params(2),
        name="ffn_half_final" if final else ("ffn_in_proj" if mixer else "ffn_half"),
    )(*args)
    return res if mixer else res[0]


def _inproj_body(xs, mod_ref, g_ref, w_ref, cos_ref, sa_ref, sb_ref,
                 u_ref, q_ref, k_ref, v_ref, nq_ref, nk_ref, nv_ref):
    tm = xs[0].shape[0]
    nb = u_ref.shape[1] // tm
    h = _stacked_mod_norm(xs, g_ref[1:2, :], mod_ref, 3)
    cos, sa, sb = (jnp.concatenate([t[...]] * TOKEN_BATCH, axis=0) for t in (cos_ref, sa_ref, sb_ref))
    scale = HEAD_DIM ** -0.5

    def mm(c0, n):
        return _dot(h, w_ref[:, c0:c0 + n])

    def rope(z):
        return z * cos + pltpu.roll(z, 16, 1) * sa + pltpu.roll(z, LANES - 16, 1) * sb

    def put(ref, val, lanes=slice(None)):
        for sub in range(TOKEN_BATCH):
            ref[sub, :, lanes] = _rows(val, sub, tm)

    c0 = SSM_WIDTH + GQA_WIDTH + 2 * GQA_KV_WIDTH
    z = mm(0, c0)
    col = lambda n: z[:, n * LANES:(n + 1) * LANES]
    n = 0
    for blk in range(SSM_BLOCKS):
        for sub in range(TOKEN_BATCH):
            u_ref[blk, _slab_rows(sub, tm, nb), :] = _rows(col(n), sub, tm)
        n += 1
    for j in range(GQA_WIDTH // LANES):
        put(q_ref, (rope(col(n)) * scale).astype(BF16), slice(j * LANES, (j + 1) * LANES))
        n += 1
    put(k_ref, rope(col(n)).astype(BF16))
    put(v_ref, col(n + 1).astype(BF16))
    put(nq_ref, (mm(c0, NA_WIDTH) * scale).astype(BF16))
    c0 += NA_WIDTH
    put(nk_ref, mm(c0, NA_WIDTH).astype(BF16))
    c0 += NA_WIDTH
    put(nv_ref, mm(c0, NA_WIDTH).astype(BF16))


def _softmax_rows(s_ref, p_ref, extra=None):
    s = s_ref[...]
    m = jnp.max(s, axis=1, keepdims=True)
    if extra is not None:
        m = jnp.maximum(m, extra)
    e = jnp.exp(s - m)
    den = jnp.sum(e, axis=1, keepdims=True)
    if extra is not None:
        den = den + jnp.exp(extra - m)
    p_ref[...] = e.astype(BF16)
    return 1.0 / den


def _head_half(qp, low, half):
    return jnp.where(low if half == 0 else jnp.logical_not(low), qp, jnp.zeros_like(qp))


def _gqa_kernel(sink_ref, q_ref, *refs, n_lat):
    nkb = GQA_QBLOCKS + 2
    k_refs, kx_ref = refs[:nkb], refs[nkb]
    v_refs, vx_ref = refs[nkb + 1:2 * nkb + 1], refs[2 * nkb + 1]
    o_ref, s_ref, p_ref = refs[2 * nkb + 2:]
    n_keys = s_ref.shape[1]
    far = 8 * n_keys
    row = lax.broadcasted_iota(jnp.int32, (Q_BLOCK, n_keys), 0)
    col = lax.broadcasted_iota(jnp.int32, (Q_BLOCK, n_keys), 1)
    low = lax.broadcasted_iota(jnp.int32, (Q_BLOCK, LANES), 1) < HEAD_DIM
    oks, keys, vals = [], [], []
    for a in range(GQA_QBLOCKS):
        i = pl.program_id(1) * GQA_QBLOCKS + a
        is_lat = i < n_lat
        off_p = jnp.where(jnp.logical_and(is_lat, i > 0), 0, far)
        off_c = jnp.where(is_lat, 0, far)
        off_n = jnp.where(jnp.logical_and(is_lat, i < n_lat - 1), 0, far)
        oks.append((col >= row + off_p) & (col < Q_BLOCK)
                   | (col >= Q_BLOCK + off_c) & (col < 2 * Q_BLOCK)
                   | (col >= 2 * Q_BLOCK) & (col <= 2 * Q_BLOCK + row - off_n)
                   | (col >= 3 * Q_BLOCK))
        keys.append(jnp.concatenate([k_refs[a][0], k_refs[a + 1][0], k_refs[a + 2][0], kx_ref[0]], axis=0))
        vals.append(jnp.concatenate([v_refs[a][0], v_refs[a + 1][0], v_refs[a + 2][0], vx_ref[0]], axis=0))
    items = [(a, h) for a in range(GQA_QBLOCKS) for h in range(GQA_HEADS)]
    invs, outs = {}, {}
    slot = lambda n: slice((n % 3) * Q_BLOCK, (n % 3 + 1) * Q_BLOCK)
    for n in range(len(items) + 2):
        if n < len(items):
            a, h = items[n]
            qp = q_ref[0, a * Q_BLOCK:(a + 1) * Q_BLOCK, (h // 2) * LANES:(h // 2 + 1) * LANES]
            s_ref[slot(n), :] = jnp.where(oks[a], _dot_t(_head_half(qp, low, h % 2), keys[a]), NEG)
        if 1 <= n <= len(items):
            a, h = items[n - 1]
            sink = jnp.full((Q_BLOCK, 1), sink_ref[GQA_HEAD_PERM[h]], F32)
            invs[a, h] = _softmax_rows(s_ref.at[slot(n - 1)], p_ref.at[slot(n - 1)], sink)
        if n >= 2:
            a, h = items[n - 2]
            outs[a, h] = _dot(p_ref[slot(n - 2), :], vals[a]) * invs[a, h]
            if h % 2 == 1:
                o_ref[0, a * Q_BLOCK:(a + 1) * Q_BLOCK, (h // 2) * LANES:(h // 2 + 1) * LANES] = jnp.where(
                    low, outs[a, h - 1], outs[a, h]).astype(BF16)


def _gqa(sink, q, k, v, *, n_lat, n_ctx_blocks, ctx_out, c_len):
    b = q.shape[0]
    nq = n_lat + (n_ctx_blocks if ctx_out else 0)
    assert n_lat % GQA_QBLOCKS == 0 and n_ctx_blocks % GQA_QBLOCKS == 0
    ctx_idx = n_lat * Q_BLOCK // c_len
    n_keys = 3 * Q_BLOCK + c_len
    qt = GQA_QBLOCKS * Q_BLOCK

    def kv_spec(t):
        return pl.BlockSpec((1, Q_BLOCK, GQA_KV_WIDTH),
                            lambda bb, g: (bb, jnp.clip(g * GQA_QBLOCKS - 1 + t, 0, n_lat - 1), 0))

    kv = [kv_spec(t) for t in range(GQA_QBLOCKS + 2)]
    kv.append(pl.BlockSpec((1, c_len, GQA_KV_WIDTH), lambda bb, g: (bb, ctx_idx, 0)))
    return pl.pallas_call(
        functools.partial(_gqa_kernel, n_lat=n_lat),
        grid=(b, nq // GQA_QBLOCKS),
        in_specs=[pl.BlockSpec(memory_space=pltpu.SMEM),
                  pl.BlockSpec((1, qt, GQA_WIDTH), lambda bb, g: (bb, g, 0))] + kv + kv,
        out_specs=pl.BlockSpec((1, qt, GQA_WIDTH), lambda bb, g: (bb, g, 0)),
        out_shape=jax.ShapeDtypeStruct((b, nq * Q_BLOCK, GQA_WIDTH), BF16),
        scratch_shapes=[pltpu.VMEM((3 * Q_BLOCK, n_keys), F32), pltpu.VMEM((3 * Q_BLOCK, n_keys), BF16)],
        compiler_params=_params(2),
        name="window_gqa",
    )(sink, q, *([k] * (GQA_QBLOCKS + 3)), *([v] * (GQA_QBLOCKS + 3)))


def _na_band_start(r, rows):
    return jnp.clip(r - NA_ROWS // 2, 0, rows - NA_ROWS)


def _na_union_start(g, rows):
    return jnp.minimum(_na_band_start(g * NA_QROWS, rows), rows - NA_UNION)


def _na_kernel(q_ref, kb_ref, vb_ref, kx_ref, vx_ref, bias_ref, o_ref, s_ref, p_ref, *, rows):
    g = pl.program_id(1)
    n_loc = NA_ROWS * GRID_W
    low = lax.broadcasted_iota(jnp.int32, (GRID_W, LANES), 1) < HEAD_DIM
    n_pairs = NA_WIDTH // LANES
    u0 = _na_union_start(g, rows)
    offs, dr0s, pens = [], [], []
    for rho in range(NA_QROWS):
        r = g * NA_QROWS + rho
        is_ctx = r >= rows
        band = _na_band_start(r, rows)
        offs.append(pl.multiple_of(jnp.where(is_ctx, 0, band - u0) * GRID_W, GRID_W))
        dr0s.append(jnp.where(is_ctx, 0, (NA_ROWS - 1) - (r - band)))
        pens.append(jnp.where(is_ctx, NEG, 0.0).astype(F32))
    items = [(rho, j) for rho in range(NA_QROWS) for j in range(n_pairs)]
    rows2 = 2 * GRID_W
    slot = lambda n: slice((n % 3) * rows2, (n % 3 + 1) * rows2)
    invs = {}
    for n in range(len(items) + 2):
        if n < len(items):
            rho, j = items[n]
            sl = slice(j * LANES, (j + 1) * LANES)
            qp = q_ref[0, rho * GRID_W:(rho + 1) * GRID_W, sl]
            keys = jnp.concatenate([kb_ref[0, pl.ds(offs[rho], n_loc), sl], kx_ref[0, :, sl]], axis=0)
            s2 = _dot_t(jnp.concatenate([_head_half(qp, low, 0), _head_half(qp, low, 1)], axis=0), keys)
            bias = jnp.concatenate(
                [jnp.concatenate([bias_ref[2 * j + half, dr0s[rho] + i] for i in range(0, NA_ROWS, 2)], axis=1)
                 for half in range(2)], axis=0)
            s_ref[slot(n), 0:n_loc] = s2[:, 0:n_loc] + (bias + pens[rho])
            s_ref[slot(n), n_loc:] = s2[:, n_loc:]
        if 1 <= n <= len(items):
            invs[n - 1] = _softmax_rows(s_ref.at[slot(n - 1)], p_ref.at[slot(n - 1)])
        if n >= 2:
            rho, j = items[n - 2]
            sl = slice(j * LANES, (j + 1) * LANES)
            vals = jnp.concatenate([vb_ref[0, pl.ds(offs[rho], n_loc), sl], vx_ref[0, :, sl]], axis=0)
            o2 = _dot(p_ref[slot(n - 2), :], vals) * invs[n - 2]
            o_ref[0, rho * GRID_W:(rho + 1) * GRID_W, sl] = jnp.where(low, o2[:GRID_W], o2[GRID_W:]).astype(BF16)


def _na(q, k, v, bias, *, li, rows, n_ctx_rows, ctx_out, c_len):
    b = q.shape[0]
    assert rows % NA_QROWS == 0 and n_ctx_rows % NA_QROWS == 0 and rows >= NA_UNION
    ng = (rows + (n_ctx_rows if ctx_out else 0)) // NA_QROWS
    qt = NA_QROWS * GRID_W
    ctx_idx = rows * GRID_W // c_len
    n_keys = NA_ROWS * GRID_W + c_len
    union_spec = pl.BlockSpec((pl.Element(1), pl.Element(NA_UNION * GRID_W), pl.Element(NA_WIDTH)),
                              lambda bb, g: (bb, _na_union_start(g, rows) * GRID_W, 0))
    ctx_spec = pl.BlockSpec((1, c_len, NA_WIDTH), lambda bb, g: (bb, ctx_idx, 0))
    return pl.pallas_call(
        functools.partial(_na_kernel, rows=rows),
        grid=(b, ng),
        in_specs=[pl.BlockSpec((1, qt, NA_WIDTH), lambda bb, g: (bb, g, 0)),
                  union_spec, union_spec, ctx_spec, ctx_spec, _layer_resident(bias, li)],
        out_specs=pl.BlockSpec((1, qt, NA_WIDTH), lambda bb, g: (bb, g, 0)),
        out_shape=jax.ShapeDtypeStruct((b, ng * qt, NA_WIDTH), BF16),
        scratch_shapes=[pltpu.VMEM((3 * 2 * GRID_W, n_keys), F32), pltpu.VMEM((3 * 2 * GRID_W, n_keys), BF16)],
        compiler_params=_params(2),
        name="neighborhood_attn",
    )(q, k, v, k, v, bias)


def _na_bias_pairs(rpb):
    cols = np.arange(GRID_W)
    col_start = np.clip(cols - NA_COLS // 2, 0, GRID_W - NA_COLS)
    kc = np.arange(GRID_W)
    in_win = (kc[None, :] >= col_start[:, None]) & (kc[None, :] < col_start[:, None] + NA_COLS)
    col_idx = np.clip(kc[None, :] - cols[:, None] + (NA_COLS - 1), 0, 2 * NA_COLS - 2)
    tab = jnp.where(in_win, rpb.astype(F32)[..., col_idx], NEG)
    return jnp.concatenate([tab[:, :, :-1], tab[:, :, 1:]], axis=-1)


def _s5_prep_kernel(are_ref, aim_ref, ldt_ref, bre_ref, bim_ref, lbr_ref, lbi_ref, bbr_ref, bbi_ref):
    are, aim = are_ref[...], aim_ref[...]
    dt = jnp.exp(ldt_ref[...])
    mag = jnp.exp(are * dt)
    lbr, lbi = mag * jnp.cos(aim * dt), mag * jnp.sin(aim * dt)
    lbr_ref[...] = lbr
    lbi_ref[...] = lbi
    nr, ni = lbr - 1.0, lbi
    den = are * are + aim * aim
    cr, ci = (nr * are + ni * aim) / den, (ni * are - nr * aim) / den
    br, bi = bre_ref[...], bim_ref[...]
    bbr_ref[...] = cr * br - ci * bi
    bbi_ref[...] = cr * bi + ci * br


def _s5_prep(a_re, a_im, log_dt, b_re, b_im):
    nd, g, p = a_re.shape
    i = b_re.shape[-1]
    n = nd * g * p
    col = lambda z: z.reshape(n, 1)
    ldt = jnp.broadcast_to(log_dt[:, :, None], (nd, g, p))
    whole = lambda w: pl.BlockSpec((n, w), lambda: (0, 0))
    lbr, lbi, bbr, bbi = pl.pallas_call(
        _s5_prep_kernel,
        in_specs=[whole(1)] * 3 + [whole(i)] * 2,
        out_specs=[whole(1)] * 2 + [whole(i)] * 2,
        out_shape=[jax.ShapeDtypeStruct((n, 1), F32)] * 2 + [jax.ShapeDtypeStruct((n, i), F32)] * 2,
        name="s5_prep",
    )(col(a_re), col(a_im), col(ldt), b_re.reshape(n, i), b_im.reshape(n, i))
    return lbr.reshape(nd, g, p), lbi.reshape(nd, g, p), bbr.reshape(nd, g, p, i), bbi.reshape(nd, g, p, i)


def _s5_block_operands(lbr, lbi, bbr, bbi, c_re, c_im):
    nd, g, p = lbr.shape
    i = bbr.shape[-1]
    nb, gb = SSM_BLOCKS, SSM_BLOCK_GROUPS
    eye = jnp.eye(gb, dtype=F32)

    def in_map(z):
        z = z.reshape(nd, nb, gb, p, i).transpose(0, 1, 2, 4, 3)
        return (z[:, :, :, :, None, :] * eye[None, None, :, None, :, None]).reshape(nd, nb, gb * i, gb * p)

    def out_map(z):
        z = z.reshape(nd, nb, gb, i, p).transpose(0, 1, 2, 4, 3)
        return (z[:, :, :, :, None, :] * eye[None, None, :, None, :, None]).reshape(nd, nb, gb * p, gb * i)

    lam = jnp.stack([lbr.reshape(nd, nb, gb * p), lbi.reshape(nd, nb, gb * p)], axis=2)
    w_in = jnp.concatenate([in_map(bbr), in_map(bbi)], axis=-1).astype(BF16)
    w_out = jnp.concatenate([out_map(c_re.astype(F32)), -out_map(c_im.astype(F32))], axis=2).astype(BF16)
    return lam, w_in, w_out


def _s5_kernel(u_ref, lam_ref, win_ref, wout_ref, y_ref, st_ref, carry_ref, *, nb, tt, reverse):
    ns = SSM_BLOCK_STATE
    rows = tt * nb
    n_pipe = SSM_PIPE_TILES

    @pl.when(pl.program_id(0) == 0)
    def _():
        carry_ref[...] = jnp.zeros_like(carry_ref)

    lr = [jnp.broadcast_to(lam_ref[j, 0:1, :], (nb, ns)) for j in range(SSM_BLOCKS)]
    li = [jnp.broadcast_to(lam_ref[j, 1:2, :], (nb, ns)) for j in range(SSM_BLOCKS)]
    order = list(range(n_pipe))[::-1] if reverse else list(range(n_pipe))
    steps = list(range(tt))[::-1] if reverse else list(range(tt))
    carry = [carry_ref[k] for k in range(2 * SSM_BLOCKS)]
    for i in range(n_pipe + 2):
        if i < n_pipe:
            trows = slice(order[i] * rows, (order[i] + 1) * rows)
            for j in range(SSM_BLOCKS):
                st_ref[i % 3, j] = _dot(u_ref[j, trows, :].astype(BF16), win_ref[j])
        if 1 <= i <= n_pipe:
            buf = (i - 1) % 3
            for t in steps:
                r = slice(t * nb, (t + 1) * nb)
                for j in range(SSM_BLOCKS):
                    sr, si = carry[2 * j], carry[2 * j + 1]
                    nr = lr[j] * sr - li[j] * si + st_ref[buf, j, r, 0:ns]
                    ni = lr[j] * si + li[j] * sr + st_ref[buf, j, r, ns:2 * ns]
                    st_ref[buf, j, r, 0:ns] = nr
                    st_ref[buf, j, r, ns:2 * ns] = ni
                    carry[2 * j], carry[2 * j + 1] = nr, ni
        if i >= 2:
            trows = slice(order[i - 2] * rows, (order[i - 2] + 1) * rows)
            for j in range(SSM_BLOCKS):
                y_ref[j, trows, :] = _dot(st_ref[(i - 2) % 3, j].astype(BF16), wout_ref[j])
    for k in range(2 * SSM_BLOCKS):
        carry_ref[k] = carry[k]


def _s5_scan(u2, lam, w_in, w_out, *, li, direction, nb, n_lat, n_ctx):
    _, rows_total, _ = u2.shape
    tt = SSM_TILE
    span = SSM_PIPE_TILES * tt
    assert n_lat % span == 0 and n_ctx % span == 0
    n_lat_g, n_ctx_g = n_lat // span, n_ctx // span
    n_g = n_lat_g + n_ctx_g
    ns2 = 2 * SSM_BLOCK_STATE
    reverse = direction == 1

    def group(k):
        if reverse:
            return jnp.where(k < n_ctx_g, n_g - 1 - k, n_lat_g - 1 - (k - n_ctx_g))
        return jnp.where(k < n_ctx_g, n_lat_g + k, k - n_ctx_g)

    def par(*tail):
        return pl.BlockSpec((None,) + tail, lambda k: (2 * li + direction,) + (0,) * len(tail))

    stream = pl.BlockSpec((SSM_BLOCKS, span * nb, LANES), lambda k: (0, group(k), 0))
    return pl.pallas_call(
        functools.partial(_s5_kernel, nb=nb, tt=tt, reverse=reverse),
        grid=(n_g,),
        in_specs=[stream, par(SSM_BLOCKS, 2, SSM_BLOCK_STATE), par(SSM_BLOCKS, LANES, ns2),
                  par(SSM_BLOCKS, ns2, LANES)],
        out_specs=stream,
        out_shape=jax.ShapeDtypeStruct((SSM_BLOCKS, rows_total, LANES), F32),
        scratch_shapes=[pltpu.VMEM((3, SSM_BLOCKS, tt * nb, ns2), F32),
                        pltpu.VMEM((2 * SSM_BLOCKS, nb, SSM_BLOCK_STATE), F32)],
        compiler_params=_params(1),
        name="s5_scan_bwd" if reverse else "s5_scan_fwd",
    )(u2, lam, w_in, w_out)


def _merge_kernel(x_ref, mod_ref, g_ref, ysf_ref, ysb_ref, u_ref, ya_ref, yn_ref, d_ref,
                  wgt_ref, wglu_ref, wps_ref, wpa_ref, wpn_ref, wo_ref, o_ref):
    d = x_ref.shape[2]
    tm = x_ref.shape[1]
    nb = u_ref.shape[1] // tm
    xs = [x_ref[sub] for sub in range(TOKEN_BATCH)]
    h = _stacked_mod_norm(xs, g_ref[1:2, :], mod_ref, 3)
    gate_logits = [_dot(h, wgt_ref[:, n * d:(n + 1) * d]) for n in range(N_BRANCH)]
    gate = lambda n: jax.nn.sigmoid(gate_logits[n])

    def from_slabs(load):
        return jnp.concatenate([jnp.concatenate([load(blk, _slab_rows(sub, tm, nb)) for blk in range(SSM_BLOCKS)],
                                                axis=1) for sub in range(TOKEN_BATCH)], axis=0)

    y = from_slabs(lambda blk, rows: ysf_ref[blk, rows, :] + ysb_ref[blk, rows, :])
    y = y + d_ref[...] * from_slabs(lambda blk, rows: u_ref[blk, rows, :])
    t = 0.5 * y * (1.0 + jnp.tanh(math.sqrt(2.0 / math.pi) * (y + 0.044715 * (y * y * y))))
    glu = (t * jax.nn.sigmoid(_dot(t.astype(BF16), wglu_ref[...]))).astype(BF16)
    stacked = lambda ref: jnp.concatenate([ref[sub] for sub in range(TOKEN_BATCH)], axis=0)
    m = (gate(0) * _dot(glu, wps_ref[...]) + gate(1) * _dot(stacked(ya_ref), wpa_ref[...])
         + gate(2) * _dot(stacked(yn_ref), wpn_ref[...]))
    out = _dot(m.astype(BF16), wo_ref[...])
    for sub, x in enumerate(xs):
        o_ref[sub] = x + mod_ref[sub, 5:6, :] * _rows(out, sub, tm)


def _merge(xs, mod, g, ysf, ysb, u2, ya, yn, ssm_d, wgt, wglu, wps, wpa, wpn, wo, *, li, n_tok, n_lat_tiles,
           ctx_group):
    b, _, d = xs.shape
    tm = TOKEN_TILE
    return pl.pallas_call(
        _merge_kernel,
        grid=(n_tok // tm, b // TOKEN_BATCH),
        in_specs=[_tok_spec(d, tm), _mod_spec(d, n_lat_tiles, ctx_group), _resident(g.shape),
                  _slab_spec(tm, b), _slab_spec(tm, b), _slab_spec(tm, b),
                  _tok_spec(GQA_WIDTH, tm), _tok_spec(NA_WIDTH, tm), _resident((1, SSM_WIDTH)),
                  _layer_resident(wgt, li), _layer_resident(wglu, li), _layer_resident(wps, li),
                  _layer_resident(wpa, li), _layer_resident(wpn, li), _layer_resident(wo, li)],
        out_specs=_tok_spec(d, tm),
        out_shape=jax.ShapeDtypeStruct((b, n_tok, d), F32),
        compiler_params=_params(2),
        name="mixer_merge",
    )(xs, mod, g, ysf, ysb, u2, ya, yn, ssm_d.reshape(1, SSM_WIDTH), wgt, wglu, wps, wpa, wpn, wo)


def _rope_tables(l, c_len):
    t = jnp.arange(l)
    pos = jnp.stack([t // GRID_W, t % GRID_W], axis=-1).astype(F32)
    half = HEAD_DIM // 2
    inv = ROPE_THETA ** (-jnp.arange(0, half, 2, dtype=F32) / half)
    ang = pos[:, :, None] * inv
    cos, sin = jnp.cos(ang), jnp.sin(ang)
    zero = jnp.zeros_like(sin[:, 0])
    cos_h = jnp.concatenate([cos[:, 0], cos[:, 0], cos[:, 1], cos[:, 1]], axis=-1)
    sa_h = jnp.concatenate([zero, sin[:, 0], zero, sin[:, 1]], axis=-1)
    sb_h = jnp.concatenate([-sin[:, 0], zero, -sin[:, 1], zero], axis=-1)

    def full(tab, ctx_fill):
        tab = jnp.concatenate([tab, tab], axis=-1)
        return jnp.concatenate([tab, jnp.full((c_len, LANES), ctx_fill, F32)], axis=0)

    return full(cos_h, 1.0), full(sa_h, 0.0), full(sb_h, 0.0)


def kernel(x, c, ctx, c_ctx, w_ada, b_ada, norm_g, ffn1_wg, ffn1_wu, ffn1_wd, ffn2_wg, ffn2_wu, ffn2_wd, w_in,
           ssm_a_re, ssm_a_im, ssm_log_dt, ssm_b_re, ssm_b_im, ssm_c_re, ssm_c_im, ssm_d, ssm_w_glu, gqa_sink,
           na_rpb, w_p_ssm, w_p_gqa, w_p_na, w_out, final_g):
    b, l, d = x.shape
    c_len = ctx.shape[1]
    s = l + c_len
    depth = w_ada.shape[0]
    tm = TOKEN_TILE
    assert b + TOKEN_BATCH <= MOD_ROWS and b % TOKEN_BATCH == 0 and l % tm == 0 and c_len % tm == 0
    assert l % (GRID_W * NA_ROWS) == 0 and c_len % Q_BLOCK == 0 and l % c_len == 0 and b % SUBLANES == 0
    assert l % SSM_TILE == 0 and c_len % SSM_TILE == 0
    n_lat_tiles = l // tm

    cin = jnp.zeros((MOD_ROWS, d), F32).at[:b].set(c).at[b:b + TOKEN_BATCH].set(c_ctx)
    mods = _ada(cin, w_ada, b_ada).reshape(depth, MOD_ROWS, N_MOD, d)
    cos_t, sa_t, sb_t = _rope_tables(l, c_len)

    perm = np.asarray(GQA_HEAD_PERM)
    q0 = SSM_WIDTH
    w_q = w_in[:, :, q0:q0 + GQA_WIDTH].reshape(depth, d, GQA_HEADS, HEAD_DIM)[:, :, perm]
    w_in_p = jnp.concatenate([w_in[:, :, :q0], w_q.reshape(depth, d, GQA_WIDTH),
                              w_in[:, :, q0 + GQA_WIDTH:MIXER_COLS]], axis=-1).astype(BF16)
    w_gt = w_in[:, :, MIXER_COLS:].astype(BF16)
    w_pa = w_p_gqa.reshape(depth, GQA_HEADS, HEAD_DIM, d)[:, perm].reshape(depth, GQA_WIDTH, d).astype(BF16)
    bf = lambda z: z.astype(BF16)
    f1 = (bf(ffn1_wg), bf(ffn1_wu), bf(ffn1_wd))
    f2 = (bf(ffn2_wg), bf(ffn2_wu), bf(ffn2_wd))
    w_glu, w_ps, w_pn, w_o = bf(ssm_w_glu), bf(w_p_ssm), bf(w_p_na), bf(w_out)

    merged = lambda z: z.reshape((depth * 2,) + z.shape[2:])
    lam, s_in, s_out = _s5_block_operands(
        *_s5_prep(merged(ssm_a_re), merged(ssm_a_im), merged(ssm_log_dt), merged(ssm_b_re), merged(ssm_b_im)),
        merged(ssm_c_re), merged(ssm_c_im))
    na_bias = _na_bias_pairs(na_rpb)

    tile_kw = dict(n_lat_tiles=n_lat_tiles, ctx_group=b // TOKEN_BATCH)
    xs = x
    for li in range(depth):
        ctx_out = li < depth - 1
        n_tok = s if ctx_out else l
        mod, g = mods[li], norm_g[li]
        xs, u2, q, k, v, nq, nk, nv = _ffn(xs, mod, g, *f1, li=li, k0=0, gi=0, n_tok=s,
                                           ctx=ctx if li == 0 else None, mixer=(w_in_p, cos_t, sa_t, sb_t), **tile_kw)
        ysf, ysb = (_s5_scan(u2, lam, s_in, s_out, li=li, direction=dd, nb=b, n_lat=l, n_ctx=c_len)
                    for dd in range(2))
        ya = _gqa(gqa_sink[li], q, k, v, n_lat=l // Q_BLOCK, n_ctx_blocks=c_len // Q_BLOCK, ctx_out=ctx_out,
                  c_len=c_len)
        yn = _na(nq, nk, nv, na_bias, li=li, rows=l // GRID_W, n_ctx_rows=c_len // GRID_W, ctx_out=ctx_out,
                 c_len=c_len)
        xs = _merge(xs, mod, g, ysf, ysb, u2, ya, yn, ssm_d[li], w_gt, w_glu, w_ps, w_pa, w_pn, w_o,
                    li=li, n_tok=n_tok, **tile_kw)
        xs = _ffn(xs, mod, g, *f2, li=li, k0=6, gi=2, n_tok=n_tok, final_g=None if ctx_out else final_g, **tile_kw)
    return xs
```

```python
import functools
import math

import jax
import jax.numpy as jnp
import numpy as np
from jax import lax
from jax.experimental import pallas as pl
from jax.experimental.pallas import tpu as pltpu

F32 = jnp.float32
BF16 = jnp.bfloat16

EPS = 1e-6
NEG = -1e30
N_MOD = 9
GRID_W = 64
HEAD_DIM = 64
GQA_HEADS = 8
GQA_KV_HEADS = 2
NA_HEADS = 8
NA_ROWS = 8
NA_COLS = 16
Q_BLOCK = 128
ROPE_THETA = 10000.0
SSM_GROUP = 16
SSM_GROUPS = 24
SSM_STATE = 64
SSM_WIDTH = SSM_GROUP * SSM_GROUPS
GQA_WIDTH = GQA_HEADS * HEAD_DIM
GQA_KV_WIDTH = GQA_KV_HEADS * HEAD_DIM
NA_WIDTH = NA_HEADS * HEAD_DIM
N_BRANCH = 3
MIXER_COLS = SSM_WIDTH + GQA_WIDTH + 2 * GQA_KV_WIDTH + 3 * NA_WIDTH

LANES = 128
SUBLANES = 8
MOD_ROWS = 16
TOKEN_TILE = 256
TOKEN_BATCH = 2
FFN_CHUNK = 256
GQA_QBLOCKS = 2
NA_QROWS = 4
NA_UNION = NA_QROWS + NA_ROWS - 1
SSM_TILE = 32
SSM_PIPE_TILES = 8
SSM_BLOCK_GROUPS = LANES // SSM_GROUP
SSM_BLOCKS = SSM_GROUPS // SSM_BLOCK_GROUPS
SSM_BLOCK_STATE = SSM_BLOCK_GROUPS * SSM_STATE
VMEM_LIMIT = 56 * 1024 * 1024
GQA_HEAD_PERM = (0, 4, 1, 5, 2, 6, 3, 7)


def _params(n_axes):
    return pltpu.CompilerParams(dimension_semantics=("arbitrary",) * n_axes, vmem_limit_bytes=VMEM_LIMIT)


def _resident(shape):
    return pl.BlockSpec(shape, lambda *_: (0,) * len(shape), pipeline_mode=pl.Buffered(1))


def _layer_resident(stack, li):
    nd = stack.ndim
    return pl.BlockSpec((None,) + stack.shape[1:], lambda *_: (li,) + (0,) * (nd - 1),
                        pipeline_mode=pl.Buffered(1))


def _dot(a, b):
    return jnp.dot(a, b, preferred_element_type=F32)


def _dot_t(a, b):
    return lax.dot_general(a, b, (((1,), (1,)), ((), ())), preferred_element_type=F32)


def _rmsnorm(x, g):
    return x * lax.rsqrt(jnp.mean(x * x, axis=-1, keepdims=True) + EPS) * g


def _mod_norm(x, g, shift, scale):
    return _rmsnorm(x, g) * (1.0 + scale) + shift


def _ada_kernel(c_ref, w_ref, b_ref, o_ref):
    s = c_ref[...]
    s = s * jax.nn.sigmoid(s)
    o_ref[0] = _dot(s.astype(BF16), w_ref[0].astype(BF16)) + b_ref[0]


def _ada(cin, w_ada, b_ada):
    depth, d, n = w_ada.shape
    tn = n // 4
    return pl.pallas_call(
        _ada_kernel,
        grid=(depth, n // tn),
        in_specs=[
            pl.BlockSpec((MOD_ROWS, d), lambda l, j: (0, 0)),
            pl.BlockSpec((1, d, tn), lambda l, j: (l, 0, j)),
            pl.BlockSpec((1, 1, tn), lambda l, j: (l, 0, j)),
        ],
        out_specs=pl.BlockSpec((1, MOD_ROWS, tn), lambda l, j: (l, 0, j)),
        out_shape=jax.ShapeDtypeStruct((depth, MOD_ROWS, n), F32),
        compiler_params=_params(2),
        name="ada_mod",
    )(cin, w_ada, b_ada.reshape(depth, 1, n))


def _tok_spec(width, tm):
    return pl.BlockSpec((TOKEN_BATCH, tm, width), lambda j, b: (b, j, 0))


def _rows(val, sub, tm):
    return val[sub * tm:(sub + 1) * tm]


def _slab_rows(sub, tm, nb):
    return pl.ds(pl.program_id(1) * TOKEN_BATCH + sub, tm, stride=nb)


def _slab_spec(tm, nb):
    return pl.BlockSpec((SSM_BLOCKS, tm * nb, LANES), lambda j, b: (0, j, 0))


def _mod_spec(d, n_lat_tiles, ctx_group):
    return pl.BlockSpec((TOKEN_BATCH, N_MOD, d), lambda j, b: (jnp.where(j >= n_lat_tiles, ctx_group, b), 0, 0))


def _stacked_mod_norm(xs, g, mod_ref, shift_row):
    return jnp.concatenate(
        [_mod_norm(x, g, mod_ref[sub, shift_row:shift_row + 1, :], mod_ref[sub, shift_row + 1:shift_row + 2, :])
         for sub, x in enumerate(xs)], axis=0).astype(BF16)


def _ffn_kernel(*refs, k0, gi, final, n_lat_tiles, split_in):
    refs = list(refs)
    x_ref = refs.pop(0)
    ctx_ref = refs.pop(0) if split_in else None
    mod_ref, g_ref, wg_ref, wu_ref, wd_ref = refs[:5]
    fg_ref = refs[5] if final else None
    o_ref, t_ref = refs[-2:]
    tm = x_ref.shape[1]
    xs = [x_ref[sub] for sub in range(TOKEN_BATCH)]
    if split_in:
        is_ctx = pl.program_id(0) >= n_lat_tiles
        xs = [jnp.where(is_ctx, ctx_ref[sub], x) for sub, x in enumerate(xs)]
    h = _stacked_mod_norm(xs, g_ref[gi:gi + 1, :], mod_ref, k0)
    f = wg_ref.shape[1]
    for c0 in range(0, f, FFN_CHUNK):
        a = _dot(h, wg_ref[:, c0:c0 + FFN_CHUNK])
        u = _dot(h, wu_ref[:, c0:c0 + FFN_CHUNK])
        t_ref[:, c0:c0 + FFN_CHUNK] = (a * jax.nn.sigmoid(a) * u).astype(BF16)
    y = _dot(t_ref[...], wd_ref[...])
    for sub, x in enumerate(xs):
        out = x + (0.5 * mod_ref[sub, k0 + 2:k0 + 3, :]) * _rows(y, sub, tm)
        if final:
            out = _rmsnorm(out, fg_ref[...])
        o_ref[sub] = out


def _ffn(xs, mod, g, wg, wu, wd, *, li, k0, gi, n_tok, n_lat_tiles, ctx_group, final_g=None, ctx=None):
    b, _, d = xs.shape
    f = wg.shape[2]
    tm = TOKEN_TILE
    final = final_g is not None
    split_in = ctx is not None
    if split_in:
        in_specs = [pl.BlockSpec((TOKEN_BATCH, tm, d), lambda j, bb: (bb, jnp.minimum(j, n_lat_tiles - 1), 0)),
                    pl.BlockSpec((TOKEN_BATCH, tm, d), lambda j, bb: (bb, jnp.maximum(j - n_lat_tiles, 0), 0))]
        args = [xs, ctx]
    else:
        in_specs = [_tok_spec(d, tm)]
        args = [xs]
    in_specs += [_mod_spec(d, n_lat_tiles, ctx_group), _resident(g.shape),
                 _layer_resident(wg, li), _layer_resident(wu, li), _layer_resident(wd, li)]
    args += [mod, g, wg, wu, wd]
    if final:
        in_specs.append(_resident((1, d)))
        args.append(final_g.reshape(1, d))
    return pl.pallas_call(
        functools.partial(_ffn_kernel, k0=k0, gi=gi, final=final, n_lat_tiles=n_lat_tiles, split_in=split_in),
        grid=(n_tok // tm, b // TOKEN_BATCH),
        in_specs=in_specs,
        out_specs=_tok_spec(d, tm),
        out_shape=jax.ShapeDtypeStruct((b, n_tok, d), F32),
        scratch_shapes=[pltpu.VMEM((TOKEN_BATCH * tm, f), BF16)],
        compiler_params=_params(2),
        name="ffn_half_final" if final else "ffn_half",
    )(*args)


def _inproj_kernel(x_ref, mod_ref, g_ref, w_ref, cos_ref, sa_ref, sb_ref,
                   u_ref, q_ref, k_ref, v_ref, nq_ref, nk_ref, nv_ref):
    tm = x_ref.shape[1]
    nb = u_ref.shape[1] // tm
    h = _stacked_mod_norm([x_ref[sub] for sub in range(TOKEN_BATCH)], g_ref[1:2, :], mod_ref, 3)
    cos, sa, sb = (jnp.concatenate([t[...]] * TOKEN_BATCH, axis=0) for t in (cos_ref, sa_ref, sb_ref))
    scale = HEAD_DIM ** -0.5

    def mm(c0, n):
        return _dot(h, w_ref[:, c0:c0 + n])

    def rope(z):
        return z * cos + pltpu.roll(z, 16, 1) * sa + pltpu.roll(z, LANES - 16, 1) * sb

    def put(ref, val, lanes=slice(None)):
        for sub in range(TOKEN_BATCH):
            ref[sub, :, lanes] = _rows(val, sub, tm)

    c0 = SSM_WIDTH + GQA_WIDTH + 2 * GQA_KV_WIDTH
    z = mm(0, c0)
    col = lambda n: z[:, n * LANES:(n + 1) * LANES]
    n = 0
    for blk in range(SSM_BLOCKS):
        for sub in range(TOKEN_BATCH):
            u_ref[blk, _slab_rows(sub, tm, nb), :] = _rows(col(n), sub, tm)
        n += 1
    for j in range(GQA_WIDTH // LANES):
        put(q_ref, (rope(col(n)) * scale).astype(BF16), slice(j * LANES, (j + 1) * LANES))
        n += 1
    put(k_ref, rope(col(n)).astype(BF16))
    put(v_ref, col(n + 1).astype(BF16))
    put(nq_ref, (mm(c0, NA_WIDTH) * scale).astype(BF16))
    c0 += NA_WIDTH
    put(nk_ref, mm(c0, NA_WIDTH).astype(BF16))
    c0 += NA_WIDTH
    put(nv_ref, mm(c0, NA_WIDTH).astype(BF16))


def _inproj(xs, mod, g, w_in, cos_t, sa_t, sb_t, *, li, n_lat_tiles, ctx_group):
    b, s, d = xs.shape
    tm = TOKEN_TILE
    assert w_in.shape[2] == MIXER_COLS
    widths = (GQA_WIDTH, GQA_KV_WIDTH, GQA_KV_WIDTH, NA_WIDTH, NA_WIDTH, NA_WIDTH)
    dtypes = (BF16,) * len(widths)
    rope_spec = pl.BlockSpec((tm, LANES), lambda j, bb: (j, 0))
    return pl.pallas_call(
        _inproj_kernel,
        grid=(s // tm, b // TOKEN_BATCH),
        in_specs=[_tok_spec(d, tm), _mod_spec(d, n_lat_tiles, ctx_group), _resident(g.shape),
                  _layer_resident(w_in, li), rope_spec, rope_spec, rope_spec],
        out_specs=[_slab_spec(tm, b)] + [_tok_spec(w, tm) for w in widths],
        out_shape=[jax.ShapeDtypeStruct((SSM_BLOCKS, s * b, LANES), F32)]
                  + [jax.ShapeDtypeStruct((b, s, w), dt) for w, dt in zip(widths, dtypes)],
        compiler_params=_params(2),
        name="mixer_in_proj",
    )(xs, mod, g, w_in, cos_t, sa_t, sb_t)


def _softmax_rows(s_ref, p_ref, extra=None):
    s = s_ref[...]
    m = jnp.max(s, axis=1, keepdims=True)
    if extra is not None:
        m = jnp.maximum(m, extra)
    e = jnp.exp(s - m)
    den = jnp.sum(e, axis=1, keepdims=True)
    if extra is not None:
        den = den + jnp.exp(extra - m)
    p_ref[...] = e.astype(BF16)
    return 1.0 / den


def _head_half(qp, low, half):
    return jnp.where(low if half == 0 else jnp.logical_not(low), qp, jnp.zeros_like(qp))


def _gqa_kernel(sink_ref, q_ref, *refs, n_lat):
    nkb = GQA_QBLOCKS + 2
    k_refs, kx_ref = refs[:nkb], refs[nkb]
    v_refs, vx_ref = refs[nkb + 1:2 * nkb + 1], refs[2 * nkb + 1]
    o_ref, s_ref, p_ref = refs[2 * nkb + 2:]
    n_keys = s_ref.shape[1]
    far = 8 * n_keys
    row = lax.broadcasted_iota(jnp.int32, (Q_BLOCK, n_keys), 0)
    col = lax.broadcasted_iota(jnp.int32, (Q_BLOCK, n_keys), 1)
    low = lax.broadcasted_iota(jnp.int32, (Q_BLOCK, LANES), 1) < HEAD_DIM
    oks, keys, vals = [], [], []
    for a in range(GQA_QBLOCKS):
        i = pl.program_id(1) * GQA_QBLOCKS + a
        is_lat = i < n_lat
        off_p = jnp.where(jnp.logical_and(is_lat, i > 0), 0, far)
        off_c = jnp.where(is_lat, 0, far)
        off_n = jnp.where(jnp.logical_and(is_lat, i < n_lat - 1), 0, far)
        oks.append((col >= row + off_p) & (col < Q_BLOCK)
                   | (col >= Q_BLOCK + off_c) & (col < 2 * Q_BLOCK)
                   | (col >= 2 * Q_BLOCK) & (col <= 2 * Q_BLOCK + row - off_n)
                   | (col >= 3 * Q_BLOCK))
        keys.append(jnp.concatenate([k_refs[a][0], k_refs[a + 1][0], k_refs[a + 2][0], kx_ref[0]], axis=0))
        vals.append(jnp.concatenate([v_refs[a][0], v_refs[a + 1][0], v_refs[a + 2][0], vx_ref[0]], axis=0))
    items = [(a, h) for a in range(GQA_QBLOCKS) for h in range(GQA_HEADS)]
    invs, outs = {}, {}
    slot = lambda n: slice((n % 3) * Q_BLOCK, (n % 3 + 1) * Q_BLOCK)
    for n in range(len(items) + 2):
        if n < len(items):
            a, h = items[n]
            qp = q_ref[0, a * Q_BLOCK:(a + 1) * Q_BLOCK, (h // 2) * LANES:(h // 2 + 1) * LANES]
            s_ref[slot(n), :] = jnp.where(oks[a], _dot_t(_head_half(qp, low, h % 2), keys[a]), NEG)
        if 1 <= n <= len(items):
            a, h = items[n - 1]
            sink = jnp.full((Q_BLOCK, 1), sink_ref[GQA_HEAD_PERM[h]], F32)
            invs[a, h] = _softmax_rows(s_ref.at[slot(n - 1)], p_ref.at[slot(n - 1)], sink)
        if n >= 2:
            a, h = items[n - 2]
            outs[a, h] = _dot(p_ref[slot(n - 2), :], vals[a]) * invs[a, h]
            if h % 2 == 1:
                o_ref[0, a * Q_BLOCK:(a + 1) * Q_BLOCK, (h // 2) * LANES:(h // 2 + 1) * LANES] = jnp.where(
                    low, outs[a, h - 1], outs[a, h]).astype(BF16)


def _gqa(sink, q, k, v, *, n_lat, n_ctx_blocks, ctx_out, c_len):
    b = q.shape[0]
    nq = n_lat + (n_ctx_blocks if ctx_out else 0)
    assert n_lat % GQA_QBLOCKS == 0 and n_ctx_blocks % GQA_QBLOCKS == 0
    ctx_idx = n_lat * Q_BLOCK // c_len
    n_keys = 3 * Q_BLOCK + c_len
    qt = GQA_QBLOCKS * Q_BLOCK

    def kv_spec(t):
        return pl.BlockSpec((1, Q_BLOCK, GQA_KV_WIDTH),
                            lambda bb, g: (bb, jnp.clip(g * GQA_QBLOCKS - 1 + t, 0, n_lat - 1), 0))

    kv = [kv_spec(t) for t in range(GQA_QBLOCKS + 2)]
    kv.append(pl.BlockSpec((1, c_len, GQA_KV_WIDTH), lambda bb, g: (bb, ctx_idx, 0)))
    return pl.pallas_call(
        functools.partial(_gqa_kernel, n_lat=n_lat),
        grid=(b, nq // GQA_QBLOCKS),
        in_specs=[pl.BlockSpec(memory_space=pltpu.SMEM),
                  pl.BlockSpec((1, qt, GQA_WIDTH), lambda bb, g: (bb, g, 0))] + kv + kv,
        out_specs=pl.BlockSpec((1, qt, GQA_WIDTH), lambda bb, g: (bb, g, 0)),
        out_shape=jax.ShapeDtypeStruct((b, nq * Q_BLOCK, GQA_WIDTH), BF16),
        scratch_shapes=[pltpu.VMEM((3 * Q_BLOCK, n_keys), F32), pltpu.VMEM((3 * Q_BLOCK, n_keys), BF16)],
        compiler_params=_params(2),
        name="window_gqa",
    )(sink, q, *([k] * (GQA_QBLOCKS + 3)), *([v] * (GQA_QBLOCKS + 3)))


def _na_band_start(r, rows):
    return jnp.clip(r - NA_ROWS // 2, 0, rows - NA_ROWS)


def _na_union_start(g, rows):
    return jnp.minimum(_na_band_start(g * NA_QROWS, rows), rows - NA_UNION)


def _na_kernel(q_ref, kb_ref, vb_ref, kx_ref, vx_ref, bias_ref, o_ref, s_ref, p_ref, *, rows):
    g = pl.program_id(1)
    n_loc = NA_ROWS * GRID_W
    low = lax.broadcasted_iota(jnp.int32, (GRID_W, LANES), 1) < HEAD_DIM
    n_pairs = NA_WIDTH // LANES
    u0 = _na_union_start(g, rows)
    offs, dr0s, pens = [], [], []
    for rho in range(NA_QROWS):
        r = g * NA_QROWS + rho
        is_ctx = r >= rows
        band = _na_band_start(r, rows)
        offs.append(pl.multiple_of(jnp.where(is_ctx, 0, band - u0) * GRID_W, GRID_W))
        dr0s.append(jnp.where(is_ctx, 0, (NA_ROWS - 1) - (r - band)))
        pens.append(jnp.where(is_ctx, NEG, 0.0).astype(F32))
    items = [(rho, j) for rho in range(NA_QROWS) for j in range(n_pairs)]
    rows2 = 2 * GRID_W
    slot = lambda n: slice((n % 3) * rows2, (n % 3 + 1) * rows2)
    invs = {}
    for n in range(len(items) + 2):
        if n < len(items):
            rho, j = items[n]
            sl = slice(j * LANES, (j + 1) * LANES)
            qp = q_ref[0, rho * GRID_W:(rho + 1) * GRID_W, sl]
            keys = jnp.concatenate([kb_ref[0, pl.ds(offs[rho], n_loc), sl], kx_ref[0, :, sl]], axis=0)
            s2 = _dot_t(jnp.concatenate([_head_half(qp, low, 0), _head_half(qp, low, 1)], axis=0), keys)
            bias = jnp.concatenate(
                [jnp.concatenate([bias_ref[2 * j + half, dr0s[rho] + i] for i in range(0, NA_ROWS, 2)], axis=1)
                 for half in range(2)], axis=0)
            s_ref[slot(n), 0:n_loc] = s2[:, 0:n_loc] + (bias + pens[rho])
            s_ref[slot(n), n_loc:] = s2[:, n_loc:]
        if 1 <= n <= len(items):
            invs[n - 1] = _softmax_rows(s_ref.at[slot(n - 1)], p_ref.at[slot(n - 1)])
        if n >= 2:
            rho, j = items[n - 2]
            sl = slice(j * LANES, (j + 1) * LANES)
            vals = jnp.concatenate([vb_ref[0, pl.ds(offs[rho], n_loc), sl], vx_ref[0, :, sl]], axis=0)
            o2 = _dot(p_ref[slot(n - 2), :], vals) * invs[n - 2]
            o_ref[0, rho * GRID_W:(rho + 1) * GRID_W, sl] = jnp.where(low, o2[:GRID_W], o2[GRID_W:]).astype(BF16)


def _na(q, k, v, bias, *, li, rows, n_ctx_rows, ctx_out, c_len):
    b = q.shape[0]
    assert rows % NA_QROWS == 0 and n_ctx_rows % NA_QROWS == 0 and rows >= NA_UNION
    ng = (rows + (n_ctx_rows if ctx_out else 0)) // NA_QROWS
    qt = NA_QROWS * GRID_W
    ctx_idx = rows * GRID_W // c_len
    n_keys = NA_ROWS * GRID_W + c_len
    union_spec = pl.BlockSpec((pl.Element(1), pl.Element(NA_UNION * GRID_W), pl.Element(NA_WIDTH)),
                              lambda bb, g: (bb, _na_union_start(g, rows) * GRID_W, 0))
    ctx_spec = pl.BlockSpec((1, c_len, NA_WIDTH), lambda bb, g: (bb, ctx_idx, 0))
    return pl.pallas_call(
        functools.partial(_na_kernel, rows=rows),
        grid=(b, ng),
        in_specs=[pl.BlockSpec((1, qt, NA_WIDTH), lambda bb, g: (bb, g, 0)),
                  union_spec, union_spec, ctx_spec, ctx_spec, _layer_resident(bias, li)],
        out_specs=pl.BlockSpec((1, qt, NA_WIDTH), lambda bb, g: (bb, g, 0)),
        out_shape=jax.ShapeDtypeStruct((b, ng * qt, NA_WIDTH), BF16),
        scratch_shapes=[pltpu.VMEM((3 * 2 * GRID_W, n_keys), F32), pltpu.VMEM((3 * 2 * GRID_W, n_keys), BF16)],
        compiler_params=_params(2),
        name="neighborhood_attn",
    )(q, k, v, k, v, bias)


def _na_bias_pairs(rpb):
    cols = np.arange(GRID_W)
    col_start = np.clip(cols - NA_COLS // 2, 0, GRID_W - NA_COLS)
    kc = np.arange(GRID_W)
    in_win = (kc[None, :] >= col_start[:, None]) & (kc[None, :] < col_start[:, None] + NA_COLS)
    col_idx = np.clip(kc[None, :] - cols[:, None] + (NA_COLS - 1), 0, 2 * NA_COLS - 2)
    tab = jnp.where(in_win, rpb.astype(F32)[..., col_idx], NEG)
    return jnp.concatenate([tab[:, :, :-1], tab[:, :, 1:]], axis=-1)


def _s5_prep_kernel(are_ref, aim_ref, ldt_ref, bre_ref, bim_ref, lbr_ref, lbi_ref, bbr_ref, bbi_ref):
    are, aim = are_ref[...], aim_ref[...]
    dt = jnp.exp(ldt_ref[...])
    mag = jnp.exp(are * dt)
    lbr, lbi = mag * jnp.cos(aim * dt), mag * jnp.sin(aim * dt)
    lbr_ref[...] = lbr
    lbi_ref[...] = lbi
    nr, ni = lbr - 1.0, lbi
    den = are * are + aim * aim
    cr, ci = (nr * are + ni * aim) / den, (ni * are - nr * aim) / den
    br, bi = bre_ref[...], bim_ref[...]
    bbr_ref[...] = cr * br - ci * bi
    bbi_ref[...] = cr * bi + ci * br


def _s5_prep(a_re, a_im, log_dt, b_re, b_im):
    nd, g, p = a_re.shape
    i = b_re.shape[-1]
    n = nd * g * p
    col = lambda z: z.reshape(n, 1)
    ldt = jnp.broadcast_to(log_dt[:, :, None], (nd, g, p))
    whole = lambda w: pl.BlockSpec((n, w), lambda: (0, 0))
    lbr, lbi, bbr, bbi = pl.pallas_call(
        _s5_prep_kernel,
        in_specs=[whole(1)] * 3 + [whole(i)] * 2,
        out_specs=[whole(1)] * 2 + [whole(i)] * 2,
        out_shape=[jax.ShapeDtypeStruct((n, 1), F32)] * 2 + [jax.ShapeDtypeStruct((n, i), F32)] * 2,
        name="s5_prep",
    )(col(a_re), col(a_im), col(ldt), b_re.reshape(n, i), b_im.reshape(n, i))
    return lbr.reshape(nd, g, p), lbi.reshape(nd, g, p), bbr.reshape(nd, g, p, i), bbi.reshape(nd, g, p, i)


def _s5_block_operands(lbr, lbi, bbr, bbi, c_re, c_im):
    nd, g, p = lbr.shape
    i = bbr.shape[-1]
    nb, gb = SSM_BLOCKS, SSM_BLOCK_GROUPS
    eye = jnp.eye(gb, dtype=F32)

    def in_map(z):
        z = z.reshape(nd, nb, gb, p, i).transpose(0, 1, 2, 4, 3)
        return (z[:, :, :, :, None, :] * eye[None, None, :, None, :, None]).reshape(nd, nb, gb * i, gb * p)

    def out_map(z):
        z = z.reshape(nd, nb, gb, i, p).transpose(0, 1, 2, 4, 3)
        return (z[:, :, :, :, None, :] * eye[None, None, :, None, :, None]).reshape(nd, nb, gb * p, gb * i)

    lam = jnp.stack([lbr.reshape(nd, nb, gb * p), lbi.reshape(nd, nb, gb * p)], axis=2)
    w_in = jnp.concatenate([in_map(bbr), in_map(bbi)], axis=-1).astype(BF16)
    w_out = jnp.concatenate([out_map(c_re.astype(F32)), -out_map(c_im.astype(F32))], axis=2).astype(BF16)
    return lam, w_in, w_out


def _s5_kernel(u_ref, lam_ref, win_ref, wout_ref, y_ref, st_ref, carry_ref, *, nb, tt, reverse):
    ns = SSM_BLOCK_STATE
    rows = tt * nb
    n_pipe = SSM_PIPE_TILES

    @pl.when(pl.program_id(0) == 0)
    def _():
        carry_ref[...] = jnp.zeros_like(carry_ref)

    lr = [jnp.broadcast_to(lam_ref[j, 0:1, :], (nb, ns)) for j in range(SSM_BLOCKS)]
    li = [jnp.broadcast_to(lam_ref[j, 1:2, :], (nb, ns)) for j in range(SSM_BLOCKS)]
    order = list(range(n_pipe))[::-1] if reverse else list(range(n_pipe))
    steps = list(range(tt))[::-1] if reverse else list(range(tt))
    carry = [carry_ref[k] for k in range(2 * SSM_BLOCKS)]
    for i in range(n_pipe + 2):
        if i < n_pipe:
            trows = slice(order[i] * rows, (order[i] + 1) * rows)
            for j in range(SSM_BLOCKS):
                st_ref[i % 3, j] = _dot(u_ref[j, trows, :].astype(BF16), win_ref[j])
        if 1 <= i <= n_pipe:
            buf = (i - 1) % 3
            for t in steps:
                r = slice(t * nb, (t + 1) * nb)
                for j in range(SSM_BLOCKS):
                    sr, si = carry[2 * j], carry[2 * j + 1]
                    nr = lr[j] * sr - li[j] * si + st_ref[buf, j, r, 0:ns]
                    ni = lr[j] * si + li[j] * sr + st_ref[buf, j, r, ns:2 * ns]
                    st_ref[buf, j, r, 0:ns] = nr
                    st_ref[buf, j, r, ns:2 * ns] = ni
                    carry[2 * j], carry[2 * j + 1] = nr, ni
        if i >= 2:
            trows = slice(order[i - 2] * rows, (order[i - 2] + 1) * rows)
            for j in range(SSM_BLOCKS):
                y_ref[j, trows, :] = _dot(st_ref[(i - 2) % 3, j].astype(BF16), wout_ref[j])
    for k in range(2 * SSM_BLOCKS):
        carry_ref[k] = carry[k]


def _s5_scan(u2, lam, w_in, w_out, *, li, direction, nb, n_lat, n_ctx):
    _, rows_total, _ = u2.shape
    tt = SSM_TILE
    span = SSM_PIPE_TILES * tt
    assert n_lat % span == 0 and n_ctx % span == 0
    n_lat_g, n_ctx_g = n_lat // span, n_ctx // span
    n_g = n_lat_g + n_ctx_g
    ns2 = 2 * SSM_BLOCK_STATE
    reverse = direction == 1

    def group(k):
        if reverse:
            return jnp.where(k < n_ctx_g, n_g - 1 - k, n_lat_g - 1 - (k - n_ctx_g))
        return jnp.where(k < n_ctx_g, n_lat_g + k, k - n_ctx_g)

    def par(*tail):
        return pl.BlockSpec((None,) + tail, lambda k: (2 * li + direction,) + (0,) * len(tail))

    stream = pl.BlockSpec((SSM_BLOCKS, span * nb, LANES), lambda k: (0, group(k), 0))
    return pl.pallas_call(
        functools.partial(_s5_kernel, nb=nb, tt=tt, reverse=reverse),
        grid=(n_g,),
        in_specs=[stream, par(SSM_BLOCKS, 2, SSM_BLOCK_STATE), par(SSM_BLOCKS, LANES, ns2),
                  par(SSM_BLOCKS, ns2, LANES)],
        out_specs=stream,
        out_shape=jax.ShapeDtypeStruct((SSM_BLOCKS, rows_total, LANES), F32),
        scratch_shapes=[pltpu.VMEM((3, SSM_BLOCKS, tt * nb, ns2), F32),
                        pltpu.VMEM((2 * SSM_BLOCKS, nb, SSM_BLOCK_STATE), F32)],
        compiler_params=_params(1),
        name="s5_scan_bwd" if reverse else "s5_scan_fwd",
    )(u2, lam, w_in, w_out)


def _merge_kernel(x_ref, mod_ref, g_ref, ysf_ref, ysb_ref, u_ref, ya_ref, yn_ref, d_ref,
                  wgt_ref, wglu_ref, wps_ref, wpa_ref, wpn_ref, wo_ref, o_ref):
    d = x_ref.shape[2]
    tm = x_ref.shape[1]
    nb = u_ref.shape[1] // tm
    xs = [x_ref[sub] for sub in range(TOKEN_BATCH)]
    h = _stacked_mod_norm(xs, g_ref[1:2, :], mod_ref, 3)
    gate_logits = [_dot(h, wgt_ref[:, n * d:(n + 1) * d]) for n in range(N_BRANCH)]
    gate = lambda n: jax.nn.sigmoid(gate_logits[n])

    def from_slabs(load):
        return jnp.concatenate([jnp.concatenate([load(blk, _slab_rows(sub, tm, nb)) for blk in range(SSM_BLOCKS)],
                                                axis=1) for sub in range(TOKEN_BATCH)], axis=0)

    y = from_slabs(lambda blk, rows: ysf_ref[blk, rows, :] + ysb_ref[blk, rows, :])
    y = y + d_ref[...] * from_slabs(lambda blk, rows: u_ref[blk, rows, :])
    t = 0.5 * y * (1.0 + jnp.tanh(math.sqrt(2.0 / math.pi) * (y + 0.044715 * (y * y * y))))
    glu = (t * jax.nn.sigmoid(_dot(t.astype(BF16), wglu_ref[...]))).astype(BF16)
    stacked = lambda ref: jnp.concatenate([ref[sub] for sub in range(TOKEN_BATCH)], axis=0)
    m = (gate(0) * _dot(glu, wps_ref[...]) + gate(1) * _dot(stacked(ya_ref), wpa_ref[...])
         + gate(2) * _dot(stacked(yn_ref), wpn_ref[...]))
    out = _dot(m.astype(BF16), wo_ref[...])
    for sub, x in enumerate(xs):
        o_ref[sub] = x + mod_ref[sub, 5:6, :] * _rows(out, sub, tm)


def _merge(xs, mod, g, ysf, ysb, u2, ya, yn, ssm_d, wgt, wglu, wps, wpa, wpn, wo, *, li, n_tok, n_lat_tiles,
           ctx_group):
    b, _, d = xs.shape
    tm = TOKEN_TILE
    return pl.pallas_call(
        _merge_kernel,
        grid=(n_tok // tm, b // TOKEN_BATCH),
        in_specs=[_tok_spec(d, tm), _mod_spec(d, n_lat_tiles, ctx_group), _resident(g.shape),
                  _slab_spec(tm, b), _slab_spec(tm, b), _slab_spec(tm, b),
                  _tok_spec(GQA_WIDTH, tm), _tok_spec(NA_WIDTH, tm), _resident((1, SSM_WIDTH)),
                  _layer_resident(wgt, li), _layer_resident(wglu, li), _layer_resident(wps, li),
                  _layer_resident(wpa, li), _layer_resident(wpn, li), _layer_resident(wo, li)],
        out_specs=_tok_spec(d, tm),
        out_shape=jax.ShapeDtypeStruct((b, n_tok, d), F32),
        compiler_params=_params(2),
        name="mixer_merge",
    )(xs, mod, g, ysf, ysb, u2, ya, yn, ssm_d.reshape(1, SSM_WIDTH), wgt, wglu, wps, wpa, wpn, wo)


def _rope_tables(l, c_len):
    t = jnp.arange(l)
    pos = jnp.stack([t // GRID_W, t % GRID_W], axis=-1).astype(F32)
    half = HEAD_DIM // 2
    inv = ROPE_THETA ** (-jnp.arange(0, half, 2, dtype=F32) / half)
    ang = pos[:, :, None] * inv
    cos, sin = jnp.cos(ang), jnp.sin(ang)
    zero = jnp.zeros_like(sin[:, 0])
    cos_h = jnp.concatenate([cos[:, 0], cos[:, 0], cos[:, 1], cos[:, 1]], axis=-1)
    sa_h = jnp.concatenate([zero, sin[:, 0], zero, sin[:, 1]], axis=-1)
    sb_h = jnp.concatenate([-sin[:, 0], zero, -sin[:, 1], zero], axis=-1)

    def full(tab, ctx_fill):
        tab = jnp.concatenate([tab, tab], axis=-1)
        return jnp.concatenate([tab, jnp.full((c_len, LANES), ctx_fill, F32)], axis=0)

    return full(cos_h, 1.0), full(sa_h, 0.0), full(sb_h, 0.0)


def kernel(x, c, ctx, c_ctx, w_ada, b_ada, norm_g, ffn1_wg, ffn1_wu, ffn1_wd, ffn2_wg, ffn2_wu, ffn2_wd, w_in,
           ssm_a_re, ssm_a_im, ssm_log_dt, ssm_b_re, ssm_b_im, ssm_c_re, ssm_c_im, ssm_d, ssm_w_glu, gqa_sink,
           na_rpb, w_p_ssm, w_p_gqa, w_p_na, w_out, final_g):
    b, l, d = x.shape
    c_len = ctx.shape[1]
    s = l + c_len
    depth = w_ada.shape[0]
    tm = TOKEN_TILE
    assert b + TOKEN_BATCH <= MOD_ROWS and b % TOKEN_BATCH == 0 and l % tm == 0 and c_len % tm == 0
    assert l % (GRID_W * NA_ROWS) == 0 and c_len % Q_BLOCK == 0 and l % c_len == 0 and b % SUBLANES == 0
    assert l % SSM_TILE == 0 and c_len % SSM_TILE == 0
    n_lat_tiles = l // tm

    cin = jnp.zeros((MOD_ROWS, d), F32).at[:b].set(c).at[b:b + TOKEN_BATCH].set(c_ctx)
    mods = _ada(cin, w_ada, b_ada).reshape(depth, MOD_ROWS, N_MOD, d)
    cos_t, sa_t, sb_t = _rope_tables(l, c_len)

    perm = np.asarray(GQA_HEAD_PERM)
    q0 = SSM_WIDTH
    w_q = w_in[:, :, q0:q0 + GQA_WIDTH].reshape(depth, d, GQA_HEADS, HEAD_DIM)[:, :, perm]
    w_in_p = jnp.concatenate([w_in[:, :, :q0], w_q.reshape(depth, d, GQA_WIDTH),
                              w_in[:, :, q0 + GQA_WIDTH:MIXER_COLS]], axis=-1).astype(BF16)
    w_gt = w_in[:, :, MIXER_COLS:].astype(BF16)
    w_pa = w_p_gqa.reshape(depth, GQA_HEADS, HEAD_DIM, d)[:, perm].reshape(depth, GQA_WIDTH, d).astype(BF16)
    bf = lambda z: z.astype(BF16)
    f1 = (bf(ffn1_wg), bf(ffn1_wu), bf(ffn1_wd))
    f2 = (bf(ffn2_wg), bf(ffn2_wu), bf(ffn2_wd))
    w_glu, w_ps, w_pn, w_o = bf(ssm_w_glu), bf(w_p_ssm), bf(w_p_na), bf(w_out)

    merged = lambda z: z.reshape((depth * 2,) + z.shape[2:])
    lam, s_in, s_out = _s5_block_operands(
        *_s5_prep(merged(ssm_a_re), merged(ssm_a_im), merged(ssm_log_dt), merged(ssm_b_re), merged(ssm_b_im)),
        merged(ssm_c_re), merged(ssm_c_im))
    na_bias = _na_bias_pairs(na_rpb)

    tile_kw = dict(n_lat_tiles=n_lat_tiles, ctx_group=b // TOKEN_BATCH)
    xs = x
    for li in range(depth):
        ctx_out = li < depth - 1
        n_tok = s if ctx_out else l
        mod, g = mods[li], norm_g[li]
        xs = _ffn(xs, mod, g, *f1, li=li, k0=0, gi=0, n_tok=s, ctx=ctx if li == 0 else None, **tile_kw)
        u2, q, k, v, nq, nk, nv = _inproj(xs, mod, g, w_in_p, cos_t, sa_t, sb_t, li=li, **tile_kw)
        ysf, ysb = (_s5_scan(u2, lam, s_in, s_out, li=li, direction=dd, nb=b, n_lat=l, n_ctx=c_len)
                    for dd in range(2))
        ya = _gqa(gqa_sink[li], q, k, v, n_lat=l // Q_BLOCK, n_ctx_blocks=c_len // Q_BLOCK, ctx_out=ctx_out,
                  c_len=c_len)
        yn = _na(nq, nk, nv, na_bias, li=li, rows=l // GRID_W, n_ctx_rows=c_len // GRID_W, ctx_out=ctx_out,
                 c_len=c_len)
        xs = _merge(xs, mod, g, ysf, ysb, u2, ya, yn, ssm_d[li], w_gt, w_glu, w_ps, w_pa, w_pn, w_o,
                    li=li, n_tok=n_tok, **tile_kw)
        xs = _ffn(xs, mod, g, *f2, li=li, k0=6, gi=2, n_tok=n_tok, final_g=None if ctx_out else final_g, **tile_kw)
    return xs
```
